```python
import math
import jax, jax.numpy as jnp
from jax import lax
import numpy as np

D_MODEL = 1024
BATCH = 1
SEQ = 16384
DEPTH = 4

CHUNK = 64
Q_BLOCK = 128
EPS = 1e-6
N_A_LAYERS = DEPTH // 2
N_B_LAYERS = DEPTH - N_A_LAYERS

A_HEADS = 6
A_HEAD_DIM = 128
A_WIDTH = A_HEADS * A_HEAD_DIM
CONV_K = 4

B_HEADS = 6
QK_NOPE = 128
QK_ROPE = 64
V_HEAD = 128
Q_LORA = 256
KV_LORA = 256
B_WIDTH = B_HEADS * V_HEAD
ROPE_THETA = 10000.0

N_MEM = 256
MEM_HEADS = 4
MEM_HEAD_DIM = 64
MEM_WIDTH = MEM_HEADS * MEM_HEAD_DIM

D_FF = 2816

MIX_WIDTH = A_WIDTH + MEM_WIDTH
A_IN = 4 * A_WIDTH + 2 * A_HEADS + MEM_WIDTH
B_IN = Q_LORA + MEM_WIDTH

kernel_name = "hybrid_gdn_mla_yoco_macaron"


def rms_norm(x, g):
    xf = x.astype(jnp.float32)
    y = xf * lax.rsqrt(jnp.mean(xf * xf, axis=-1, keepdims=True) + EPS)
    return (y * g.astype(jnp.float32)).astype(x.dtype)


def l2_norm(x):
    xf = x.astype(jnp.float32)
    return (xf * lax.rsqrt(jnp.sum(xf * xf, axis=-1, keepdims=True) + EPS)).astype(x.dtype)


def swiglu(x, w_gu, w_down):
    gate, up = jnp.split(x @ w_gu, 2, axis=-1)
    return (jax.nn.silu(gate) * up) @ w_down


def rope_tables(positions, dim):
    inv = ROPE_THETA ** (-jnp.arange(0, dim, 2, dtype=jnp.float32) / dim)
    ang = positions.astype(jnp.float32)[..., None] * inv
    return jnp.cos(ang), jnp.sin(ang)


def apply_rope(x, cos, sin):
    xf = x.astype(jnp.float32)
    x1, x2 = jnp.split(xf, 2, axis=-1)
    c, s = cos[:, :, None, :], sin[:, :, None, :]
    return jnp.concatenate([x1 * c - x2 * s, x1 * s + x2 * c], axis=-1).astype(x.dtype)


def causal_depthwise_conv(x, w):
    k = w.shape[0]
    return lax.conv_general_dilated(
        x, w[:, None, :].astype(x.dtype), window_strides=(1,), padding=[(k - 1, 0)],
        dimension_numbers=("NWC", "WIO", "NWC"), feature_group_count=x.shape[-1])


def chunked_gated_delta_rule(q, k, v, beta, g):
    b, s, h, dk = q.shape
    dv = v.shape[-1]
    n = s // CHUNK
    f32 = jnp.float32

    def chunks(t):
        return t.astype(f32).reshape(b, n, CHUNK, h, -1).transpose(0, 3, 1, 2, 4)

    q, k, v = chunks(q), chunks(k), chunks(v)
    beta = chunks(beta[..., None])[..., 0]
    g_cum = jnp.cumsum(chunks(g[..., None])[..., 0], axis=-1)
    causal = jnp.tril(jnp.ones((CHUNK, CHUNK), dtype=bool))
    strict = jnp.tril(jnp.ones((CHUNK, CHUNK), dtype=bool), -1)
    diff = g_cum[..., :, None] - g_cum[..., None, :]
    decay = jnp.where(causal, jnp.exp(jnp.where(causal, diff, 0.0)), 0.0)
    k_beta = k * beta[..., None]
    t_mat = jnp.where(strict, jnp.einsum("bhnid,bhnjd->bhnij", k_beta, k) * decay, 0.0) \
        + jnp.eye(CHUNK, dtype=f32)
    u = lax.linalg.triangular_solve(t_mat, v * beta[..., None], left_side=True,
                                    lower=True, unit_diagonal=True)
    w = lax.linalg.triangular_solve(t_mat, k_beta * jnp.exp(g_cum)[..., None], left_side=True,
                                    lower=True, unit_diagonal=True)
    qk = jnp.einsum("bhnid,bhnjd->bhnij", q, k) * decay
    g_last = g_cum[..., -1:]
    q_dec = q * jnp.exp(g_cum)[..., None]
    k_dec = k * jnp.exp(g_last - g_cum)[..., None]
    chunk_decay = jnp.exp(g_last[..., 0])
    xs = tuple(jnp.moveaxis(t, 2, 0) for t in (q_dec, k_dec, w, u, qk, chunk_decay))

    def step(state, inp):
        q_c, k_c, w_c, u_c, qk_c, d_c = inp
        v_new = u_c - jnp.einsum("bhcd,bhde->bhce", w_c, state)
        out = jnp.einsum("bhcd,bhde->bhce", q_c, state) + jnp.einsum("bhij,bhje->bhie", qk_c, v_new)
        state = state * d_c[..., None, None] + jnp.einsum("bhcd,bhce->bhde", k_c, v_new)
        return state, out

    _, o = lax.scan(step, jnp.zeros((b, h, dk, dv), f32), xs)
    return o.transpose(1, 0, 3, 2, 4).reshape(b, s, h, dv)


def gated_deltanet(qkv, gate, b_raw, a_raw, conv_w, A_log, dt_bias, out_gain):
    b, s, _ = qkv.shape
    qkv = jax.nn.silu(causal_depthwise_conv(qkv, conv_w))
    q, k, v = (t.reshape(b, s, A_HEADS, A_HEAD_DIM) for t in jnp.split(qkv, 3, axis=-1))
    q = l2_norm(q) * A_HEAD_DIM ** -0.5
    k = l2_norm(k)
    beta = jax.nn.sigmoid(b_raw.astype(jnp.float32))
    g = -jnp.exp(A_log.astype(jnp.float32)) * jax.nn.softplus(
        a_raw.astype(jnp.float32) + dt_bias.astype(jnp.float32))
    o = chunked_gated_delta_rule(q, k, v, beta, g).astype(qkv.dtype)
    o = rms_norm(o, out_gain) * jax.nn.silu(gate.reshape(b, s, A_HEADS, A_HEAD_DIM))
    return o.reshape(b, s, A_WIDTH)


def mla_attention(q_nope, q_rope, k_nope, k_rope, v):
    b, s, h, _ = q_nope.shape
    nb = s // Q_BLOCK
    scale = (QK_NOPE + QK_ROPE) ** -0.5
    key_chunk = jnp.arange(s) // CHUNK
    qn = q_nope.reshape(b, nb, Q_BLOCK, h, QK_NOPE).swapaxes(0, 1)
    qr = q_rope.reshape(b, nb, Q_BLOCK, h, QK_ROPE).swapaxes(0, 1)

    def block(args):
        i, qn_b, qr_b = args
        sc = (jnp.einsum("bqhd,bkhd->bhqk", qn_b, k_nope)
              + jnp.einsum("bqhd,bkd->bhqk", qr_b, k_rope)).astype(jnp.float32) * scale
        q_chunk = (i * Q_BLOCK + jnp.arange(Q_BLOCK)) // CHUNK
        mask = key_chunk[None, :] <= q_chunk[:, None]
        p = jax.nn.softmax(jnp.where(mask, sc, -jnp.inf), axis=-1).astype(v.dtype)
        return jnp.einsum("bhqk,bkhd->bqhd", p, v)

    out = lax.map(block, (jnp.arange(nb), qn, qr))
    return out.swapaxes(0, 1).reshape(b, s, h * V_HEAD)


def memory_attention(q, mem_kv):
    b, s, _ = q.shape
    q = q.reshape(b, s, MEM_HEADS, MEM_HEAD_DIM)
    k, v = (t.reshape(b, N_MEM, MEM_HEADS, MEM_HEAD_DIM) for t in jnp.split(mem_kv, 2, axis=-1))
    sc = jnp.einsum("bqhd,bmhd->bhqm", q, k).astype(jnp.float32) * MEM_HEAD_DIM ** -0.5
    p = jax.nn.softmax(sc, axis=-1).astype(v.dtype)
    return jnp.einsum("bhqm,bmhd->bqhd", p, v).reshape(b, s, MEM_WIDTH)


def setup_inputs(seed: int = 0) -> dict:
    key = jax.random.key(seed)
    ks = iter(jax.random.split(key, 40))
    f32 = jnp.float32

    def dense(shape, fan_in):
        return jax.random.normal(next(ks), shape, f32) * fan_in ** -0.5

    def gain(shape):
        return 1.0 + 0.02 * jax.random.normal(next(ks), shape, f32)

    x = jax.random.normal(next(ks), (BATCH, SEQ, D_MODEL), f32)
    mem = jax.random.normal(next(ks), (BATCH, N_MEM, D_MODEL), f32)
    offset = jax.random.randint(next(ks), (BATCH, 1), 0, 64, dtype=jnp.int32) * CHUNK
    positions = (offset + jnp.arange(SEQ, dtype=jnp.int32)[None, :]).astype(jnp.int32)

    ffn1_norm = gain((DEPTH, D_MODEL))
    ffn1_w_gu = dense((DEPTH, D_MODEL, 2 * D_FF), D_MODEL)
    ffn1_w_down = dense((DEPTH, D_FF, D_MODEL), D_FF)
    mix_norm = gain((DEPTH, D_MODEL))
    ffn2_norm = gain((DEPTH, D_MODEL))
    ffn2_w_gu = dense((DEPTH, D_MODEL, 2 * D_FF), D_MODEL)
    ffn2_w_down = dense((DEPTH, D_FF, D_MODEL), D_FF)
    w_out = dense((DEPTH, MIX_WIDTH, D_MODEL), MIX_WIDTH)
    mem_norm = gain((D_MODEL,))
    w_mem_kv = dense((DEPTH, D_MODEL, 2 * MEM_WIDTH), D_MODEL)

    a_w_in = dense((N_A_LAYERS, D_MODEL, A_IN), D_MODEL)
    a_conv = dense((N_A_LAYERS, CONV_K, 3 * A_WIDTH), CONV_K)
    a_A_log = jnp.log(jax.random.uniform(next(ks), (N_A_LAYERS, A_HEADS), f32, 1.0, 16.0))
    dt = jnp.exp(jax.random.uniform(next(ks), (N_A_LAYERS, A_HEADS), f32,
                                    math.log(1e-3), math.log(1e-1)))
    a_dt_bias = dt + jnp.log(-jnp.expm1(-dt))
    a_out_norm = gain((N_A_LAYERS, A_HEAD_DIM))

    b_w_in = dense((N_B_LAYERS, D_MODEL, B_IN), D_MODEL)
    b_q_norm = gain((N_B_LAYERS, Q_LORA))
    b_w_uq = dense((N_B_LAYERS, Q_LORA, B_HEADS * (QK_NOPE + QK_ROPE)), Q_LORA)

    kv_in_norm = gain((D_MODEL,))
    w_dkv = dense((D_MODEL, KV_LORA + QK_ROPE), D_MODEL)
    kv_lat_norm = gain((KV_LORA,))
    w_ukv = dense((KV_LORA, B_HEADS * (QK_NOPE + V_HEAD)), KV_LORA)
    final_norm = gain((D_MODEL,))

    return {"x": x, "mem": mem, "positions": positions,
            "ffn1_norm": ffn1_norm, "ffn1_w_gu": ffn1_w_gu, "ffn1_w_down": ffn1_w_down,
            "mix_norm": mix_norm,
            "ffn2_norm": ffn2_norm, "ffn2_w_gu": ffn2_w_gu, "ffn2_w_down": ffn2_w_down,
            "w_out": w_out, "mem_norm": mem_norm, "w_mem_kv": w_mem_kv,
            "a_w_in": a_w_in, "a_conv": a_conv, "a_A_log": a_A_log, "a_dt_bias": a_dt_bias,
            "a_out_norm": a_out_norm,
            "b_w_in": b_w_in, "b_q_norm": b_q_norm, "b_w_uq": b_w_uq,
            "kv_in_norm": kv_in_norm, "w_dkv": w_dkv, "kv_lat_norm": kv_lat_norm, "w_ukv": w_ukv,
            "final_norm": final_norm}


def reference(x, mem, positions, ffn1_norm, ffn1_w_gu, ffn1_w_down, mix_norm,
              ffn2_norm, ffn2_w_gu, ffn2_w_down, w_out, mem_norm, w_mem_kv,
              a_w_in, a_conv, a_A_log, a_dt_bias, a_out_norm,
              b_w_in, b_q_norm, b_w_uq, kv_in_norm, w_dkv, kv_lat_norm, w_ukv, final_norm):
    b, s, _ = x.shape
    mem_n = rms_norm(mem, mem_norm)
    cos, sin = rope_tables(positions, QK_ROPE)

    for i in range(N_A_LAYERS):
        l = i
        x = x + 0.5 * swiglu(rms_norm(x, ffn1_norm[l]), ffn1_w_gu[l], ffn1_w_down[l])
        h = rms_norm(x, mix_norm[l]) @ a_w_in[i]
        qkv, gate, b_raw, a_raw, q_mem = jnp.split(
            h, [3 * A_WIDTH, 4 * A_WIDTH, 4 * A_WIDTH + A_HEADS, 4 * A_WIDTH + 2 * A_HEADS], axis=-1)
        o_a = gated_deltanet(qkv, gate, b_raw, a_raw, a_conv[i], a_A_log[i], a_dt_bias[i], a_out_norm[i])
        o_m = memory_attention(q_mem, mem_n @ w_mem_kv[l])
        x = x + jnp.concatenate([o_a, o_m], axis=-1) @ w_out[l]
        x = x + 0.5 * swiglu(rms_norm(x, ffn2_norm[l]), ffn2_w_gu[l], ffn2_w_down[l])

    ckr = rms_norm(x, kv_in_norm) @ w_dkv
    c_kv = rms_norm(ckr[..., :KV_LORA], kv_lat_norm)
    k_rope = apply_rope(ckr[..., None, KV_LORA:], cos, sin)[:, :, 0]
    k_nope, v_mla = jnp.split((c_kv @ w_ukv).reshape(b, s, B_HEADS, QK_NOPE + V_HEAD), [QK_NOPE], axis=-1)

    for j in range(N_B_LAYERS):
        l = N_A_LAYERS + j
        x = x + 0.5 * swiglu(rms_norm(x, ffn1_norm[l]), ffn1_w_gu[l], ffn1_w_down[l])
        h = rms_norm(x, mix_norm[l]) @ b_w_in[j]
        cq, q_mem = jnp.split(h, [Q_LORA], axis=-1)
        q = (rms_norm(cq, b_q_norm[j]) @ b_w_uq[j]).reshape(b, s, B_HEADS, QK_NOPE + QK_ROPE)
        q_nope, q_rope = jnp.split(q, [QK_NOPE], axis=-1)
        q_rope = apply_rope(q_rope, cos, sin)
        o_b = mla_attention(q_nope, q_rope, k_nope, k_rope, v_mla)
        o_m = memory_attention(q_mem, mem_n @ w_mem_kv[l])
        x = x + jnp.concatenate([o_b, o_m], axis=-1) @ w_out[l]
        x = x + 0.5 * swiglu(rms_norm(x, ffn2_norm[l]), ffn2_w_gu[l], ffn2_w_down[l])

    return rms_norm(x, final_norm)
```

```python
import functools

import jax
import jax.numpy as jnp
from jax import lax
from jax.experimental import pallas as pl
from jax.experimental.pallas import tpu as pltpu

F32 = jnp.float32
BF16 = jnp.bfloat16

D_MODEL = 1024
SEQ = 16384
DEPTH = 4
CHUNK = 64
EPS = 1e-6
N_A_LAYERS = DEPTH // 2
N_B_LAYERS = DEPTH - N_A_LAYERS

A_HEADS = 6
A_HEAD_DIM = 128
A_WIDTH = A_HEADS * A_HEAD_DIM
CONV_K = 4

B_HEADS = 6
QK_NOPE = 128
QK_ROPE = 64
V_HEAD = 128
Q_LORA = 256
KV_LORA = 256
B_WIDTH = B_HEADS * V_HEAD
ROPE_THETA = 10000.0

N_MEM = 256
MEM_HEADS = 4
MEM_HEAD_DIM = 64
MEM_WIDTH = MEM_HEADS * MEM_HEAD_DIM

D_FF = 2816

LANES = 128
SUBLANES = 8
QK_PAD = 256
VMEM_LIMIT = 56 * 1024 * 1024

FFN_ROWS = 512
FFN_COLS = 512
A_IN_ROWS = 256
B_IN_ROWS = 512
KV_ROWS = 512
ATT_Q = 512
ATT_K = 512
ROPE_ROWS = 2048


def _params(*sem):
    return pltpu.CompilerParams(dimension_semantics=sem, vmem_limit_bytes=VMEM_LIMIT)


def _const_spec(shape):
    n = len(shape)
    return pl.BlockSpec(shape, lambda *_: (0,) * n, pipeline_mode=pl.Buffered(1))


def _rms(x, g):
    return x * lax.rsqrt(jnp.mean(x * x, axis=-1, keepdims=True) + EPS) * g


def _sigmoid(x):
    return 1.0 / (1.0 + jnp.exp(-x))


def _dot(a, b):
    return jnp.dot(a, b, preferred_element_type=F32)


def _dot_nt(a, b):
    return lax.dot_general(a, b, (((1,), (1,)), ((), ())), preferred_element_type=F32)


def _dot_tn(a, b):
    return lax.dot_general(a, b, (((0,), (0,)), ((), ())), preferred_element_type=F32)


def _softmax_rows(s):
    m = jnp.max(s, axis=-1, keepdims=True)
    p = jnp.exp(s - m)
    return p / jnp.sum(p, axis=-1, keepdims=True)


def _mem_attention(qm, kt_ref, vp_ref):
    out = None
    for h in range(MEM_HEADS):
        s = _dot(qm, kt_ref[h]) * (MEM_HEAD_DIM ** -0.5)
        p = _softmax_rows(s).astype(BF16)
        o = _dot(p, vp_ref[h])
        out = o if out is None else out + o
    return out


def _rope_kernel(pos_ref, inv_ref, cos_ref, sin_ref):
    ang = pos_ref[...].astype(F32) * inv_ref[...]
    cos_ref[...] = jnp.cos(ang)
    sin_ref[...] = jnp.sin(ang)


def _rope_tables(pos_col, inv_row):
    s = pos_col.shape[0]
    return pl.pallas_call(
        _rope_kernel,
        grid=(s // ROPE_ROWS,),
        in_specs=[pl.BlockSpec((ROPE_ROWS, 1), lambda i: (i, 0)),
                  pl.BlockSpec((1, LANES), lambda i: (0, 0))],
        out_specs=[pl.BlockSpec((ROPE_ROWS, LANES), lambda i: (i, 0))] * 2,
        out_shape=[jax.ShapeDtypeStruct((s, LANES), F32)] * 2,
        compiler_params=_params("parallel"),
        name="rope_tables",
    )(pos_col, inv_row)


def _mem_kv_kernel(mem_ref, g_ref, w_ref, o_ref):
    mn = _rms(mem_ref[...], g_ref[...]).astype(BF16)
    o_ref[0] = _dot(mn, w_ref[0])


def _mem_kv(mem2d, mem_norm_row, w_mem_kv_bf):
    return pl.pallas_call(
        _mem_kv_kernel,
        grid=(DEPTH,),
        in_specs=[pl.BlockSpec((N_MEM, D_MODEL), lambda l: (0, 0)),
                  pl.BlockSpec((1, D_MODEL), lambda l: (0, 0)),
                  pl.BlockSpec((1, D_MODEL, 2 * MEM_WIDTH), lambda l: (l, 0, 0))],
        out_specs=pl.BlockSpec((1, N_MEM, 2 * MEM_WIDTH), lambda l: (l, 0, 0)),
        out_shape=jax.ShapeDtypeStruct((DEPTH, N_MEM, 2 * MEM_WIDTH), F32),
        compiler_params=_params("parallel"),
        name="mem_kv",
    )(mem2d, mem_norm_row, w_mem_kv_bf)


def _ffn_kernel(*refs, has_proj, has_final):
    refs = list(refs)
    x_ref = refs.pop(0)
    if has_proj:
        oa_ref, om_ref, woa_ref, wom_ref = refs[:4]
        refs = refs[4:]
    g_ref, wg_ref, wu_ref, wd_ref = refs[:4]
    refs = refs[4:]
    if has_final:
        fg_ref = refs.pop(0)
    o_ref, act_ref = refs

    x = x_ref[...]
    if has_proj:
        x = x + _dot(oa_ref[...], woa_ref[...]) + _dot(om_ref[...], wom_ref[...])
    xn = _rms(x, g_ref[...]).astype(BF16)
    for c in range(0, D_FF, FFN_COLS):
        w = min(FFN_COLS, D_FF - c)
        gate = _dot(xn, wg_ref[:, c:c + w])
        up = _dot(xn, wu_ref[:, c:c + w])
        act_ref[:, c:c + w] = (gate * _sigmoid(gate) * up).astype(BF16)
    y = x + 0.5 * _dot(act_ref[...], wd_ref[...])
    if has_final:
        y = _rms(y, fg_ref[...])
    o_ref[...] = y


def _ffn(x, norm_row, wg, wu, wd, proj=None, final_row=None):
    s = x.shape[0]
    row_spec = lambda w: pl.BlockSpec((FFN_ROWS, w), lambda i: (i, 0))
    args, specs = [x], [row_spec(D_MODEL)]
    if proj is not None:
        oa, om, woa, wom = proj
        args += [oa, om, woa, wom]
        specs += [row_spec(oa.shape[1]), row_spec(om.shape[1]),
                  _const_spec(woa.shape), _const_spec(wom.shape)]
    args += [norm_row, wg, wu, wd]
    specs += [_const_spec(norm_row.shape), _const_spec(wg.shape), _const_spec(wu.shape),
              _const_spec(wd.shape)]
    if final_row is not None:
        args.append(final_row)
        specs.append(_const_spec(final_row.shape))
    return pl.pallas_call(
        functools.partial(_ffn_kernel, has_proj=proj is not None, has_final=final_row is not None),
        grid=(s // FFN_ROWS,),
        in_specs=specs,
        out_specs=row_spec(D_MODEL),
        out_shape=jax.ShapeDtypeStruct((s, D_MODEL), F32),
        scratch_shapes=[pltpu.VMEM((FFN_ROWS, D_FF), BF16)],
        compiler_params=_params("parallel"),
        name="ffn",
    )(*args)


def _a_in_kernel(x_ref, g_ref, wqkv_ref, wgate_ref, wba_ref, wqm_ref, conv_ref, alog_ref, dtb_ref,
                 kt_ref, vp_ref,
                 q_ref, k_ref, v_ref, gate_ref, beta_ref, gcum_ref, om_ref, ext_ref):
    tm = A_IN_ROWS
    tail = SUBLANES

    @pl.when(pl.program_id(0) == 0)
    def _():
        ext_ref[0:tail, :] = jnp.zeros((tail, 3 * A_WIDTH), F32)

    xn = _rms(x_ref[...], g_ref[...]).astype(BF16)
    ext_ref[tail:tail + tm, :] = _dot(xn, wqkv_ref[...])
    for b in range(3 * A_HEADS):
        sl = slice(LANES * b, LANES * (b + 1))
        acc = ext_ref[tail:tail + tm, sl] * conv_ref[CONV_K - 1:CONV_K, sl]
        for j in range(1, CONV_K):
            acc = acc + ext_ref[tail - j:tail - j + tm, sl] * conv_ref[CONV_K - 1 - j:CONV_K - j, sl]
        y = acc * _sigmoid(acc)
        if b < 2 * A_HEADS:
            y = y * lax.rsqrt(jnp.sum(y * y, axis=-1, keepdims=True) + EPS)
        if b < A_HEADS:
            q_ref[:, sl] = y * (A_HEAD_DIM ** -0.5)
        elif b < 2 * A_HEADS:
            k_ref[:, LANES * (b - A_HEADS):LANES * (b - A_HEADS + 1)] = y
        else:
            v_ref[:, LANES * (b - 2 * A_HEADS):LANES * (b - 2 * A_HEADS + 1)] = y
    ext_ref[0:tail, :] = ext_ref[tm:tm + tail, :]

    gate = _dot(xn, wgate_ref[...])
    gate_ref[...] = (gate * _sigmoid(gate)).astype(BF16)

    ba = _dot(xn, wba_ref[...])
    beta_ref[...] = _sigmoid(ba[:, :LANES])
    z = ba[:, LANES:] + dtb_ref[...]
    softplus = jnp.maximum(z, 0.0) + jnp.log(1.0 + jnp.exp(-jnp.abs(z)))
    g = -jnp.exp(alog_ref[...]) * softplus
    r = lax.broadcasted_iota(jnp.int32, (tm, tm), 0)
    c = lax.broadcasted_iota(jnp.int32, (tm, tm), 1)
    tri = jnp.where((c <= r) & ((c // CHUNK) == (r // CHUNK)), 1.0, 0.0).astype(BF16)
    g1 = g.astype(BF16)
    g2 = (g - g1.astype(F32)).astype(BF16)
    g3 = (g - g1.astype(F32) - g2.astype(F32)).astype(BF16)
    gcum_ref[...] = _dot(tri, g1) + _dot(tri, g2) + _dot(tri, g3)

    qm = _dot(xn, wqm_ref[...]).astype(BF16)
    om_ref[...] = _mem_attention(qm, kt_ref, vp_ref).astype(BF16)


def _a_in(x, norm_row, wqkv, wgate, wba, wqm, conv_w, alog_row, dtb_row, kt, vp):
    s = x.shape[0]
    tm = A_IN_ROWS
    row_spec = lambda w: pl.BlockSpec((tm, w), lambda i: (i, 0))
    consts = [norm_row, wqkv, wgate, wba, wqm, conv_w, alog_row, dtb_row, kt, vp]
    return pl.pallas_call(
        _a_in_kernel,
        grid=(s // tm,),
        in_specs=[row_spec(D_MODEL)] + [_const_spec(a.shape) for a in consts],
        out_specs=[row_spec(A_WIDTH), row_spec(A_WIDTH), row_spec(A_WIDTH), row_spec(A_WIDTH),
                   row_spec(LANES), row_spec(LANES), row_spec(MEM_WIDTH)],
        out_shape=[jax.ShapeDtypeStruct((s, A_WIDTH), F32), jax.ShapeDtypeStruct((s, A_WIDTH), F32),
                   jax.ShapeDtypeStruct((s, A_WIDTH), F32), jax.ShapeDtypeStruct((s, A_WIDTH), BF16),
                   jax.ShapeDtypeStruct((s, LANES), F32), jax.ShapeDtypeStruct((s, LANES), F32),
                   jax.ShapeDtypeStruct((s, MEM_WIDTH), BF16)],
        scratch_shapes=[pltpu.VMEM((tm + SUBLANES, 3 * A_WIDTH), F32)],
        compiler_params=_params("arbitrary"),
        name="a_in",
    )(x, *consts)


def _gdn_kernel(q_ref, k_ref, v_ref, gate_ref, beta_ref, gcum_ref, gain_ref, o_ref, state_ref):
    @pl.when(pl.program_id(0) == 0)
    def _():
        state_ref[...] = jnp.zeros(state_ref.shape, F32)

    gc_all = gcum_ref[...]
    gc_rows = gc_all.T
    beta_all = beta_ref[...]
    r = lax.broadcasted_iota(jnp.int32, (CHUNK, CHUNK), 0)
    c = lax.broadcasted_iota(jnp.int32, (CHUNK, CHUNK), 1)
    causal = c <= r
    strict = c < r
    for h in range(A_HEADS):
        sl = slice(A_HEAD_DIM * h, A_HEAD_DIM * (h + 1))
        gc = gc_all[:, h:h + 1]
        gcr = gc_rows[h:h + 1, :]
        bt = beta_all[:, h:h + 1]
        decay = jnp.where(causal, jnp.exp(jnp.where(causal, gc - gcr, 0.0)), 0.0)
        g_last = gc[CHUNK - 1:CHUNK, :]
        eg = jnp.exp(gc)
        q = q_ref[:, sl]
        k = k_ref[:, sl]
        v = v_ref[:, sl]
        kb = k * bt
        k_bf = k.astype(BF16)
        a = jnp.where(strict, _dot_nt(kb.astype(BF16), k_bf) * decay, 0.0)
        x = jnp.concatenate([v * bt, kb * eg], axis=1)
        p = a.astype(BF16)
        x = x - _dot(p, x.astype(BF16))
        pw = 2
        while pw < CHUNK:
            a = _dot(p, p)
            p = a.astype(BF16)
            x = x + _dot(p, x.astype(BF16))
            pw *= 2
        u = x[:, :A_HEAD_DIM]
        w = x[:, A_HEAD_DIM:]
        qk = _dot_nt(q.astype(BF16), k_bf) * decay
        st = state_ref[h]
        st_bf = st.astype(BF16)
        v_new = u - _dot(w.astype(BF16), st_bf)
        v_new_bf = v_new.astype(BF16)
        out = _dot((q * eg).astype(BF16), st_bf) + _dot(qk.astype(BF16), v_new_bf)
        k_dec = (k * jnp.exp(g_last - gc)).astype(BF16)
        state_ref[h] = st * jnp.exp(g_last) + _dot_tn(k_dec, v_new_bf)
        o = _rms(out, gain_ref[...]) * gate_ref[:, sl].astype(F32)
        o_ref[:, sl] = o.astype(BF16)


def _gdn(q, k, v, gate, beta, gcum, gain_row):
    s = q.shape[0]
    row_spec = lambda w: pl.BlockSpec((CHUNK, w), lambda i: (i, 0))
    return pl.pallas_call(
        _gdn_kernel,
        grid=(s // CHUNK,),
        in_specs=[row_spec(A_WIDTH)] * 4 + [row_spec(LANES)] * 2 + [_const_spec(gain_row.shape)],
        out_specs=row_spec(A_WIDTH),
        out_shape=jax.ShapeDtypeStruct((s, A_WIDTH), BF16),
        scratch_shapes=[pltpu.VMEM((A_HEADS, A_HEAD_DIM, A_HEAD_DIM), F32)],
        compiler_params=_params("arbitrary"),
        name="gdn",
    )(q, k, v, gate, beta, gcum, gain_row)


def _kv_kernel(x_ref, g_ref, wd_ref, lg_ref, wu_ref, cos_ref, sin_ref, k_ref, v_ref):
    xn = _rms(x_ref[...], g_ref[...]).astype(BF16)
    ckr = _dot(xn, wd_ref[...])
    cn = _rms(ckr[:, :KV_LORA], lg_ref[...]).astype(BF16)
    kr = ckr[:, KV_LORA:KV_LORA + LANES] * cos_ref[...] + ckr[:, KV_LORA + LANES:] * sin_ref[...]
    kr = kr.astype(BF16)
    kv = _dot(cn, wu_ref[...])
    for h in range(B_HEADS):
        k_ref[h, :, :QK_NOPE] = kv[:, QK_NOPE * h:QK_NOPE * (h + 1)].astype(BF16)
        k_ref[h, :, QK_NOPE:] = kr
        v_ref[h] = kv[:, B_HEADS * QK_NOPE + V_HEAD * h:B_HEADS * QK_NOPE + V_HEAD * (h + 1)].astype(BF16)


def _kv(x, norm_row, wd, lat_row, wu, cos_t, sin_t):
    s = x.shape[0]
    tm = KV_ROWS
    consts1 = [norm_row, wd, lat_row, wu]
    return pl.pallas_call(
        _kv_kernel,
        grid=(s // tm,),
        in_specs=[pl.BlockSpec((tm, D_MODEL), lambda i: (i, 0))]
        + [_const_spec(a.shape) for a in consts1]
        + [pl.BlockSpec((tm, LANES), lambda i: (i, 0))] * 2,
        out_specs=[pl.BlockSpec((B_HEADS, tm, QK_PAD), lambda i: (0, i, 0)),
                   pl.BlockSpec((B_HEADS, tm, V_HEAD), lambda i: (0, i, 0))],
        out_shape=[jax.ShapeDtypeStruct((B_HEADS, s, QK_PAD), BF16),
                   jax.ShapeDtypeStruct((B_HEADS, s, V_HEAD), BF16)],
        compiler_params=_params("parallel"),
        name="mla_kv",
    )(x, *consts1, cos_t, sin_t)


def _b_in_kernel(x_ref, g_ref, win_ref, qg_ref, wuq_ref, wrot_ref, cos_ref, sin_ref, kt_ref, vp_ref,
                 q_ref, om_ref):
    scale = (QK_NOPE + QK_ROPE) ** -0.5
    xn = _rms(x_ref[...], g_ref[...]).astype(BF16)
    h_in = _dot(xn, win_ref[...])
    cqn = _rms(h_in[:, :Q_LORA], qg_ref[...]).astype(BF16)
    qa = _dot(cqn, wuq_ref[...])
    qb = _dot(cqn, wrot_ref[...])
    cos_t = cos_ref[...]
    sin_t = sin_ref[...]
    for h in range(B_HEADS):
        q_ref[h, :, :QK_NOPE] = (qa[:, QK_PAD * h:QK_PAD * h + QK_NOPE] * scale).astype(BF16)
        hi = qa[:, QK_PAD * h + QK_NOPE:QK_PAD * (h + 1)] * cos_t + qb[:, LANES * h:LANES * (h + 1)] * sin_t
        q_ref[h, :, QK_NOPE:] = (hi * scale).astype(BF16)
    qm = h_in[:, Q_LORA:].astype(BF16)
    om_ref[...] = _mem_attention(qm, kt_ref, vp_ref).astype(BF16)


def _b_in(x, norm_row, win, qg_row, wuq, wrot, cos_t, sin_t, kt, vp):
    s = x.shape[0]
    tm = B_IN_ROWS
    consts1 = [norm_row, win, qg_row, wuq, wrot]
    consts2 = [kt, vp]
    return pl.pallas_call(
        _b_in_kernel,
        grid=(s // tm,),
        in_specs=[pl.BlockSpec((tm, D_MODEL), lambda i: (i, 0))]
        + [_const_spec(a.shape) for a in consts1]
        + [pl.BlockSpec((tm, LANES), lambda i: (i, 0))] * 2
        + [_const_spec(a.shape) for a in consts2],
        out_specs=[pl.BlockSpec((B_HEADS, tm, QK_PAD), lambda i: (0, i, 0)),
                   pl.BlockSpec((tm, MEM_WIDTH), lambda i: (i, 0))],
        out_shape=[jax.ShapeDtypeStruct((B_HEADS, s, QK_PAD), BF16),
                   jax.ShapeDtypeStruct((s, MEM_WIDTH), BF16)],
        compiler_params=_params("parallel"),
        name="b_in",
    )(x, *consts1, cos_t, sin_t, *consts2)


def _attn_kernel(q_ref, k_ref, v_ref, o_ref):
    i = pl.program_id(1)
    q = q_ref[0]

    def block(j, carry, masked):
        m, l, acc = carry
        start = pl.multiple_of(j * ATT_K, ATT_K)
        kb = k_ref[0, pl.ds(start, ATT_K), :]
        vb = v_ref[0, pl.ds(start, ATT_K), :]
        s = _dot_nt(q, kb)
        if masked:
            r = lax.broadcasted_iota(jnp.int32, (ATT_Q, ATT_K), 0)
            c = lax.broadcasted_iota(jnp.int32, (ATT_Q, ATT_K), 1)
            s = jnp.where((c // CHUNK) <= (r // CHUNK), s, -jnp.inf)
        m_new = jnp.maximum(m, jnp.max(s, axis=-1, keepdims=True))
        alpha = jnp.exp(m - m_new)
        p = jnp.exp(s - m_new)
        l = alpha * l + jnp.sum(p, axis=-1, keepdims=True)
        acc = alpha * acc + _dot(p.astype(BF16), vb)
        return m_new, l, acc

    init = (jnp.full((ATT_Q, 1), -jnp.inf, F32), jnp.zeros((ATT_Q, 1), F32),
            jnp.zeros((ATT_Q, V_HEAD), F32))
    carry = lax.fori_loop(0, i, lambda j, cr: block(j, cr, False), init)
    m, l, acc = block(i, carry, True)
    o_ref[...] = (acc / l).astype(BF16)


def _attn(q, k, v):
    h, s, _ = q.shape
    assert ATT_Q == ATT_K and ATT_Q % CHUNK == 0
    return pl.pallas_call(
        _attn_kernel,
        grid=(h, s // ATT_Q),
        in_specs=[pl.BlockSpec((1, ATT_Q, QK_PAD), lambda hh, i: (hh, i, 0)),
                  pl.BlockSpec((1, s, QK_PAD), lambda hh, i: (hh, 0, 0)),
                  pl.BlockSpec((1, s, V_HEAD), lambda hh, i: (hh, 0, 0))],
        out_specs=pl.BlockSpec((ATT_Q, V_HEAD), lambda hh, i: (i, hh)),
        out_shape=jax.ShapeDtypeStruct((s, h * V_HEAD), BF16),
        compiler_params=_params("parallel", "arbitrary"),
        name="mla_attn",
    )(q, k, v)


def _rot_cols(w):
    half = w.shape[-1] // 2
    return jnp.concatenate([-w[..., half:], w[..., :half]], axis=-1)


def _pad_cols(w, width):
    return jnp.pad(w, [(0, 0)] * (w.ndim - 1) + [(0, width - w.shape[-1])])


def _mem_layout(mem_kv_l):
    k = mem_kv_l[:, :MEM_WIDTH]
    v = mem_kv_l[:, MEM_WIDTH:]
    head_of = jnp.arange(MEM_WIDTH) // MEM_HEAD_DIM
    sel = (head_of[None, :] == jnp.arange(MEM_HEADS)[:, None]).astype(F32)
    kt = (k.T[None, :, :] * sel[:, :, None]).astype(BF16)
    vp = (v[None, :, :] * sel[:, None, :]).astype(BF16)
    return kt, vp


def kernel(x, mem, positions, ffn1_norm, ffn1_w_gu, ffn1_w_down, mix_norm, ffn2_norm, ffn2_w_gu,
           ffn2_w_down, w_out, mem_norm, w_mem_kv, a_w_in, a_conv, a_A_log, a_dt_bias, a_out_norm,
           b_w_in, b_q_norm, b_w_uq, kv_in_norm, w_dkv, kv_lat_norm, w_ukv, final_norm):
    b, s, d = x.shape
    assert (b, s, d) == (1, SEQ, D_MODEL)
    xs = x.reshape(s, d)
    row = lambda v: v.reshape(1, -1).astype(F32)

    inv = ROPE_THETA ** (-jnp.arange(0, QK_ROPE, 2, dtype=F32) / QK_ROPE)
    inv_row = _pad_cols(jnp.concatenate([inv, inv])[None, :], LANES)
    cos_t, sin_t = _rope_tables(positions.reshape(s, 1), inv_row)

    mem_kv_all = _mem_kv(mem.reshape(N_MEM, d), row(mem_norm), w_mem_kv.astype(BF16))

    def ffn_weights(w_gu, w_down):
        return w_gu[:, :D_FF].astype(BF16), w_gu[:, D_FF:].astype(BF16), w_down.astype(BF16)

    def out_weights(l):
        return w_out[l, :A_WIDTH].astype(BF16), w_out[l, A_WIDTH:].astype(BF16)

    for i in range(N_A_LAYERS):
        l = i
        xs = _ffn(xs, row(ffn1_norm[l]), *ffn_weights(ffn1_w_gu[l], ffn1_w_down[l]))
        w_in = a_w_in[i]
        wqkv = w_in[:, :3 * A_WIDTH].astype(BF16)
        wgate = w_in[:, 3 * A_WIDTH:4 * A_WIDTH].astype(BF16)
        wb = w_in[:, 4 * A_WIDTH:4 * A_WIDTH + A_HEADS]
        wa = w_in[:, 4 * A_WIDTH + A_HEADS:4 * A_WIDTH + 2 * A_HEADS]
        wba = jnp.concatenate([_pad_cols(wb, LANES), _pad_cols(wa, LANES)], axis=1).astype(BF16)
        wqm = w_in[:, 4 * A_WIDTH + 2 * A_HEADS:].astype(BF16)
        kt, vp = _mem_layout(mem_kv_all[l])
        q, k, v, gate, beta, gcum, o_m = _a_in(
            xs, row(mix_norm[l]), wqkv, wgate, wba, wqm, a_conv[i].astype(F32),
            _pad_cols(row(a_A_log[i]), LANES), _pad_cols(row(a_dt_bias[i]), LANES), kt, vp)
        o_a = _gdn(q, k, v, gate, beta, gcum, row(a_out_norm[i]))
        xs = _ffn(xs, row(ffn2_norm[l]), *ffn_weights(ffn2_w_gu[l], ffn2_w_down[l]),
                  proj=(o_a, o_m, *out_weights(l)))

    w_c = w_dkv[:, :KV_LORA]
    w_r = w_dkv[:, KV_LORA:]
    wd = jnp.concatenate([w_c, _pad_cols(w_r, LANES), _pad_cols(_rot_cols(w_r), LANES)], axis=1).astype(BF16)
    w_ukv3 = w_ukv.reshape(KV_LORA, B_HEADS, QK_NOPE + V_HEAD)
    wu = jnp.concatenate([w_ukv3[:, :, :QK_NOPE].reshape(KV_LORA, B_HEADS * QK_NOPE),
                          w_ukv3[:, :, QK_NOPE:].reshape(KV_LORA, B_HEADS * V_HEAD)], axis=1).astype(BF16)
    k_all, v_all = _kv(xs, row(kv_in_norm), wd, row(kv_lat_norm), wu, cos_t, sin_t)

    for j in range(N_B_LAYERS):
        l = N_A_LAYERS + j
        xs = _ffn(xs, row(ffn1_norm[l]), *ffn_weights(ffn1_w_gu[l], ffn1_w_down[l]))
        w_uq3 = b_w_uq[j].reshape(Q_LORA, B_HEADS, QK_NOPE + QK_ROPE)
        wuq = _pad_cols(w_uq3, QK_PAD).reshape(Q_LORA, B_HEADS * QK_PAD).astype(BF16)
        wrot = _pad_cols(_rot_cols(w_uq3[:, :, QK_NOPE:]), LANES).reshape(Q_LORA, B_HEADS * LANES).astype(BF16)
        kt, vp = _mem_layout(mem_kv_all[l])
        q_all, o_m = _b_in(xs, row(mix_norm[l]), b_w_in[j].astype(BF16), row(b_q_norm[j]), wuq, wrot,
                           cos_t, sin_t, kt, vp)
        o_b = _attn(q_all, k_all, v_all)
        last = j == N_B_LAYERS - 1
        xs = _ffn(xs, row(ffn2_norm[l]), *ffn_weights(ffn2_w_gu[l], ffn2_w_down[l]),
                  proj=(o_b, o_m, *out_weights(l)), final_row=row(final_norm) if last else None)

    return xs.reshape(b, s, d)
```

```python
import functools

import jax
import jax.numpy as jnp
from jax import lax
from jax.experimental import pallas as pl
from jax.experimental.pallas import tpu as pltpu

F32 = jnp.float32
BF16 = jnp.bfloat16

D_MODEL = 1024
SEQ = 16384
DEPTH = 4
CHUNK = 64
EPS = 1e-6
N_A_LAYERS = DEPTH // 2
N_B_LAYERS = DEPTH - N_A_LAYERS

A_HEADS = 6
A_HEAD_DIM = 128
A_WIDTH = A_HEADS * A_HEAD_DIM
CONV_K = 4

B_HEADS = 6
QK_NOPE = 128
QK_ROPE = 64
V_HEAD = 128
Q_LORA = 256
KV_LORA = 256
B_WIDTH = B_HEADS * V_HEAD
ROPE_THETA = 10000.0

N_MEM = 256
MEM_HEADS = 4
MEM_HEAD_DIM = 64
MEM_WIDTH = MEM_HEADS * MEM_HEAD_DIM

D_FF = 2816

LANES = 128
SUBLANES = 8
QK_PAD = 256
VMEM_LIMIT = 56 * 1024 * 1024

FFN_ROWS = 512
FFN_COLS = 512
A_IN_ROWS = 256
B_IN_ROWS = 512
KV_ROWS = 512
ATT_Q = 1024
ATT_K = 512
V_PAD = 256
LOG2E = 1.4426950408889634
ROPE_ROWS = 2048


def _params(*sem):
    return pltpu.CompilerParams(dimension_semantics=sem, vmem_limit_bytes=VMEM_LIMIT)


def _const_spec(shape):
    n = len(shape)
    return pl.BlockSpec(shape, lambda *_: (0,) * n, pipeline_mode=pl.Buffered(1))


def _rms(x, g):
    return x * lax.rsqrt(jnp.mean(x * x, axis=-1, keepdims=True) + EPS) * g


def _sigmoid(x):
    return 1.0 / (1.0 + jnp.exp(-x))


def _dot(a, b):
    return jnp.dot(a, b, preferred_element_type=F32)


def _dot_nt(a, b):
    return lax.dot_general(a, b, (((1,), (1,)), ((), ())), preferred_element_type=F32)


def _dot_tn(a, b):
    return lax.dot_general(a, b, (((0,), (0,)), ((), ())), preferred_element_type=F32)


def _softmax_rows(s):
    m = jnp.max(s, axis=-1, keepdims=True)
    p = jnp.exp(s - m)
    return p / jnp.sum(p, axis=-1, keepdims=True)


def _mem_attention(qm, kt_ref, vp_ref):
    out = None
    for h in range(MEM_HEADS):
        s = _dot(qm, kt_ref[h]) * (MEM_HEAD_DIM ** -0.5)
        p = _softmax_rows(s).astype(BF16)
        o = _dot(p, vp_ref[h])
        out = o if out is None else out + o
    return out


def _rope_kernel(pos_ref, inv_ref, cos_ref, sin_ref):
    ang = pos_ref[...].astype(F32) * inv_ref[...]
    cos_ref[...] = jnp.cos(ang)
    sin_ref[...] = jnp.sin(ang)


def _rope_tables(pos_col, inv_row):
    s = pos_col.shape[0]
    return pl.pallas_call(
        _rope_kernel,
        grid=(s // ROPE_ROWS,),
        in_specs=[pl.BlockSpec((ROPE_ROWS, 1), lambda i: (i, 0)),
                  pl.BlockSpec((1, LANES), lambda i: (0, 0))],
        out_specs=[pl.BlockSpec((ROPE_ROWS, LANES), lambda i: (i, 0))] * 2,
        out_shape=[jax.ShapeDtypeStruct((s, LANES), F32)] * 2,
        compiler_params=_params("parallel"),
        name="rope_tables",
    )(pos_col, inv_row)


def _mem_kv_kernel(mem_ref, g_ref, w_ref, o_ref):
    mn = _rms(mem_ref[...], g_ref[...]).astype(BF16)
    o_ref[0] = _dot(mn, w_ref[0])


def _mem_kv(mem2d, mem_norm_row, w_mem_kv_bf):
    return pl.pallas_call(
        _mem_kv_kernel,
        grid=(DEPTH,),
        in_specs=[pl.BlockSpec((N_MEM, D_MODEL), lambda l: (0, 0)),
                  pl.BlockSpec((1, D_MODEL), lambda l: (0, 0)),
                  pl.BlockSpec((1, D_MODEL, 2 * MEM_WIDTH), lambda l: (l, 0, 0))],
        out_specs=pl.BlockSpec((1, N_MEM, 2 * MEM_WIDTH), lambda l: (l, 0, 0)),
        out_shape=jax.ShapeDtypeStruct((DEPTH, N_MEM, 2 * MEM_WIDTH), F32),
        compiler_params=_params("parallel"),
        name="mem_kv",
    )(mem2d, mem_norm_row, w_mem_kv_bf)


def _ffn_kernel(*refs, has_proj, has_final):
    refs = list(refs)
    x_ref = refs.pop(0)
    if has_proj:
        oa_ref, om_ref, woa_ref, wom_ref = refs[:4]
        refs = refs[4:]
    g_ref, wg_ref, wu_ref, wd_ref = refs[:4]
    refs = refs[4:]
    if has_final:
        fg_ref = refs.pop(0)
    o_ref, act_ref = refs

    x = x_ref[...]
    if has_proj:
        x = x + _dot(oa_ref[...], woa_ref[...]) + _dot(om_ref[...], wom_ref[...])
    xn = _rms(x, g_ref[...]).astype(BF16)
    for c in range(0, D_FF, FFN_COLS):
        w = min(FFN_COLS, D_FF - c)
        gate = _dot(xn, wg_ref[:, c:c + w])
        up = _dot(xn, wu_ref[:, c:c + w])
        act_ref[:, c:c + w] = (gate * _sigmoid(gate) * up).astype(BF16)
    y = x + 0.5 * _dot(act_ref[...], wd_ref[...])
    if has_final:
        y = _rms(y, fg_ref[...])
    o_ref[...] = y


def _ffn(x, norm_row, wg, wu, wd, proj=None, final_row=None):
    s = x.shape[0]
    row_spec = lambda w: pl.BlockSpec((FFN_ROWS, w), lambda i: (i, 0))
    args, specs = [x], [row_spec(D_MODEL)]
    if proj is not None:
        oa, om, woa, wom = proj
        args += [oa, om, woa, wom]
        specs += [row_spec(oa.shape[1]), row_spec(om.shape[1]),
                  _const_spec(woa.shape), _const_spec(wom.shape)]
    args += [norm_row, wg, wu, wd]
    specs += [_const_spec(norm_row.shape), _const_spec(wg.shape), _const_spec(wu.shape),
              _const_spec(wd.shape)]
    if final_row is not None:
        args.append(final_row)
        specs.append(_const_spec(final_row.shape))
    return pl.pallas_call(
        functools.partial(_ffn_kernel, has_proj=proj is not None, has_final=final_row is not None),
        grid=(s // FFN_ROWS,),
        in_specs=specs,
        out_specs=row_spec(D_MODEL),
        out_shape=jax.ShapeDtypeStruct((s, D_MODEL), F32),
        scratch_shapes=[pltpu.VMEM((FFN_ROWS, D_FF), BF16)],
        compiler_params=_params("parallel"),
        name="ffn",
    )(*args)


def _a_in_kernel(x_ref, g_ref, wqkv_ref, wgate_ref, wba_ref, wqm_ref, conv_ref, alog_ref, dtb_ref,
                 kt_ref, vp_ref,
                 q_ref, k_ref, v_ref, gate_ref, beta_ref, gcum_ref, om_ref, ext_ref):
    tm = A_IN_ROWS
    tail = SUBLANES

    @pl.when(pl.program_id(0) == 0)
    def _():
        ext_ref[0:tail, :] = jnp.zeros((tail, 3 * A_WIDTH), F32)

    xn = _rms(x_ref[...], g_ref[...]).astype(BF16)
    ext_ref[tail:tail + tm, :] = _dot(xn, wqkv_ref[...])
    for b in range(3 * A_HEADS):
        sl = slice(LANES * b, LANES * (b + 1))
        acc = ext_ref[tail:tail + tm, sl] * conv_ref[CONV_K - 1:CONV_K, sl]
        for j in range(1, CONV_K):
            acc = acc + ext_ref[tail - j:tail - j + tm, sl] * conv_ref[CONV_K - 1 - j:CONV_K - j, sl]
        y = acc * _sigmoid(acc)
        if b < 2 * A_HEADS:
            y = y * lax.rsqrt(jnp.sum(y * y, axis=-1, keepdims=True) + EPS)
        if b < A_HEADS:
            q_ref[:, sl] = y * (A_HEAD_DIM ** -0.5)
        elif b < 2 * A_HEADS:
            k_ref[:, LANES * (b - A_HEADS):LANES * (b - A_HEADS + 1)] = y
        else:
            v_ref[:, LANES * (b - 2 * A_HEADS):LANES * (b - 2 * A_HEADS + 1)] = y
    ext_ref[0:tail, :] = ext_ref[tm:tm + tail, :]

    gate = _dot(xn, wgate_ref[...])
    gate_ref[...] = (gate * _sigmoid(gate)).astype(BF16)

    ba = _dot(xn, wba_ref[...])
    beta_ref[...] = _sigmoid(ba[:, :LANES])
    z = ba[:, LANES:] + dtb_ref[...]
    softplus = jnp.maximum(z, 0.0) + jnp.log(1.0 + jnp.exp(-jnp.abs(z)))
    g = -jnp.exp(alog_ref[...]) * softplus
    r = lax.broadcasted_iota(jnp.int32, (tm, tm), 0)
    c = lax.broadcasted_iota(jnp.int32, (tm, tm), 1)
    tri = jnp.where((c <= r) & ((c // CHUNK) == (r // CHUNK)), 1.0, 0.0).astype(BF16)
    g1 = g.astype(BF16)
    g2 = (g - g1.astype(F32)).astype(BF16)
    g3 = (g - g1.astype(F32) - g2.astype(F32)).astype(BF16)
    gcum_ref[...] = _dot(tri, g1) + _dot(tri, g2) + _dot(tri, g3)

    qm = _dot(xn, wqm_ref[...]).astype(BF16)
    om_ref[...] = _mem_attention(qm, kt_ref, vp_ref).astype(BF16)


def _a_in(x, norm_row, wqkv, wgate, wba, wqm, conv_w, alog_row, dtb_row, kt, vp):
    s = x.shape[0]
    tm = A_IN_ROWS
    row_spec = lambda w: pl.BlockSpec((tm, w), lambda i: (i, 0))
    consts = [norm_row, wqkv, wgate, wba, wqm, conv_w, alog_row, dtb_row, kt, vp]
    return pl.pallas_call(
        _a_in_kernel,
        grid=(s // tm,),
        in_specs=[row_spec(D_MODEL)] + [_const_spec(a.shape) for a in consts],
        out_specs=[row_spec(A_WIDTH), row_spec(A_WIDTH), row_spec(A_WIDTH), row_spec(A_WIDTH),
                   row_spec(LANES), row_spec(LANES), row_spec(MEM_WIDTH)],
        out_shape=[jax.ShapeDtypeStruct((s, A_WIDTH), F32), jax.ShapeDtypeStruct((s, A_WIDTH), F32),
                   jax.ShapeDtypeStruct((s, A_WIDTH), F32), jax.ShapeDtypeStruct((s, A_WIDTH), BF16),
                   jax.ShapeDtypeStruct((s, LANES), F32), jax.ShapeDtypeStruct((s, LANES), F32),
                   jax.ShapeDtypeStruct((s, MEM_WIDTH), BF16)],
        scratch_shapes=[pltpu.VMEM((tm + SUBLANES, 3 * A_WIDTH), F32)],
        compiler_params=_params("arbitrary"),
        name="a_in",
    )(x, *consts)


def _gdn_kernel(q_ref, k_ref, v_ref, gate_ref, beta_ref, gcum_ref, gain_ref, o_ref, state_ref):
    @pl.when(pl.program_id(0) == 0)
    def _():
        state_ref[...] = jnp.zeros(state_ref.shape, F32)

    gc_all = gcum_ref[...]
    gc_rows = gc_all.T
    beta_all = beta_ref[...]
    r = lax.broadcasted_iota(jnp.int32, (CHUNK, CHUNK), 0)
    c = lax.broadcasted_iota(jnp.int32, (CHUNK, CHUNK), 1)
    causal = c <= r
    strict = c < r
    for h in range(A_HEADS):
        sl = slice(A_HEAD_DIM * h, A_HEAD_DIM * (h + 1))
        gc = gc_all[:, h:h + 1]
        gcr = gc_rows[h:h + 1, :]
        bt = beta_all[:, h:h + 1]
        decay = jnp.where(causal, jnp.exp(jnp.where(causal, gc - gcr, 0.0)), 0.0)
        g_last = gc[CHUNK - 1:CHUNK, :]
        eg = jnp.exp(gc)
        q = q_ref[:, sl]
        k = k_ref[:, sl]
        v = v_ref[:, sl]
        kb = k * bt
        k_bf = k.astype(BF16)
        a = jnp.where(strict, _dot_nt(kb.astype(BF16), k_bf) * decay, 0.0)
        x = jnp.concatenate([v * bt, kb * eg], axis=1)
        p = a.astype(BF16)
        x = x - _dot(p, x.astype(BF16))
        pw = 2
        while pw < CHUNK:
            a = _dot(p, p)
            p = a.astype(BF16)
            x = x + _dot(p, x.astype(BF16))
            pw *= 2
        u = x[:, :A_HEAD_DIM]
        w = x[:, A_HEAD_DIM:]
        qk = _dot_nt(q.astype(BF16), k_bf) * decay
        st = state_ref[h]
        st_bf = st.astype(BF16)
        v_new = u - _dot(w.astype(BF16), st_bf)
        v_new_bf = v_new.astype(BF16)
        out = _dot((q * eg).astype(BF16), st_bf) + _dot(qk.astype(BF16), v_new_bf)
        k_dec = (k * jnp.exp(g_last - gc)).astype(BF16)
        state_ref[h] = st * jnp.exp(g_last) + _dot_tn(k_dec, v_new_bf)
        o = _rms(out, gain_ref[...]) * gate_ref[:, sl].astype(F32)
        o_ref[:, sl] = o.astype(BF16)


def _gdn(q, k, v, gate, beta, gcum, gain_row):
    s = q.shape[0]
    row_spec = lambda w: pl.BlockSpec((CHUNK, w), lambda i: (i, 0))
    return pl.pallas_call(
        _gdn_kernel,
        grid=(s // CHUNK,),
        in_specs=[row_spec(A_WIDTH)] * 4 + [row_spec(LANES)] * 2 + [_const_spec(gain_row.shape)],
        out_specs=row_spec(A_WIDTH),
        out_shape=jax.ShapeDtypeStruct((s, A_WIDTH), BF16),
        scratch_shapes=[pltpu.VMEM((A_HEADS, A_HEAD_DIM, A_HEAD_DIM), F32)],
        compiler_params=_params("arbitrary"),
        name="gdn",
    )(q, k, v, gate, beta, gcum, gain_row)


def _kv_kernel(x_ref, g_ref, wd_ref, lg_ref, wu_ref, cos_ref, sin_ref, kt_ref, v_ref):
    xn = _rms(x_ref[...], g_ref[...]).astype(BF16)
    ckr = _dot(xn, wd_ref[...])
    cn = _rms(ckr[:, :KV_LORA], lg_ref[...]).astype(BF16)
    kr = ckr[:, KV_LORA:KV_LORA + LANES] * cos_ref[...] + ckr[:, KV_LORA + LANES:] * sin_ref[...]
    kv = _dot(cn, wu_ref[...])
    ones_col = jnp.where(lax.broadcasted_iota(jnp.int32, (KV_ROWS, V_PAD - V_HEAD), 1) == 0, 1.0, 0.0)
    for h in range(B_HEADS):
        k_full = jnp.concatenate([kv[:, QK_NOPE * h:QK_NOPE * (h + 1)], kr], axis=1)
        kt_ref[h] = k_full.T.astype(BF16)
        v_h = kv[:, B_HEADS * QK_NOPE + V_HEAD * h:B_HEADS * QK_NOPE + V_HEAD * (h + 1)]
        v_ref[h] = jnp.concatenate([v_h, ones_col], axis=1).astype(BF16)


def _kv(x, norm_row, wd, lat_row, wu, cos_t, sin_t):
    s = x.shape[0]
    tm = KV_ROWS
    consts1 = [norm_row, wd, lat_row, wu]
    return pl.pallas_call(
        _kv_kernel,
        grid=(s // tm,),
        in_specs=[pl.BlockSpec((tm, D_MODEL), lambda i: (i, 0))]
        + [_const_spec(a.shape) for a in consts1]
        + [pl.BlockSpec((tm, LANES), lambda i: (i, 0))] * 2,
        out_specs=[pl.BlockSpec((B_HEADS, QK_PAD, tm), lambda i: (0, 0, i)),
                   pl.BlockSpec((B_HEADS, tm, V_PAD), lambda i: (0, i, 0))],
        out_shape=[jax.ShapeDtypeStruct((B_HEADS, QK_PAD, s), BF16),
                   jax.ShapeDtypeStruct((B_HEADS, s, V_PAD), BF16)],
        compiler_params=_params("parallel"),
        name="mla_kv",
    )(x, *consts1, cos_t, sin_t)


def _b_in_kernel(x_ref, g_ref, win_ref, qg_ref, wuq_ref, wrot_ref, cos_ref, sin_ref, kt_ref, vp_ref,
                 q_ref, om_ref):
    scale = (QK_NOPE + QK_ROPE) ** -0.5 * LOG2E
    xn = _rms(x_ref[...], g_ref[...]).astype(BF16)
    h_in = _dot(xn, win_ref[...])
    cqn = _rms(h_in[:, :Q_LORA], qg_ref[...]).astype(BF16)
    qa = _dot(cqn, wuq_ref[...])
    qb = _dot(cqn, wrot_ref[...])
    cos_t = cos_ref[...]
    sin_t = sin_ref[...]
    for h in range(B_HEADS):
        q_ref[h, :, :QK_NOPE] = (qa[:, QK_PAD * h:QK_PAD * h + QK_NOPE] * scale).astype(BF16)
        hi = qa[:, QK_PAD * h + QK_NOPE:QK_PAD * (h + 1)] * cos_t + qb[:, LANES * h:LANES * (h + 1)] * sin_t
        q_ref[h, :, QK_NOPE:] = (hi * scale).astype(BF16)
    qm = h_in[:, Q_LORA:].astype(BF16)
    om_ref[...] = _mem_attention(qm, kt_ref, vp_ref).astype(BF16)


def _b_in(x, norm_row, win, qg_row, wuq, wrot, cos_t, sin_t, kt, vp):
    s = x.shape[0]
    tm = B_IN_ROWS
    consts1 = [norm_row, win, qg_row, wuq, wrot]
    consts2 = [kt, vp]
    return pl.pallas_call(
        _b_in_kernel,
        grid=(s // tm,),
        in_specs=[pl.BlockSpec((tm, D_MODEL), lambda i: (i, 0))]
        + [_const_spec(a.shape) for a in consts1]
        + [pl.BlockSpec((tm, LANES), lambda i: (i, 0))] * 2
        + [_const_spec(a.shape) for a in consts2],
        out_specs=[pl.BlockSpec((B_HEADS, tm, QK_PAD), lambda i: (0, i, 0)),
                   pl.BlockSpec((tm, MEM_WIDTH), lambda i: (i, 0))],
        out_shape=[jax.ShapeDtypeStruct((B_HEADS, s, QK_PAD), BF16),
                   jax.ShapeDtypeStruct((s, MEM_WIDTH), BF16)],
        compiler_params=_params("parallel"),
        name="b_in",
    )(x, *consts1, cos_t, sin_t, *consts2)


def _attn_kernel(q_ref, kt_ref, v_ref, o_ref, m_ref, acc_ref):
    i = pl.program_id(1)
    q = q_ref[0]
    m_ref[...] = jnp.full(m_ref.shape, -jnp.inf, F32)
    acc_ref[...] = jnp.zeros(acc_ref.shape, F32)
    lane_tiles = ATT_K // LANES

    def pair(pi, masked):
        m = m_ref[...]
        acc = acc_ref[...]
        for half in range(2):
            start = pl.multiple_of((2 * pi + half) * ATT_K, ATT_K)
            s = _dot(q, kt_ref[0, :, pl.ds(start, ATT_K)])
            if masked:
                r = lax.broadcasted_iota(jnp.int32, (ATT_Q, ATT_K), 0)
                c = lax.broadcasted_iota(jnp.int32, (ATT_Q, ATT_K), 1) + half * ATT_K
                s = jnp.where((c // CHUNK) <= (r // CHUNK), s, -jnp.inf)
            mx = s[:, :LANES]
            for t in range(1, lane_tiles):
                mx = jnp.maximum(mx, s[:, LANES * t:LANES * (t + 1)])
            mx = jnp.broadcast_to(jnp.max(mx, axis=-1, keepdims=True), (ATT_Q, LANES))
            m_new = jnp.maximum(m, mx)
            alpha = jnp.exp2(m - m_new)
            p = jnp.exp2(s - jnp.concatenate([m_new] * lane_tiles, axis=1)).astype(BF16)
            pv = _dot(p, v_ref[0, pl.ds(start, ATT_K), :])
            acc = acc * jnp.concatenate([alpha] * (V_PAD // LANES), axis=1) + pv
            m = m_new
        m_ref[...] = m
        acc_ref[...] = acc

    lax.fori_loop(0, i, lambda pi, _: pair(pi, False), None)
    pair(i, True)
    acc = acc_ref[...]
    o_ref[...] = (acc[:, :V_HEAD] / acc[:, V_HEAD:V_HEAD + 1]).astype(BF16)


def _attn(q, kt, v):
    h, s, _ = q.shape
    assert ATT_Q == 2 * ATT_K and ATT_K % CHUNK == 0
    return pl.pallas_call(
        _attn_kernel,
        grid=(h, s // ATT_Q),
        in_specs=[pl.BlockSpec((1, ATT_Q, QK_PAD), lambda hh, i: (hh, i, 0)),
                  pl.BlockSpec((1, QK_PAD, s), lambda hh, i: (hh, 0, 0), pipeline_mode=pl.Buffered(1)),
                  pl.BlockSpec((1, s, V_PAD), lambda hh, i: (hh, 0, 0), pipeline_mode=pl.Buffered(1))],
        out_specs=pl.BlockSpec((ATT_Q, V_HEAD), lambda hh, i: (i, hh)),
        out_shape=jax.ShapeDtypeStruct((s, h * V_HEAD), BF16),
        scratch_shapes=[pltpu.VMEM((ATT_Q, LANES), F32), pltpu.VMEM((ATT_Q, V_PAD), F32)],
        compiler_params=_params("parallel", "arbitrary"),
        name="mla_attn",
    )(q, kt, v)


def _rot_cols(w):
    half = w.shape[-1] // 2
    return jnp.concatenate([-w[..., half:], w[..., :half]], axis=-1)


def _pad_cols(w, width):
    return jnp.pad(w, [(0, 0)] * (w.ndim - 1) + [(0, width - w.shape[-1])])


def _mem_layout(mem_kv_l):
    k = mem_kv_l[:, :MEM_WIDTH]
    v = mem_kv_l[:, MEM_WIDTH:]
    head_of = jnp.arange(MEM_WIDTH) // MEM_HEAD_DIM
    sel = (head_of[None, :] == jnp.arange(MEM_HEADS)[:, None]).astype(F32)
    kt = (k.T[None, :, :] * sel[:, :, None]).astype(BF16)
    vp = (v[None, :, :] * sel[:, None, :]).astype(BF16)
    return kt, vp


def kernel(x, mem, positions, ffn1_norm, ffn1_w_gu, ffn1_w_down, mix_norm, ffn2_norm, ffn2_w_gu,
           ffn2_w_down, w_out, mem_norm, w_mem_kv, a_w_in, a_conv, a_A_log, a_dt_bias, a_out_norm,
           b_w_in, b_q_norm, b_w_uq, kv_in_norm, w_dkv, kv_lat_norm, w_ukv, final_norm):
    b, s, d = x.shape
    assert (b, s, d) == (1, SEQ, D_MODEL)
    xs = x.reshape(s, d)
    row = lambda v: v.reshape(1, -1).astype(F32)

    inv = ROPE_THETA ** (-jnp.arange(0, QK_ROPE, 2, dtype=F32) / QK_ROPE)
    inv_row = _pad_cols(jnp.concatenate([inv, inv])[None, :], LANES)
    cos_t, sin_t = _rope_tables(positions.reshape(s, 1), inv_row)

    mem_kv_all = _mem_kv(mem.reshape(N_MEM, d), row(mem_norm), w_mem_kv.astype(BF16))

    def ffn_weights(w_gu, w_down):
        return w_gu[:, :D_FF].astype(BF16), w_gu[:, D_FF:].astype(BF16), w_down.astype(BF16)

    def out_weights(l):
        return w_out[l, :A_WIDTH].astype(BF16), w_out[l, A_WIDTH:].astype(BF16)

    for i in range(N_A_LAYERS):
        l = i
        xs = _ffn(xs, row(ffn1_norm[l]), *ffn_weights(ffn1_w_gu[l], ffn1_w_down[l]))
        w_in = a_w_in[i]
        wqkv = w_in[:, :3 * A_WIDTH].astype(BF16)
        wgate = w_in[:, 3 * A_WIDTH:4 * A_WIDTH].astype(BF16)
        wb = w_in[:, 4 * A_WIDTH:4 * A_WIDTH + A_HEADS]
        wa = w_in[:, 4 * A_WIDTH + A_HEADS:4 * A_WIDTH + 2 * A_HEADS]
        wba = jnp.concatenate([_pad_cols(wb, LANES), _pad_cols(wa, LANES)], axis=1).astype(BF16)
        wqm = w_in[:, 4 * A_WIDTH + 2 * A_HEADS:].astype(BF16)
        kt, vp = _mem_layout(mem_kv_all[l])
        q, k, v, gate, beta, gcum, o_m = _a_in(
            xs, row(mix_norm[l]), wqkv, wgate, wba, wqm, a_conv[i].astype(F32),
            _pad_cols(row(a_A_log[i]), LANES), _pad_cols(row(a_dt_bias[i]), LANES), kt, vp)
        o_a = _gdn(q, k, v, gate, beta, gcum, row(a_out_norm[i]))
        xs = _ffn(xs, row(ffn2_norm[l]), *ffn_weights(ffn2_w_gu[l], ffn2_w_down[l]),
                  proj=(o_a, o_m, *out_weights(l)))

    w_c = w_dkv[:, :KV_LORA]
    w_r = w_dkv[:, KV_LORA:]
    wd = jnp.concatenate([w_c, _pad_cols(w_r, LANES), _pad_cols(_rot_cols(w_r), LANES)], axis=1).astype(BF16)
    w_ukv3 = w_ukv.reshape(KV_LORA, B_HEADS, QK_NOPE + V_HEAD)
    wu = jnp.concatenate([w_ukv3[:, :, :QK_NOPE].reshape(KV_LORA, B_HEADS * QK_NOPE),
                          w_ukv3[:, :, QK_NOPE:].reshape(KV_LORA, B_HEADS * V_HEAD)], axis=1).astype(BF16)
    k_all, v_all = _kv(xs, row(kv_in_norm), wd, row(kv_lat_norm), wu, cos_t, sin_t)

    for j in range(N_B_LAYERS):
        l = N_A_LAYERS + j
        xs = _ffn(xs, row(ffn1_norm[l]), *ffn_weights(ffn1_w_gu[l], ffn1_w_down[l]))
        w_uq3 = b_w_uq[j].reshape(Q_LORA, B_HEADS, QK_NOPE + QK_ROPE)
        wuq = _pad_cols(w_uq3, QK_PAD).reshape(Q_LORA, B_HEADS * QK_PAD).astype(BF16)
        wrot = _pad_cols(_rot_cols(w_uq3[:, :, QK_NOPE:]), LANES).reshape(Q_LORA, B_HEADS * LANES).astype(BF16)
        kt, vp = _mem_layout(mem_kv_all[l])
        q_all, o_m = _b_in(xs, row(mix_norm[l]), b_w_in[j].astype(BF16), row(b_q_norm[j]), wuq, wrot,
                           cos_t, sin_t, kt, vp)
        o_b = _attn(q_all, k_all, v_all)
        last = j == N_B_LAYERS - 1
        xs = _ffn(xs, row(ffn2_norm[l]), *ffn_weights(ffn2_w_gu[l], ffn2_w_down[l]),
                  proj=(o_b, o_m, *out_weights(l)), final_row=row(final_norm) if last else None)

    return xs.reshape(b, s, d)
```

```python
import functools

import jax
import jax.numpy as jnp
from jax import lax
from jax.experimental import pallas as pl
from jax.experimental.pallas import tpu as pltpu

F32 = jnp.float32
BF16 = jnp.bfloat16

D_MODEL = 1024
SEQ = 16384
DEPTH = 4
CHUNK = 64
EPS = 1e-6
N_A_LAYERS = DEPTH // 2
N_B_LAYERS = DEPTH - N_A_LAYERS

A_HEADS = 6
A_HEAD_DIM = 128
A_WIDTH = A_HEADS * A_HEAD_DIM
CONV_K = 4

B_HEADS = 6
QK_NOPE = 128
QK_ROPE = 64
V_HEAD = 128
Q_LORA = 256
KV_LORA = 256
B_WIDTH = B_HEADS * V_HEAD
ROPE_THETA = 10000.0

N_MEM = 256
MEM_HEADS = 4
MEM_HEAD_DIM = 64
MEM_WIDTH = MEM_HEADS * MEM_HEAD_DIM

D_FF = 2816

LANES = 128
SUBLANES = 8
QK_PAD = 256
VMEM_LIMIT = 56 * 1024 * 1024

FFN_ROWS = 512
FFN_COLS = 512
A_IN_ROWS = 256
GDN_CHUNKS = 4
GDN_ROWS = GDN_CHUNKS * CHUNK
B_IN_ROWS = 512
KV_ROWS = 512
ATT_Q = 1024
ATT_K = 512
V_PAD = 256
LOG2E = 1.4426950408889634
ROPE_ROWS = 2048


def _params(*sem):
    return pltpu.CompilerParams(dimension_semantics=sem, vmem_limit_bytes=VMEM_LIMIT)


def _const_spec(shape):
    n = len(shape)
    return pl.BlockSpec(shape, lambda *_: (0,) * n, pipeline_mode=pl.Buffered(1))


def _rms(x, g):
    return x * lax.rsqrt(jnp.mean(x * x, axis=-1, keepdims=True) + EPS) * g


def _sigmoid(x):
    return 1.0 / (1.0 + jnp.exp(-x))


def _dot(a, b):
    return jnp.dot(a, b, preferred_element_type=F32)


def _dot_nt(a, b):
    return lax.dot_general(a, b, (((1,), (1,)), ((), ())), preferred_element_type=F32)


def _dot_tn(a, b):
    return lax.dot_general(a, b, (((0,), (0,)), ((), ())), preferred_element_type=F32)


def _softmax_rows(s):
    m = jnp.max(s, axis=-1, keepdims=True)
    p = jnp.exp(s - m)
    return p / jnp.sum(p, axis=-1, keepdims=True)


def _mem_attention(qm, kt_ref, vp_ref):
    out = None
    for h in range(MEM_HEADS):
        s = _dot(qm, kt_ref[h]) * (MEM_HEAD_DIM ** -0.5)
        p = _softmax_rows(s).astype(BF16)
        o = _dot(p, vp_ref[h])
        out = o if out is None else out + o
    return out


def _rope_kernel(pos_ref, inv_ref, cos_ref, sin_ref):
    ang = pos_ref[...].astype(F32) * inv_ref[...]
    cos_ref[...] = jnp.cos(ang)
    sin_ref[...] = jnp.sin(ang)


def _rope_tables(pos_col, inv_row):
    s = pos_col.shape[0]
    return pl.pallas_call(
        _rope_kernel,
        grid=(s // ROPE_ROWS,),
        in_specs=[pl.BlockSpec((ROPE_ROWS, 1), lambda i: (i, 0)),
                  pl.BlockSpec((1, LANES), lambda i: (0, 0))],
        out_specs=[pl.BlockSpec((ROPE_ROWS, LANES), lambda i: (i, 0))] * 2,
        out_shape=[jax.ShapeDtypeStruct((s, LANES), F32)] * 2,
        compiler_params=_params("parallel"),
        name="rope_tables",
    )(pos_col, inv_row)


def _mem_kv_kernel(mem_ref, g_ref, w_ref, o_ref):
    mn = _rms(mem_ref[...], g_ref[...]).astype(BF16)
    o_ref[0] = _dot(mn, w_ref[0])


def _mem_kv(mem2d, mem_norm_row, w_mem_kv_bf):
    return pl.pallas_call(
        _mem_kv_kernel,
        grid=(DEPTH,),
        in_specs=[pl.BlockSpec((N_MEM, D_MODEL), lambda l: (0, 0)),
                  pl.BlockSpec((1, D_MODEL), lambda l: (0, 0)),
                  pl.BlockSpec((1, D_MODEL, 2 * MEM_WIDTH), lambda l: (l, 0, 0))],
        out_specs=pl.BlockSpec((1, N_MEM, 2 * MEM_WIDTH), lambda l: (l, 0, 0)),
        out_shape=jax.ShapeDtypeStruct((DEPTH, N_MEM, 2 * MEM_WIDTH), F32),
        compiler_params=_params("parallel"),
        name="mem_kv",
    )(mem2d, mem_norm_row, w_mem_kv_bf)


def _ffn_kernel(*refs, has_proj, has_final):
    refs = list(refs)
    x_ref = refs.pop(0)
    if has_proj:
        oa_ref, om_ref, woa_ref, wom_ref = refs[:4]
        refs = refs[4:]
    g_ref, wg_ref, wu_ref, wd_ref = refs[:4]
    refs = refs[4:]
    if has_final:
        fg_ref = refs.pop(0)
    o_ref, act_ref = refs

    x = x_ref[...]
    if has_proj:
        x = x + _dot(oa_ref[...], woa_ref[...]) + _dot(om_ref[...], wom_ref[...])
    xn = _rms(x, g_ref[...]).astype(BF16)
    for c in range(0, D_FF, FFN_COLS):
        w = min(FFN_COLS, D_FF - c)
        gate = _dot(xn, wg_ref[:, c:c + w])
        up = _dot(xn, wu_ref[:, c:c + w])
        act_ref[:, c:c + w] = (gate * _sigmoid(gate) * up).astype(BF16)
    y = x + 0.5 * _dot(act_ref[...], wd_ref[...])
    if has_final:
        y = _rms(y, fg_ref[...])
    o_ref[...] = y


def _ffn(x, norm_row, wg, wu, wd, proj=None, final_row=None):
    s = x.shape[0]
    row_spec = lambda w: pl.BlockSpec((FFN_ROWS, w), lambda i: (i, 0))
    args, specs = [x], [row_spec(D_MODEL)]
    if proj is not None:
        oa, om, woa, wom = proj
        args += [oa, om, woa, wom]
        specs += [row_spec(oa.shape[1]), row_spec(om.shape[1]),
                  _const_spec(woa.shape), _const_spec(wom.shape)]
    args += [norm_row, wg, wu, wd]
    specs += [_const_spec(norm_row.shape), _const_spec(wg.shape), _const_spec(wu.shape),
              _const_spec(wd.shape)]
    if final_row is not None:
        args.append(final_row)
        specs.append(_const_spec(final_row.shape))
    return pl.pallas_call(
        functools.partial(_ffn_kernel, has_proj=proj is not None, has_final=final_row is not None),
        grid=(s // FFN_ROWS,),
        in_specs=specs,
        out_specs=row_spec(D_MODEL),
        out_shape=jax.ShapeDtypeStruct((s, D_MODEL), F32),
        scratch_shapes=[pltpu.VMEM((FFN_ROWS, D_FF), BF16)],
        compiler_params=_params("parallel"),
        name="ffn",
    )(*args)


def _a_in_kernel(x_ref, g_ref, wqkv_ref, wgate_ref, wba_ref, wqm_ref, conv_ref, alog_ref, dtb_ref,
                 kt_ref, vp_ref,
                 q_ref, k_ref, v_ref, gate_ref, beta_ref, gcum_ref, om_ref, ext_ref):
    tm = A_IN_ROWS
    tail = SUBLANES

    @pl.when(pl.program_id(0) == 0)
    def _():
        ext_ref[0:tail, :] = jnp.zeros((tail, 3 * A_WIDTH), F32)

    xn = _rms(x_ref[...], g_ref[...]).astype(BF16)
    ext_ref[tail:tail + tm, :] = _dot(xn, wqkv_ref[...])
    for b in range(3 * A_HEADS):
        sl = slice(LANES * b, LANES * (b + 1))
        acc = ext_ref[tail:tail + tm, sl] * conv_ref[CONV_K - 1:CONV_K, sl]
        for j in range(1, CONV_K):
            acc = acc + ext_ref[tail - j:tail - j + tm, sl] * conv_ref[CONV_K - 1 - j:CONV_K - j, sl]
        y = acc * _sigmoid(acc)
        if b < 2 * A_HEADS:
            y = y * lax.rsqrt(jnp.sum(y * y, axis=-1, keepdims=True) + EPS)
        if b < A_HEADS:
            q_ref[:, sl] = y * (A_HEAD_DIM ** -0.5)
        elif b < 2 * A_HEADS:
            k_ref[:, LANES * (b - A_HEADS):LANES * (b - A_HEADS + 1)] = y
        else:
            v_ref[:, LANES * (b - 2 * A_HEADS):LANES * (b - 2 * A_HEADS + 1)] = y
    ext_ref[0:tail, :] = ext_ref[tm:tm + tail, :]

    gate = _dot(xn, wgate_ref[...])
    gate_ref[...] = (gate * _sigmoid(gate)).astype(BF16)

    ba = _dot(xn, wba_ref[...])
    beta_ref[...] = _sigmoid(ba[:, :LANES])
    z = ba[:, LANES:] + dtb_ref[...]
    softplus = jnp.maximum(z, 0.0) + jnp.log(1.0 + jnp.exp(-jnp.abs(z)))
    g = -jnp.exp(alog_ref[...]) * softplus
    r = lax.broadcasted_iota(jnp.int32, (tm, tm), 0)
    c = lax.broadcasted_iota(jnp.int32, (tm, tm), 1)
    tri = jnp.where((c <= r) & ((c // CHUNK) == (r // CHUNK)), 1.0, 0.0).astype(BF16)
    g1 = g.astype(BF16)
    g2 = (g - g1.astype(F32)).astype(BF16)
    g3 = (g - g1.astype(F32) - g2.astype(F32)).astype(BF16)
    gcum_ref[...] = _dot(tri, g1) + _dot(tri, g2) + _dot(tri, g3)

    qm = _dot(xn, wqm_ref[...]).astype(BF16)
    om_ref[...] = _mem_attention(qm, kt_ref, vp_ref).astype(BF16)


def _a_in(x, norm_row, wqkv, wgate, wba, wqm, conv_w, alog_row, dtb_row, kt, vp):
    s = x.shape[0]
    tm = A_IN_ROWS
    row_spec = lambda w: pl.BlockSpec((tm, w), lambda i: (i, 0))
    consts = [norm_row, wqkv, wgate, wba, wqm, conv_w, alog_row, dtb_row, kt, vp]
    return pl.pallas_call(
        _a_in_kernel,
        grid=(s // tm,),
        in_specs=[row_spec(D_MODEL)] + [_const_spec(a.shape) for a in consts],
        out_specs=[row_spec(A_WIDTH), row_spec(A_WIDTH), row_spec(A_WIDTH), row_spec(A_WIDTH),
                   row_spec(LANES), row_spec(LANES), row_spec(MEM_WIDTH)],
        out_shape=[jax.ShapeDtypeStruct((s, A_WIDTH), F32), jax.ShapeDtypeStruct((s, A_WIDTH), F32),
                   jax.ShapeDtypeStruct((s, A_WIDTH), F32), jax.ShapeDtypeStruct((s, A_WIDTH), BF16),
                   jax.ShapeDtypeStruct((s, LANES), F32), jax.ShapeDtypeStruct((s, LANES), F32),
                   jax.ShapeDtypeStruct((s, MEM_WIDTH), BF16)],
        scratch_shapes=[pltpu.VMEM((tm + SUBLANES, 3 * A_WIDTH), F32)],
        compiler_params=_params("arbitrary"),
        name="a_in",
    )(x, *consts)


def _gdn_kernel(q_ref, k_ref, v_ref, gate_ref, beta_ref, gcum_ref, gain_ref, o_ref, state_ref):
    @pl.when(pl.program_id(0) == 0)
    def _():
        state_ref[...] = jnp.zeros(state_ref.shape, F32)

    rows = GDN_ROWS
    gc_all = gcum_ref[...]
    gc_rows = gc_all.T
    beta_all = beta_ref[...]
    r = lax.broadcasted_iota(jnp.int32, (rows, rows), 0)
    c = lax.broadcasted_iota(jnp.int32, (rows, rows), 1)
    same = (r // CHUNK) == (c // CHUNK)
    causal = same & (c <= r)
    strict = same & (c < r)
    row_chunk = lax.broadcasted_iota(jnp.int32, (rows, A_HEAD_DIM), 0) // CHUNK

    def chunk_columns(x):
        return jnp.concatenate([jnp.where(row_chunk == ci, x, 0.0) for ci in range(GDN_CHUNKS)], axis=1)

    heads = range(A_HEADS)
    sls = [slice(A_HEAD_DIM * h, A_HEAD_DIM * (h + 1)) for h in heads]
    gcs = [gc_all[:, h:h + 1] for h in heads]
    bts = [beta_all[:, h:h + 1] for h in heads]
    g_last = [[gc[CHUNK * (ci + 1) - 1:CHUNK * (ci + 1), :] for ci in range(GDN_CHUNKS)] for gc in gcs]
    ks = [k_ref[:, sl] for sl in sls]
    kbs = [k * bt for k, bt in zip(ks, bts)]
    d1s = [_dot_nt(jnp.concatenate([kb, q_ref[:, sl]], axis=0).astype(BF16), k.astype(BF16))
           for kb, k, sl in zip(kbs, ks, sls)]
    ps, qks, xs = [], [], []
    for h in heads:
        decay = jnp.where(causal, jnp.exp(jnp.where(causal, gcs[h] - gc_rows[h:h + 1, :], 0.0)), 0.0)
        ps.append(jnp.where(strict, d1s[h][:rows] * decay, 0.0).astype(BF16))
        qks.append((d1s[h][rows:] * decay).astype(BF16))
        xs.append(jnp.concatenate([v_ref[:, sls[h]] * bts[h], kbs[h] * jnp.exp(gcs[h])], axis=1))
    sign = -1.0
    pw = 1
    while 2 * pw < CHUNK:
        ds = [_dot(ps[h], jnp.concatenate([xs[h].astype(BF16), ps[h]], axis=1)) for h in heads]
        xs = [xs[h] + sign * ds[h][:, :2 * A_HEAD_DIM] for h in heads]
        ps = [ds[h][:, 2 * A_HEAD_DIM:].astype(BF16) for h in heads]
        sign = 1.0
        pw *= 2
    x_bfs = [(xs[h] + _dot(ps[h], xs[h].astype(BF16))).astype(BF16) for h in heads]
    d2s = [_dot(qks[h], x_bfs[h]) for h in heads]
    d3s = []
    for h in heads:
        g_last_rows = jnp.concatenate([jnp.broadcast_to(g, (CHUNK, 1)) for g in g_last[h]], axis=0)
        k_dec = ks[h] * jnp.exp(g_last_rows - gcs[h])
        d3s.append(_dot_tn(chunk_columns(k_dec).astype(BF16), x_bfs[h]))
    sts = [state_ref[h] for h in heads]
    starts = [[] for _ in heads]
    for ci in range(GDN_CHUNKS):
        for h in heads:
            st_bf = sts[h].astype(BF16)
            starts[h].append(st_bf)
            blk = d3s[h][A_HEAD_DIM * ci:A_HEAD_DIM * (ci + 1)]
            sts[h] = (sts[h] * jnp.exp(g_last[h][ci]) + blk[:, :A_HEAD_DIM]
                      - _dot(blk[:, A_HEAD_DIM:].astype(BF16), st_bf))
    for h in heads:
        state_ref[h] = sts[h]
        q_eff = q_ref[:, sls[h]] * jnp.exp(gcs[h]) - d2s[h][:, A_HEAD_DIM:]
        out = d2s[h][:, :A_HEAD_DIM] + _dot(chunk_columns(q_eff).astype(BF16), jnp.concatenate(starts[h], axis=0))
        o = _rms(out, gain_ref[...]) * gate_ref[:, sls[h]].astype(F32)
        o_ref[:, sls[h]] = o.astype(BF16)


def _gdn(q, k, v, gate, beta, gcum, gain_row):
    s = q.shape[0]
    row_spec = lambda w: pl.BlockSpec((GDN_ROWS, w), lambda i: (i, 0))
    return pl.pallas_call(
        _gdn_kernel,
        grid=(s // GDN_ROWS,),
        in_specs=[row_spec(A_WIDTH)] * 4 + [row_spec(LANES)] * 2 + [_const_spec(gain_row.shape)],
        out_specs=row_spec(A_WIDTH),
        out_shape=jax.ShapeDtypeStruct((s, A_WIDTH), BF16),
        scratch_shapes=[pltpu.VMEM((A_HEADS, A_HEAD_DIM, A_HEAD_DIM), F32)],
        compiler_params=_params("arbitrary"),
        name="gdn",
    )(q, k, v, gate, beta, gcum, gain_row)


def _kv_kernel(x_ref, g_ref, wd_ref, lg_ref, wu_ref, cos_ref, sin_ref, kt_ref, v_ref):
    xn = _rms(x_ref[...], g_ref[...]).astype(BF16)
    ckr = _dot(xn, wd_ref[...])
    cn = _rms(ckr[:, :KV_LORA], lg_ref[...]).astype(BF16)
    kr = ckr[:, KV_LORA:KV_LORA + LANES] * cos_ref[...] + ckr[:, KV_LORA + LANES:] * sin_ref[...]
    kv = _dot(cn, wu_ref[...])
    ones_col = jnp.where(lax.broadcasted_iota(jnp.int32, (KV_ROWS, V_PAD - V_HEAD), 1) == 0, 1.0, 0.0)
    for h in range(B_HEADS):
        k_full = jnp.concatenate([kv[:, QK_NOPE * h:QK_NOPE * (h + 1)], kr], axis=1)
        kt_ref[h] = k_full.T.astype(BF16)
        v_h = kv[:, B_HEADS * QK_NOPE + V_HEAD * h:B_HEADS * QK_NOPE + V_HEAD * (h + 1)]
        v_ref[h] = jnp.concatenate([v_h, ones_col], axis=1).astype(BF16)


def _kv(x, norm_row, wd, lat_row, wu, cos_t, sin_t):
    s = x.shape[0]
    tm = KV_ROWS
    consts1 = [norm_row, wd, lat_row, wu]
    return pl.pallas_call(
        _kv_kernel,
        grid=(s // tm,),
        in_specs=[pl.BlockSpec((tm, D_MODEL), lambda i: (i, 0))]
        + [_const_spec(a.shape) for a in consts1]
        + [pl.BlockSpec((tm, LANES), lambda i: (i, 0))] * 2,
        out_specs=[pl.BlockSpec((B_HEADS, QK_PAD, tm), lambda i: (0, 0, i)),
                   pl.BlockSpec((B_HEADS, tm, V_PAD), lambda i: (0, i, 0))],
        out_shape=[jax.ShapeDtypeStruct((B_HEADS, QK_PAD, s), BF16),
                   jax.ShapeDtypeStruct((B_HEADS, s, V_PAD), BF16)],
        compiler_params=_params("parallel"),
        name="mla_kv",
    )(x, *consts1, cos_t, sin_t)


def _b_in_kernel(x_ref, g_ref, win_ref, qg_ref, wuq_ref, wrot_ref, cos_ref, sin_ref, kt_ref, vp_ref,
                 q_ref, om_ref):
    scale = (QK_NOPE + QK_ROPE) ** -0.5 * LOG2E
    xn = _rms(x_ref[...], g_ref[...]).astype(BF16)
    h_in = _dot(xn, win_ref[...])
    cqn = _rms(h_in[:, :Q_LORA], qg_ref[...]).astype(BF16)
    qa = _dot(cqn, wuq_ref[...])
    qb = _dot(cqn, wrot_ref[...])
    cos_t = cos_ref[...]
    sin_t = sin_ref[...]
    for h in range(B_HEADS):
        q_ref[h, :, :QK_NOPE] = (qa[:, QK_PAD * h:QK_PAD * h + QK_NOPE] * scale).astype(BF16)
        hi = qa[:, QK_PAD * h + QK_NOPE:QK_PAD * (h + 1)] * cos_t + qb[:, LANES * h:LANES * (h + 1)] * sin_t
        q_ref[h, :, QK_NOPE:] = (hi * scale).astype(BF16)
    qm = h_in[:, Q_LORA:].astype(BF16)
    om_ref[...] = _mem_attention(qm, kt_ref, vp_ref).astype(BF16)


def _b_in(x, norm_row, win, qg_row, wuq, wrot, cos_t, sin_t, kt, vp):
    s = x.shape[0]
    tm = B_IN_ROWS
    consts1 = [norm_row, win, qg_row, wuq, wrot]
    consts2 = [kt, vp]
    return pl.pallas_call(
        _b_in_kernel,
        grid=(s // tm,),
        in_specs=[pl.BlockSpec((tm, D_MODEL), lambda i: (i, 0))]
        + [_const_spec(a.shape) for a in consts1]
        + [pl.BlockSpec((tm, LANES), lambda i: (i, 0))] * 2
        + [_const_spec(a.shape) for a in consts2],
        out_specs=[pl.BlockSpec((B_HEADS, tm, QK_PAD), lambda i: (0, i, 0)),
                   pl.BlockSpec((tm, MEM_WIDTH), lambda i: (i, 0))],
        out_shape=[jax.ShapeDtypeStruct((B_HEADS, s, QK_PAD), BF16),
                   jax.ShapeDtypeStruct((s, MEM_WIDTH), BF16)],
        compiler_params=_params("parallel"),
        name="b_in",
    )(x, *consts1, cos_t, sin_t, *consts2)


def _attn_kernel(q_ref, kt_ref, v_ref, o_ref, m_ref, acc_ref):
    i = pl.program_id(1)
    q = q_ref[0]
    m_ref[...] = jnp.full(m_ref.shape, -jnp.inf, F32)
    acc_ref[...] = jnp.zeros(acc_ref.shape, F32)
    lane_tiles = ATT_K // LANES

    def pair(pi, masked):
        m = m_ref[...]
        acc = acc_ref[...]
        starts = [pl.multiple_of((2 * pi + half) * ATT_K, ATT_K) for half in range(2)]
        ss = [_dot(q, kt_ref[0, :, pl.ds(st, ATT_K)]) for st in starts]
        ps, alphas = [], []
        for half in range(2):
            s = ss[half]
            if masked:
                r = lax.broadcasted_iota(jnp.int32, (ATT_Q, ATT_K), 0)
                c = lax.broadcasted_iota(jnp.int32, (ATT_Q, ATT_K), 1) + half * ATT_K
                s = jnp.where((c // CHUNK) <= (r // CHUNK), s, -jnp.inf)
            mx = s[:, :LANES]
            for t in range(1, lane_tiles):
                mx = jnp.maximum(mx, s[:, LANES * t:LANES * (t + 1)])
            mx = jnp.broadcast_to(jnp.max(mx, axis=-1, keepdims=True), (ATT_Q, LANES))
            m_new = jnp.maximum(m, mx)
            alphas.append(jnp.exp2(m - m_new))
            ps.append(jnp.exp2(s - jnp.concatenate([m_new] * lane_tiles, axis=1)).astype(BF16))
            m = m_new
        for half in range(2):
            pv = _dot(ps[half], v_ref[0, pl.ds(starts[half], ATT_K), :])
            acc = acc * jnp.concatenate([alphas[half]] * (V_PAD // LANES), axis=1) + pv
        m_ref[...] = m
        acc_ref[...] = acc

    lax.fori_loop(0, i, lambda pi, _: pair(pi, False), None)
    pair(i, True)
    acc = acc_ref[...]
    o_ref[...] = (acc[:, :V_HEAD] / acc[:, V_HEAD:V_HEAD + 1]).astype(BF16)


def _attn(q, kt, v):
    h, s, _ = q.shape
    assert ATT_Q == 2 * ATT_K and ATT_K % CHUNK == 0
    return pl.pallas_call(
        _attn_kernel,
        grid=(h, s // ATT_Q),
        in_specs=[pl.BlockSpec((1, ATT_Q, QK_PAD), lambda hh, i: (hh, i, 0)),
                  pl.BlockSpec((1, QK_PAD, s), lambda hh, i: (hh, 0, 0), pipeline_mode=pl.Buffered(1)),
                  pl.BlockSpec((1, s, V_PAD), lambda hh, i: (hh, 0, 0), pipeline_mode=pl.Buffered(1))],
        out_specs=pl.BlockSpec((ATT_Q, V_HEAD), lambda hh, i: (i, hh)),
        out_shape=jax.ShapeDtypeStruct((s, h * V_HEAD), BF16),
        scratch_shapes=[pltpu.VMEM((ATT_Q, LANES), F32), pltpu.VMEM((ATT_Q, V_PAD), F32)],
        compiler_params=_params("parallel", "arbitrary"),
        name="mla_attn",
    )(q, kt, v)


def _rot_cols(w):
    half = w.shape[-1] // 2
    return jnp.concatenate([-w[..., half:], w[..., :half]], axis=-1)


def _pad_cols(w, width):
    return jnp.pad(w, [(0, 0)] * (w.ndim - 1) + [(0, width - w.shape[-1])])


def _mem_layout(mem_kv_l):
    k = mem_kv_l[:, :MEM_WIDTH]
    v = mem_kv_l[:, MEM_WIDTH:]
    head_of = jnp.arange(MEM_WIDTH) // MEM_HEAD_DIM
    sel = (head_of[None, :] == jnp.arange(MEM_HEADS)[:, None]).astype(F32)
    kt = (k.T[None, :, :] * sel[:, :, None]).astype(BF16)
    vp = (v[None, :, :] * sel[:, None, :]).astype(BF16)
    return kt, vp


def kernel(x, mem, positions, ffn1_norm, ffn1_w_gu, ffn1_w_down, mix_norm, ffn2_norm, ffn2_w_gu,
           ffn2_w_down, w_out, mem_norm, w_mem_kv, a_w_in, a_conv, a_A_log, a_dt_bias, a_out_norm,
           b_w_in, b_q_norm, b_w_uq, kv_in_norm, w_dkv, kv_lat_norm, w_ukv, final_norm):
    b, s, d = x.shape
    assert (b, s, d) == (1, SEQ, D_MODEL)
    xs = x.reshape(s, d)
    row = lambda v: v.reshape(1, -1).astype(F32)

    inv = ROPE_THETA ** (-jnp.arange(0, QK_ROPE, 2, dtype=F32) / QK_ROPE)
    inv_row = _pad_cols(jnp.concatenate([inv, inv])[None, :], LANES)
    cos_t, sin_t = _rope_tables(positions.reshape(s, 1), inv_row)

    mem_kv_all = _mem_kv(mem.reshape(N_MEM, d), row(mem_norm), w_mem_kv.astype(BF16))

    def ffn_weights(w_gu, w_down):
        return w_gu[:, :D_FF].astype(BF16), w_gu[:, D_FF:].astype(BF16), w_down.astype(BF16)

    def out_weights(l):
        return w_out[l, :A_WIDTH].astype(BF16), w_out[l, A_WIDTH:].astype(BF16)

    for i in range(N_A_LAYERS):
        l = i
        xs = _ffn(xs, row(ffn1_norm[l]), *ffn_weights(ffn1_w_gu[l], ffn1_w_down[l]))
        w_in = a_w_in[i]
        wqkv = w_in[:, :3 * A_WIDTH].astype(BF16)
        wgate = w_in[:, 3 * A_WIDTH:4 * A_WIDTH].astype(BF16)
        wb = w_in[:, 4 * A_WIDTH:4 * A_WIDTH + A_HEADS]
        wa = w_in[:, 4 * A_WIDTH + A_HEADS:4 * A_WIDTH + 2 * A_HEADS]
        wba = jnp.concatenate([_pad_cols(wb, LANES), _pad_cols(wa, LANES)], axis=1).astype(BF16)
        wqm = w_in[:, 4 * A_WIDTH + 2 * A_HEADS:].astype(BF16)
        kt, vp = _mem_layout(mem_kv_all[l])
        q, k, v, gate, beta, gcum, o_m = _a_in(
            xs, row(mix_norm[l]), wqkv, wgate, wba, wqm, a_conv[i].astype(F32),
            _pad_cols(row(a_A_log[i]), LANES), _pad_cols(row(a_dt_bias[i]), LANES), kt, vp)
        o_a = _gdn(q, k, v, gate, beta, gcum, row(a_out_norm[i]))
        xs = _ffn(xs, row(ffn2_norm[l]), *ffn_weights(ffn2_w_gu[l], ffn2_w_down[l]),
                  proj=(o_a, o_m, *out_weights(l)))

    w_c = w_dkv[:, :KV_LORA]
    w_r = w_dkv[:, KV_LORA:]
    wd = jnp.concatenate([w_c, _pad_cols(w_r, LANES), _pad_cols(_rot_cols(w_r), LANES)], axis=1).astype(BF16)
    w_ukv3 = w_ukv.reshape(KV_LORA, B_HEADS, QK_NOPE + V_HEAD)
    wu = jnp.concatenate([w_ukv3[:, :, :QK_NOPE].reshape(KV_LORA, B_HEADS * QK_NOPE),
                          w_ukv3[:, :, QK_NOPE:].reshape(KV_LORA, B_HEADS * V_HEAD)], axis=1).astype(BF16)
    k_all, v_all = _kv(xs, row(kv_in_norm), wd, row(kv_lat_norm), wu, cos_t, sin_t)

    for j in range(N_B_LAYERS):
        l = N_A_LAYERS + j
        xs = _ffn(xs, row(ffn1_norm[l]), *ffn_weights(ffn1_w_gu[l], ffn1_w_down[l]))
        w_uq3 = b_w_uq[j].reshape(Q_LORA, B_HEADS, QK_NOPE + QK_ROPE)
        wuq = _pad_cols(w_uq3, QK_PAD).reshape(Q_LORA, B_HEADS * QK_PAD).astype(BF16)
        wrot = _pad_cols(_rot_cols(w_uq3[:, :, QK_NOPE:]), LANES).reshape(Q_LORA, B_HEADS * LANES).astype(BF16)
        kt, vp = _mem_layout(mem_kv_all[l])
        q_all, o_m = _b_in(xs, row(mix_norm[l]), b_w_in[j].astype(BF16), row(b_q_norm[j]), wuq, wrot,
                           cos_t, sin_t, kt, vp)
        o_b = _attn(q_all, k_all, v_all)
        last = j == N_B_LAYERS - 1
        xs = _ffn(xs, row(ffn2_norm[l]), *ffn_weights(ffn2_w_gu[l], ffn2_w_down[l]),
                  proj=(o_b, o_m, *out_weights(l)), final_row=row(final_norm) if last else None)

    return xs.reshape(b, s, d)
```

```python
import functools

import jax
import jax.numpy as jnp
from jax import lax
from jax.experimental import pallas as pl
from jax.experimental.pallas import tpu as pltpu

F32 = jnp.float32
BF16 = jnp.bfloat16

D_MODEL = 1024
SEQ = 16384
DEPTH = 4
CHUNK = 64
EPS = 1e-6
N_A_LAYERS = DEPTH // 2
N_B_LAYERS = DEPTH - N_A_LAYERS

A_HEADS = 6
A_HEAD_DIM = 128
A_WIDTH = A_HEADS * A_HEAD_DIM
CONV_K = 4

B_HEADS = 6
QK_NOPE = 128
QK_ROPE = 64
V_HEAD = 128
Q_LORA = 256
KV_LORA = 256
B_WIDTH = B_HEADS * V_HEAD
ROPE_THETA = 10000.0

N_MEM = 256
MEM_HEADS = 4
MEM_HEAD_DIM = 64
MEM_WIDTH = MEM_HEADS * MEM_HEAD_DIM

D_FF = 2816

LANES = 128
SUBLANES = 8
QK_PAD = 256
VMEM_LIMIT = 56 * 1024 * 1024

FFN_ROWS = 512
FFN_COLS = 512
A_IN_ROWS = 256
GDN_CHUNKS = 4
GDN_ROWS = GDN_CHUNKS * CHUNK
B_IN_ROWS = 512
KV_ROWS = 512
ATT_Q = 1024
ATT_K = 256
ATT_BLOCKS = ATT_Q // ATT_K
BF16_SUBLANES = 16
V_ROWS = V_HEAD + BF16_SUBLANES
LOG2E = 1.4426950408889634
ROPE_ROWS = 2048


def _params(*sem):
    return pltpu.CompilerParams(dimension_semantics=sem, vmem_limit_bytes=VMEM_LIMIT)


def _const_spec(shape):
    n = len(shape)
    return pl.BlockSpec(shape, lambda *_: (0,) * n, pipeline_mode=pl.Buffered(1))


def _layer_spec(stacked, l):
    n = stacked.ndim - 1
    return pl.BlockSpec((None,) + tuple(stacked.shape[1:]), lambda *_: (l,) + (0,) * n,
                        pipeline_mode=pl.Buffered(1))


def _rms(x, g):
    return x * lax.rsqrt(jnp.mean(x * x, axis=-1, keepdims=True) + EPS) * g


def _sigmoid(x):
    return 1.0 / (1.0 + jnp.exp(-x))


def _dot(a, b):
    return jnp.dot(a, b, preferred_element_type=F32)


def _dot_nt(a, b):
    return lax.dot_general(a, b, (((1,), (1,)), ((), ())), preferred_element_type=F32)


def _dot_tn(a, b):
    return lax.dot_general(a, b, (((0,), (0,)), ((), ())), preferred_element_type=F32)


def _softmax_rows(s):
    m = jnp.max(s, axis=-1, keepdims=True)
    p = jnp.exp(s - m)
    return p / jnp.sum(p, axis=-1, keepdims=True)


def _mem_attention(qm, kt_ref, vp_ref):
    out = None
    for h in range(MEM_HEADS):
        s = _dot(qm, kt_ref[h]) * (MEM_HEAD_DIM ** -0.5)
        p = _softmax_rows(s).astype(BF16)
        o = _dot(p, vp_ref[h])
        out = o if out is None else out + o
    return out


def _rope_kernel(pos_ref, inv_ref, cos_ref, sin_ref):
    ang = pos_ref[...].astype(F32) * inv_ref[...]
    cos_ref[...] = jnp.cos(ang)
    sin_ref[...] = jnp.sin(ang)


def _rope_tables(pos_col, inv_row):
    s = pos_col.shape[0]
    return pl.pallas_call(
        _rope_kernel,
        grid=(s // ROPE_ROWS,),
        in_specs=[pl.BlockSpec((ROPE_ROWS, 1), lambda i: (i, 0)),
                  pl.BlockSpec((1, LANES), lambda i: (0, 0))],
        out_specs=[pl.BlockSpec((ROPE_ROWS, LANES), lambda i: (i, 0))] * 2,
        out_shape=[jax.ShapeDtypeStruct((s, LANES), F32)] * 2,
        compiler_params=_params("parallel"),
        name="rope_tables",
    )(pos_col, inv_row)


def _mem_kv_kernel(mem_ref, g_ref, w_ref, o_ref):
    mn = _rms(mem_ref[...], g_ref[...]).astype(BF16)
    o_ref[0] = _dot(mn, w_ref[0])


def _mem_kv(mem2d, mem_norm_row, w_mem_kv_bf):
    return pl.pallas_call(
        _mem_kv_kernel,
        grid=(DEPTH,),
        in_specs=[pl.BlockSpec((N_MEM, D_MODEL), lambda l: (0, 0)),
                  pl.BlockSpec((1, D_MODEL), lambda l: (0, 0)),
                  pl.BlockSpec((1, D_MODEL, 2 * MEM_WIDTH), lambda l: (l, 0, 0))],
        out_specs=pl.BlockSpec((1, N_MEM, 2 * MEM_WIDTH), lambda l: (l, 0, 0)),
        out_shape=jax.ShapeDtypeStruct((DEPTH, N_MEM, 2 * MEM_WIDTH), F32),
        compiler_params=_params("parallel"),
        name="mem_kv",
    )(mem2d, mem_norm_row, w_mem_kv_bf)


def _ffn_kernel(*refs, has_proj, has_final):
    refs = list(refs)
    x_ref = refs.pop(0)
    if has_proj:
        oa_ref, om_ref, wout_ref = refs[:3]
        refs = refs[3:]
    g_ref, wgu_ref, wd_ref = refs[:3]
    refs = refs[3:]
    if has_final:
        fg_ref = refs.pop(0)
    o_ref, act_ref = refs

    x = x_ref[...]
    if has_proj:
        mix_w = oa_ref.shape[1]
        x = x + _dot(oa_ref[...], wout_ref[:mix_w]) + _dot(om_ref[...], wout_ref[mix_w:])
    xn = _rms(x, g_ref[...]).astype(BF16)
    for c in range(0, D_FF, FFN_COLS):
        w = min(FFN_COLS, D_FF - c)
        gate = _dot(xn, wgu_ref[:, c:c + w])
        up = _dot(xn, wgu_ref[:, D_FF + c:D_FF + c + w])
        act_ref[:, c:c + w] = (gate * _sigmoid(gate) * up).astype(BF16)
    y = x + 0.5 * _dot(act_ref[...], wd_ref[...])
    if has_final:
        y = _rms(y, fg_ref[...])
    o_ref[...] = y


def _ffn(x, l, norms, wgu, wd, proj=None, final_row=None):
    s = x.shape[0]
    row_spec = lambda w: pl.BlockSpec((FFN_ROWS, w), lambda i: (i, 0))
    args, specs = [x], [row_spec(D_MODEL)]
    if proj is not None:
        oa, om, wout = proj
        args += [oa, om, wout]
        specs += [row_spec(oa.shape[1]), row_spec(om.shape[1]), _layer_spec(wout, l)]
    args += [norms, wgu, wd]
    specs += [_layer_spec(norms, l), _layer_spec(wgu, l), _layer_spec(wd, l)]
    if final_row is not None:
        args.append(final_row)
        specs.append(_const_spec(final_row.shape))
    return pl.pallas_call(
        functools.partial(_ffn_kernel, has_proj=proj is not None, has_final=final_row is not None),
        grid=(s // FFN_ROWS,),
        in_specs=specs,
        out_specs=row_spec(D_MODEL),
        out_shape=jax.ShapeDtypeStruct((s, D_MODEL), F32),
        scratch_shapes=[pltpu.VMEM((FFN_ROWS, D_FF), BF16)],
        compiler_params=_params("parallel"),
        name="ffn",
    )(*args)


def _a_in_kernel(x_ref, g_ref, w_ref, conv_ref, alog_ref, dtb_ref, kt_ref, vp_ref,
                 q_ref, k_ref, v_ref, gate_ref, beta_ref, gcum_ref, om_ref, ext_ref):
    tm = A_IN_ROWS
    tail = SUBLANES
    wqkv_ref = w_ref.at[:, :3 * A_WIDTH]
    wgate_ref = w_ref.at[:, 3 * A_WIDTH:4 * A_WIDTH]
    wba_ref = w_ref.at[:, 4 * A_WIDTH:4 * A_WIDTH + 2 * LANES]
    wqm_ref = w_ref.at[:, 4 * A_WIDTH + 2 * LANES:]

    @pl.when(pl.program_id(0) == 0)
    def _():
        ext_ref[0:tail, :] = jnp.zeros((tail, 3 * A_WIDTH), F32)

    xn = _rms(x_ref[...], g_ref[...]).astype(BF16)
    ext_ref[tail:tail + tm, :] = _dot(xn, wqkv_ref[...])
    for b in range(3 * A_HEADS):
        sl = slice(LANES * b, LANES * (b + 1))
        acc = ext_ref[tail:tail + tm, sl] * conv_ref[CONV_K - 1:CONV_K, sl]
        for j in range(1, CONV_K):
            acc = acc + ext_ref[tail - j:tail - j + tm, sl] * conv_ref[CONV_K - 1 - j:CONV_K - j, sl]
        y = acc * _sigmoid(acc)
        if b < 2 * A_HEADS:
            y = y * lax.rsqrt(jnp.sum(y * y, axis=-1, keepdims=True) + EPS)
        if b < A_HEADS:
            q_ref[:, sl] = y * (A_HEAD_DIM ** -0.5)
        elif b < 2 * A_HEADS:
            k_ref[:, LANES * (b - A_HEADS):LANES * (b - A_HEADS + 1)] = y
        else:
            v_ref[:, LANES * (b - 2 * A_HEADS):LANES * (b - 2 * A_HEADS + 1)] = y
    ext_ref[0:tail, :] = ext_ref[tm:tm + tail, :]

    gate = _dot(xn, wgate_ref[...])
    gate_ref[...] = (gate * _sigmoid(gate)).astype(BF16)

    ba = _dot(xn, wba_ref[...])
    beta_ref[...] = _sigmoid(ba[:, :LANES])
    z = ba[:, LANES:] + dtb_ref[...]
    softplus = jnp.maximum(z, 0.0) + jnp.log(1.0 + jnp.exp(-jnp.abs(z)))
    g = -jnp.exp(alog_ref[...]) * softplus
    r = lax.broadcasted_iota(jnp.int32, (tm, tm), 0)
    c = lax.broadcasted_iota(jnp.int32, (tm, tm), 1)
    tri = jnp.where((c <= r) & ((c // CHUNK) == (r // CHUNK)), 1.0, 0.0).astype(BF16)
    g1 = g.astype(BF16)
    g2 = (g - g1.astype(F32)).astype(BF16)
    g3 = (g - g1.astype(F32) - g2.astype(F32)).astype(BF16)
    gcum_ref[...] = _dot(tri, g1) + _dot(tri, g2) + _dot(tri, g3)

    qm = _dot(xn, wqm_ref[...]).astype(BF16)
    om_ref[...] = _mem_attention(qm, kt_ref, vp_ref).astype(BF16)


def _a_in(x, l, i_a, norms, w_a, conv_w, alog_rows, dtb_rows, kt, vp):
    s = x.shape[0]
    tm = A_IN_ROWS
    row_spec = lambda w: pl.BlockSpec((tm, w), lambda i: (i, 0))
    consts = [norms, w_a, conv_w, alog_rows, dtb_rows, kt, vp]
    layer_of = [l, i_a, i_a, i_a, i_a, l, l]
    return pl.pallas_call(
        _a_in_kernel,
        grid=(s // tm,),
        in_specs=[row_spec(D_MODEL)] + [_layer_spec(a, j) for a, j in zip(consts, layer_of)],
        out_specs=[row_spec(A_WIDTH), row_spec(A_WIDTH), row_spec(A_WIDTH), row_spec(A_WIDTH),
                   row_spec(LANES), row_spec(LANES), row_spec(MEM_WIDTH)],
        out_shape=[jax.ShapeDtypeStruct((s, A_WIDTH), F32), jax.ShapeDtypeStruct((s, A_WIDTH), F32),
                   jax.ShapeDtypeStruct((s, A_WIDTH), F32), jax.ShapeDtypeStruct((s, A_WIDTH), BF16),
                   jax.ShapeDtypeStruct((s, LANES), F32), jax.ShapeDtypeStruct((s, LANES), F32),
                   jax.ShapeDtypeStruct((s, MEM_WIDTH), BF16)],
        scratch_shapes=[pltpu.VMEM((tm + SUBLANES, 3 * A_WIDTH), F32)],
        compiler_params=_params("arbitrary"),
        name="a_in",
    )(x, *consts)


def _gdn_kernel(q_ref, k_ref, v_ref, gate_ref, beta_ref, gcum_ref, gain_ref, o_ref, state_ref):
    @pl.when(pl.program_id(0) == 0)
    def _():
        state_ref[...] = jnp.zeros(state_ref.shape, F32)

    rows = GDN_ROWS
    gc_all = gcum_ref[...]
    gc_rows = gc_all.T
    beta_all = beta_ref[...]
    r = lax.broadcasted_iota(jnp.int32, (rows, rows), 0)
    c = lax.broadcasted_iota(jnp.int32, (rows, rows), 1)
    same = (r // CHUNK) == (c // CHUNK)
    causal = same & (c <= r)
    strict = same & (c < r)
    row_chunk = lax.broadcasted_iota(jnp.int32, (rows, A_HEAD_DIM), 0) // CHUNK

    def chunk_columns(x):
        return jnp.concatenate([jnp.where(row_chunk == ci, x, 0.0) for ci in range(GDN_CHUNKS)], axis=1)

    heads = range(A_HEADS)
    sls = [slice(A_HEAD_DIM * h, A_HEAD_DIM * (h + 1)) for h in heads]
    gcs = [gc_all[:, h:h + 1] for h in heads]
    bts = [beta_all[:, h:h + 1] for h in heads]
    g_last = [[gc[CHUNK * (ci + 1) - 1:CHUNK * (ci + 1), :] for ci in range(GDN_CHUNKS)] for gc in gcs]
    ks = [k_ref[:, sl] for sl in sls]
    kbs = [k * bt for k, bt in zip(ks, bts)]
    d1s = [_dot_nt(jnp.concatenate([kb, q_ref[:, sl]], axis=0).astype(BF16), k.astype(BF16))
           for kb, k, sl in zip(kbs, ks, sls)]
    ps, qks, xs = [], [], []
    for h in heads:
        decay = jnp.where(causal, jnp.exp(jnp.where(causal, gcs[h] - gc_rows[h:h + 1, :], 0.0)), 0.0)
        ps.append(jnp.where(strict, d1s[h][:rows] * decay, 0.0).astype(BF16))
        qks.append((d1s[h][rows:] * decay).astype(BF16))
        xs.append(jnp.concatenate([v_ref[:, sls[h]] * bts[h], kbs[h] * jnp.exp(gcs[h])], axis=1))
    sign = -1.0
    pw = 1
    while 2 * pw < CHUNK:
        ds = [_dot(ps[h], jnp.concatenate([xs[h].astype(BF16), ps[h]], axis=1)) for h in heads]
        xs = [xs[h] + sign * ds[h][:, :2 * A_HEAD_DIM] for h in heads]
        ps = [ds[h][:, 2 * A_HEAD_DIM:].astype(BF16) for h in heads]
        sign = 1.0
        pw *= 2
    x_bfs = [(xs[h] + _dot(ps[h], xs[h].astype(BF16))).astype(BF16) for h in heads]
    d2s = [_dot(qks[h], x_bfs[h]) for h in heads]
    d3s = []
    for h in heads:
        g_last_rows = jnp.concatenate([jnp.broadcast_to(g, (CHUNK, 1)) for g in g_last[h]], axis=0)
        k_dec = ks[h] * jnp.exp(g_last_rows - gcs[h])
        d3s.append(_dot_tn(chunk_columns(k_dec).astype(BF16), x_bfs[h]))
    sts = [state_ref[h] for h in heads]
    starts = [[] for _ in heads]
    for ci in range(GDN_CHUNKS):
        for h in heads:
            st_bf = sts[h].astype(BF16)
            starts[h].append(st_bf)
            blk = d3s[h][A_HEAD_DIM * ci:A_HEAD_DIM * (ci + 1)]
            sts[h] = (sts[h] * jnp.exp(g_last[h][ci]) + blk[:, :A_HEAD_DIM]
                      - _dot(blk[:, A_HEAD_DIM:].astype(BF16), st_bf))
    for h in heads:
        state_ref[h] = sts[h]
        q_eff = q_ref[:, sls[h]] * jnp.exp(gcs[h]) - d2s[h][:, A_HEAD_DIM:]
        out = d2s[h][:, :A_HEAD_DIM] + _dot(chunk_columns(q_eff).astype(BF16), jnp.concatenate(starts[h], axis=0))
        o = _rms(out, gain_ref[...]) * gate_ref[:, sls[h]].astype(F32)
        o_ref[:, sls[h]] = o.astype(BF16)


def _gdn(q, k, v, gate, beta, gcum, i_a, gain_rows):
    s = q.shape[0]
    row_spec = lambda w: pl.BlockSpec((GDN_ROWS, w), lambda i: (i, 0))
    return pl.pallas_call(
        _gdn_kernel,
        grid=(s // GDN_ROWS,),
        in_specs=[row_spec(A_WIDTH)] * 4 + [row_spec(LANES)] * 2 + [_layer_spec(gain_rows, i_a)],
        out_specs=row_spec(A_WIDTH),
        out_shape=jax.ShapeDtypeStruct((s, A_WIDTH), BF16),
        scratch_shapes=[pltpu.VMEM((A_HEADS, A_HEAD_DIM, A_HEAD_DIM), F32)],
        compiler_params=_params("arbitrary"),
        name="gdn",
    )(q, k, v, gate, beta, gcum, gain_rows)


def _kv_kernel(x_ref, g_ref, wd_ref, lg_ref, wu_ref, cos_ref, sin_ref, k_ref, vt_ref):
    xn = _rms(x_ref[...], g_ref[...]).astype(BF16)
    ckr = _dot(xn, wd_ref[...])
    cn = _rms(ckr[:, :KV_LORA], lg_ref[...]).astype(BF16)
    kr = ckr[:, KV_LORA:KV_LORA + LANES] * cos_ref[...] + ckr[:, KV_LORA + LANES:] * sin_ref[...]
    kv = _dot(cn, wu_ref[...])
    ones_row = jnp.where(lax.broadcasted_iota(jnp.int32, (V_ROWS - V_HEAD, KV_ROWS), 0) == 0, 1.0, 0.0)
    kr = kr.astype(BF16)
    for h in range(B_HEADS):
        k_ref[h, :, :QK_NOPE] = kv[:, QK_NOPE * h:QK_NOPE * (h + 1)].astype(BF16)
        k_ref[h, :, QK_NOPE:] = kr
        v_h = kv[:, B_HEADS * QK_NOPE + V_HEAD * h:B_HEADS * QK_NOPE + V_HEAD * (h + 1)]
        vt_ref[h] = jnp.concatenate([v_h.T, ones_row], axis=0).astype(BF16)


def _kv(x, norm_row, wd, lat_row, wu, cos_t, sin_t):
    s = x.shape[0]
    tm = KV_ROWS
    consts1 = [norm_row, wd, lat_row, wu]
    return pl.pallas_call(
        _kv_kernel,
        grid=(s // tm,),
        in_specs=[pl.BlockSpec((tm, D_MODEL), lambda i: (i, 0))]
        + [_const_spec(a.shape) for a in consts1]
        + [pl.BlockSpec((tm, LANES), lambda i: (i, 0))] * 2,
        out_specs=[pl.BlockSpec((B_HEADS, tm, QK_PAD), lambda i: (0, i, 0)),
                   pl.BlockSpec((B_HEADS, V_ROWS, tm), lambda i: (0, 0, i))],
        out_shape=[jax.ShapeDtypeStruct((B_HEADS, s, QK_PAD), BF16),
                   jax.ShapeDtypeStruct((B_HEADS, V_ROWS, s), BF16)],
        compiler_params=_params("parallel"),
        name="mla_kv",
    )(x, *consts1, cos_t, sin_t)


def _b_in_kernel(x_ref, g_ref, win_ref, qg_ref, wuq_ref, wrot_ref, cos_ref, sin_ref, kt_ref, vp_ref,
                 qt_ref, om_ref):
    scale = (QK_NOPE + QK_ROPE) ** -0.5 * LOG2E
    xn = _rms(x_ref[...], g_ref[...]).astype(BF16)
    h_in = _dot(xn, win_ref[...])
    cqn = _rms(h_in[:, :Q_LORA], qg_ref[...]).astype(BF16)
    qa = _dot(cqn, wuq_ref[...])
    qb = _dot(cqn, wrot_ref[...])
    cos_t = cos_ref[...]
    sin_t = sin_ref[...]
    for h in range(B_HEADS):
        lo = qa[:, QK_PAD * h:QK_PAD * h + QK_NOPE]
        hi = qa[:, QK_PAD * h + QK_NOPE:QK_PAD * (h + 1)] * cos_t + qb[:, LANES * h:LANES * (h + 1)] * sin_t
        qt_ref[h] = (jnp.concatenate([lo, hi], axis=1) * scale).T.astype(BF16)
    qm = h_in[:, Q_LORA:].astype(BF16)
    om_ref[...] = _mem_attention(qm, kt_ref, vp_ref).astype(BF16)


def _b_in(x, l, j_b, norms, win, qg_rows, wuq, wrot, cos_t, sin_t, kt, vp):
    s = x.shape[0]
    tm = B_IN_ROWS
    consts1 = [norms, win, qg_rows, wuq, wrot]
    consts2 = [kt, vp]
    return pl.pallas_call(
        _b_in_kernel,
        grid=(s // tm,),
        in_specs=[pl.BlockSpec((tm, D_MODEL), lambda i: (i, 0))]
        + [_layer_spec(a, jj) for a, jj in zip(consts1, [l, j_b, j_b, j_b, j_b])]
        + [pl.BlockSpec((tm, LANES), lambda i: (i, 0))] * 2
        + [_layer_spec(a, l) for a in consts2],
        out_specs=[pl.BlockSpec((B_HEADS, QK_PAD, tm), lambda i: (0, 0, i)),
                   pl.BlockSpec((tm, MEM_WIDTH), lambda i: (i, 0))],
        out_shape=[jax.ShapeDtypeStruct((B_HEADS, QK_PAD, s), BF16),
                   jax.ShapeDtypeStruct((s, MEM_WIDTH), BF16)],
        compiler_params=_params("parallel"),
        name="b_in",
    )(x, *consts1, cos_t, sin_t, *consts2)


def _attn_kernel(qt_ref, k_ref, vt_ref, o_ref, m_ref, acc_ref):
    i = pl.program_id(1)
    qt = qt_ref[0]
    m_ref[...] = jnp.full(m_ref.shape, -jnp.inf, F32)
    acc_ref[...] = jnp.zeros(acc_ref.shape, F32)

    def pair(pi, masked):
        m = m_ref[...]
        acc = acc_ref[...]
        starts = [pl.multiple_of((ATT_BLOCKS * pi + b) * ATT_K, ATT_K) for b in range(ATT_BLOCKS)]
        offs = [b * ATT_K if masked else 0 for b in range(ATT_BLOCKS)]
        ss = [_dot(k_ref[0, pl.ds(st, ATT_K), :], qt[:, off:]) for st, off in zip(starts, offs)]
        ps, alphas = [], []
        for b, off in enumerate(offs):
            s = ss[b]
            if masked:
                kc = lax.broadcasted_iota(jnp.int32, s.shape, 0)
                qc = lax.broadcasted_iota(jnp.int32, s.shape, 1)
                s = jnp.where((kc // CHUNK) <= (qc // CHUNK), s, -jnp.inf)
            m_old = m[:, off:]
            m_new = jnp.maximum(m_old, jnp.max(s, axis=0, keepdims=True))
            alphas.append(jnp.exp2(m_old - m_new))
            ps.append(jnp.exp2(s - m_new).astype(BF16))
            m = m_new if off == 0 else jnp.concatenate([m[:, :off], m_new], axis=1)
        for b, off in enumerate(offs):
            pv = _dot(vt_ref[0, :, pl.ds(starts[b], ATT_K)], ps[b])
            upd = acc[:, off:] * alphas[b] + pv
            acc = upd if off == 0 else jnp.concatenate([acc[:, :off], upd], axis=1)
        m_ref[...] = m
        acc_ref[...] = acc

    lax.fori_loop(0, i, lambda pi, _: pair(pi, False), None)
    pair(i, True)
    acc = acc_ref[...]
    o_ref[...] = (acc[:V_HEAD] / acc[V_HEAD:V_HEAD + 1]).T.astype(BF16)


def _attn(qt, k, vt):
    h, _, s = qt.shape
    assert ATT_Q == ATT_BLOCKS * ATT_K and ATT_K % CHUNK == 0
    return pl.pallas_call(
        _attn_kernel,
        grid=(h, s // ATT_Q),
        in_specs=[pl.BlockSpec((1, QK_PAD, ATT_Q), lambda hh, i: (hh, 0, i)),
                  pl.BlockSpec((1, s, QK_PAD), lambda hh, i: (hh, 0, 0), pipeline_mode=pl.Buffered(1)),
                  pl.BlockSpec((1, V_ROWS, s), lambda hh, i: (hh, 0, 0), pipeline_mode=pl.Buffered(1))],
        out_specs=pl.BlockSpec((ATT_Q, V_HEAD), lambda hh, i: (i, hh)),
        out_shape=jax.ShapeDtypeStruct((s, h * V_HEAD), BF16),
        scratch_shapes=[pltpu.VMEM((1, ATT_Q), F32), pltpu.VMEM((V_ROWS, ATT_Q), F32)],
        compiler_params=_params("parallel", "arbitrary"),
        name="mla_attn",
    )(qt, k, vt)


def _rot_cols(w):
    half = w.shape[-1] // 2
    return jnp.concatenate([-w[..., half:], w[..., :half]], axis=-1)


def _pad_cols(w, width):
    return jnp.pad(w, [(0, 0)] * (w.ndim - 1) + [(0, width - w.shape[-1])])


def _mem_layout(mem_kv_all):
    k = mem_kv_all[:, :, :MEM_WIDTH]
    v = mem_kv_all[:, :, MEM_WIDTH:]
    head_of = jnp.arange(MEM_WIDTH) // MEM_HEAD_DIM
    sel = (head_of[None, :] == jnp.arange(MEM_HEADS)[:, None]).astype(F32)
    kt = (jnp.swapaxes(k, 1, 2)[:, None, :, :] * sel[None, :, :, None]).astype(BF16)
    vp = (v[:, None, :, :] * sel[None, :, None, :]).astype(BF16)
    return kt, vp


def kernel(x, mem, positions, ffn1_norm, ffn1_w_gu, ffn1_w_down, mix_norm, ffn2_norm, ffn2_w_gu,
           ffn2_w_down, w_out, mem_norm, w_mem_kv, a_w_in, a_conv, a_A_log, a_dt_bias, a_out_norm,
           b_w_in, b_q_norm, b_w_uq, kv_in_norm, w_dkv, kv_lat_norm, w_ukv, final_norm):
    b, s, d = x.shape
    assert (b, s, d) == (1, SEQ, D_MODEL)
    xs = x.reshape(s, d)
    row = lambda v: v.reshape(1, -1).astype(F32)

    inv = ROPE_THETA ** (-jnp.arange(0, QK_ROPE, 2, dtype=F32) / QK_ROPE)
    inv_row = _pad_cols(jnp.concatenate([inv, inv])[None, :], LANES)
    cos_t, sin_t = _rope_tables(positions.reshape(s, 1), inv_row)

    mem_kv_all = _mem_kv(mem.reshape(N_MEM, d), row(mem_norm), w_mem_kv.astype(BF16))
    kt, vp = _mem_layout(mem_kv_all)

    rows = lambda v: v.astype(F32)[:, None, :]
    ffn1_n, ffn2_n, mix_n = rows(ffn1_norm), rows(ffn2_norm), rows(mix_norm)
    ffn1_gu, ffn1_d = ffn1_w_gu.astype(BF16), ffn1_w_down.astype(BF16)
    ffn2_gu, ffn2_d = ffn2_w_gu.astype(BF16), ffn2_w_down.astype(BF16)
    w_out_bf = w_out.astype(BF16)

    b_off = 4 * A_WIDTH
    w_a = jnp.concatenate(
        [a_w_in[:, :, :b_off], _pad_cols(a_w_in[:, :, b_off:b_off + A_HEADS], LANES),
         _pad_cols(a_w_in[:, :, b_off + A_HEADS:b_off + 2 * A_HEADS], LANES),
         a_w_in[:, :, b_off + 2 * A_HEADS:]], axis=2).astype(BF16)
    conv_w = a_conv.astype(F32)
    alog_rows = _pad_cols(rows(a_A_log), LANES)
    dtb_rows = _pad_cols(rows(a_dt_bias), LANES)
    gain_rows = rows(a_out_norm)

    for i in range(N_A_LAYERS):
        l = i
        xs = _ffn(xs, l, ffn1_n, ffn1_gu, ffn1_d)
        q, k, v, gate, beta, gcum, o_m = _a_in(xs, l, i, mix_n, w_a, conv_w, alog_rows, dtb_rows, kt, vp)
        o_a = _gdn(q, k, v, gate, beta, gcum, i, gain_rows)
        xs = _ffn(xs, l, ffn2_n, ffn2_gu, ffn2_d, proj=(o_a, o_m, w_out_bf))

    w_c = w_dkv[:, :KV_LORA]
    w_r = w_dkv[:, KV_LORA:]
    wd = jnp.concatenate([w_c, _pad_cols(w_r, LANES), _pad_cols(_rot_cols(w_r), LANES)], axis=1).astype(BF16)
    w_ukv3 = w_ukv.reshape(KV_LORA, B_HEADS, QK_NOPE + V_HEAD)
    wu = jnp.concatenate([w_ukv3[:, :, :QK_NOPE].reshape(KV_LORA, B_HEADS * QK_NOPE),
                          w_ukv3[:, :, QK_NOPE:].reshape(KV_LORA, B_HEADS * V_HEAD)], axis=1).astype(BF16)
    k_all, v_all = _kv(xs, row(kv_in_norm), wd, row(kv_lat_norm), wu, cos_t, sin_t)

    w_uq4 = b_w_uq.reshape(N_B_LAYERS, Q_LORA, B_HEADS, QK_NOPE + QK_ROPE)
    wuq = _pad_cols(w_uq4, QK_PAD).reshape(N_B_LAYERS, Q_LORA, B_HEADS * QK_PAD).astype(BF16)
    wrot = _pad_cols(_rot_cols(w_uq4[..., QK_NOPE:]), LANES).reshape(N_B_LAYERS, Q_LORA, B_HEADS * LANES).astype(BF16)
    b_win = b_w_in.astype(BF16)
    qg_rows = rows(b_q_norm)

    for j in range(N_B_LAYERS):
        l = N_A_LAYERS + j
        xs = _ffn(xs, l, ffn1_n, ffn1_gu, ffn1_d)
        q_all, o_m = _b_in(xs, l, j, mix_n, b_win, qg_rows, wuq, wrot, cos_t, sin_t, kt, vp)
        o_b = _attn(q_all, k_all, v_all)
        last = j == N_B_LAYERS - 1
        xs = _ffn(xs, l, ffn2_n, ffn2_gu, ffn2_d, proj=(o_b, o_m, w_out_bf),
                  final_row=row(final_norm) if last else None)

    return xs.reshape(b, s, d)
```

```python
import functools

import jax
import jax.numpy as jnp
from jax import lax
from jax.experimental import pallas as pl
from jax.experimental.pallas import tpu as pltpu

F32 = jnp.float32
BF16 = jnp.bfloat16

D_MODEL = 1024
SEQ = 16384
DEPTH = 4
CHUNK = 64
EPS = 1e-6
N_A_LAYERS = DEPTH // 2
N_B_LAYERS = DEPTH - N_A_LAYERS

A_HEADS = 6
A_HEAD_DIM = 128
A_WIDTH = A_HEADS * A_HEAD_DIM
CONV_K = 4

B_HEADS = 6
QK_NOPE = 128
QK_ROPE = 64
V_HEAD = 128
Q_LORA = 256
KV_LORA = 256
B_WIDTH = B_HEADS * V_HEAD
ROPE_THETA = 10000.0

N_MEM = 256
MEM_HEADS = 4
MEM_HEAD_DIM = 64
MEM_WIDTH = MEM_HEADS * MEM_HEAD_DIM

D_FF = 2816

LANES = 128
SUBLANES = 8
QK_PAD = 256
VMEM_LIMIT = 56 * 1024 * 1024

FFN_ROWS = 1024
FFN_COLS = 512
A_IN_ROWS = 256
GDN_CHUNKS = 4
GDN_ROWS = GDN_CHUNKS * CHUNK
B_IN_ROWS = 512
KV_ROWS = 512
ATT_Q = 1024
ATT_K = 256
ATT_BLOCKS = ATT_Q // ATT_K
BF16_SUBLANES = 16
V_ROWS = V_HEAD + BF16_SUBLANES
LOG2E = 1.4426950408889634
ROPE_ROWS = 2048


def _params(*sem):
    return pltpu.CompilerParams(dimension_semantics=sem, vmem_limit_bytes=VMEM_LIMIT)


def _const_spec(shape):
    n = len(shape)
    return pl.BlockSpec(shape, lambda *_: (0,) * n, pipeline_mode=pl.Buffered(1))


def _layer_spec(stacked, l):
    n = stacked.ndim - 1
    return pl.BlockSpec((None,) + tuple(stacked.shape[1:]), lambda *_: (l,) + (0,) * n,
                        pipeline_mode=pl.Buffered(1))


def _rms(x, g):
    return x * lax.rsqrt(jnp.mean(x * x, axis=-1, keepdims=True) + EPS) * g


def _sigmoid(x):
    return 1.0 / (1.0 + jnp.exp(-x))


def _dot(a, b):
    return jnp.dot(a, b, preferred_element_type=F32)


def _dot_nt(a, b):
    return lax.dot_general(a, b, (((1,), (1,)), ((), ())), preferred_element_type=F32)


def _dot_tn(a, b):
    return lax.dot_general(a, b, (((0,), (0,)), ((), ())), preferred_element_type=F32)


def _softmax_rows(s):
    m = jnp.max(s, axis=-1, keepdims=True)
    p = jnp.exp(s - m)
    return p / jnp.sum(p, axis=-1, keepdims=True)


def _mem_attention(qm, kt_ref, vp_ref):
    out = None
    for h in range(MEM_HEADS):
        s = _dot(qm, kt_ref[h]) * (MEM_HEAD_DIM ** -0.5)
        p = _softmax_rows(s).astype(BF16)
        o = _dot(p, vp_ref[h])
        out = o if out is None else out + o
    return out


def _rope_kernel(pos_ref, inv_ref, cos_ref, sin_ref):
    ang = pos_ref[...].astype(F32) * inv_ref[...]
    cos_ref[...] = jnp.cos(ang)
    sin_ref[...] = jnp.sin(ang)


def _rope_tables(pos_col, inv_row):
    s = pos_col.shape[0]
    return pl.pallas_call(
        _rope_kernel,
        grid=(s // ROPE_ROWS,),
        in_specs=[pl.BlockSpec((ROPE_ROWS, 1), lambda i: (i, 0)),
                  pl.BlockSpec((1, LANES), lambda i: (0, 0))],
        out_specs=[pl.BlockSpec((ROPE_ROWS, LANES), lambda i: (i, 0))] * 2,
        out_shape=[jax.ShapeDtypeStruct((s, LANES), F32)] * 2,
        compiler_params=_params("parallel"),
        name="rope_tables",
    )(pos_col, inv_row)


def _mem_kv_kernel(mem_ref, g_ref, w_ref, o_ref):
    mn = _rms(mem_ref[...], g_ref[...]).astype(BF16)
    o_ref[0] = _dot(mn, w_ref[0])


def _mem_kv(mem2d, mem_norm_row, w_mem_kv_bf):
    return pl.pallas_call(
        _mem_kv_kernel,
        grid=(DEPTH,),
        in_specs=[pl.BlockSpec((N_MEM, D_MODEL), lambda l: (0, 0)),
                  pl.BlockSpec((1, D_MODEL), lambda l: (0, 0)),
                  pl.BlockSpec((1, D_MODEL, 2 * MEM_WIDTH), lambda l: (l, 0, 0))],
        out_specs=pl.BlockSpec((1, N_MEM, 2 * MEM_WIDTH), lambda l: (l, 0, 0)),
        out_shape=jax.ShapeDtypeStruct((DEPTH, N_MEM, 2 * MEM_WIDTH), F32),
        compiler_params=_params("parallel"),
        name="mem_kv",
    )(mem2d, mem_norm_row, w_mem_kv_bf)


def _ffn_kernel(*refs, has_proj, has_final):
    refs = list(refs)
    x_ref = refs.pop(0)
    if has_proj:
        oa_ref, om_ref, wout_ref = refs[:3]
        refs = refs[3:]
    g_ref, wgu_ref, wd_ref = refs[:3]
    refs = refs[3:]
    if has_final:
        fg_ref = refs.pop(0)
    o_ref, act_ref = refs

    x = x_ref[...]
    if has_proj:
        mix_w = oa_ref.shape[1]
        x = x + _dot(oa_ref[...], wout_ref[:mix_w]) + _dot(om_ref[...], wout_ref[mix_w:])
    xn = _rms(x, g_ref[...]).astype(BF16)
    for c in range(0, D_FF, FFN_COLS):
        w = min(FFN_COLS, D_FF - c)
        gate = _dot(xn, wgu_ref[:, c:c + w])
        up = _dot(xn, wgu_ref[:, D_FF + c:D_FF + c + w])
        act_ref[:, c:c + w] = (gate * _sigmoid(gate) * up).astype(BF16)
    y = x + 0.5 * _dot(act_ref[...], wd_ref[...])
    if has_final:
        y = _rms(y, fg_ref[...])
    o_ref[...] = y


def _ffn(x, l, norms, wgu, wd, proj=None, final_row=None):
    s = x.shape[0]
    row_spec = lambda w: pl.BlockSpec((FFN_ROWS, w), lambda i: (i, 0))
    args, specs = [x], [row_spec(D_MODEL)]
    if proj is not None:
        oa, om, wout = proj
        args += [oa, om, wout]
        specs += [row_spec(oa.shape[1]), row_spec(om.shape[1]), _layer_spec(wout, l)]
    args += [norms, wgu, wd]
    specs += [_layer_spec(norms, l), _layer_spec(wgu, l), _layer_spec(wd, l)]
    if final_row is not None:
        args.append(final_row)
        specs.append(_const_spec(final_row.shape))
    return pl.pallas_call(
        functools.partial(_ffn_kernel, has_proj=proj is not None, has_final=final_row is not None),
        grid=(s // FFN_ROWS,),
        in_specs=specs,
        out_specs=row_spec(D_MODEL),
        out_shape=jax.ShapeDtypeStruct((s, D_MODEL), F32),
        scratch_shapes=[pltpu.VMEM((FFN_ROWS, D_FF), BF16)],
        compiler_params=_params("parallel"),
        name="ffn",
    )(*args)


def _a_in_kernel(x_ref, g_ref, w_ref, conv_ref, alog_ref, dtb_ref, kt_ref, vp_ref,
                 q_ref, k_ref, v_ref, gate_ref, beta_ref, gcum_ref, om_ref, ext_ref):
    tm = A_IN_ROWS
    tail = SUBLANES
    wqkv_ref = w_ref.at[:, :3 * A_WIDTH]
    wgate_ref = w_ref.at[:, 3 * A_WIDTH:4 * A_WIDTH]
    wba_ref = w_ref.at[:, 4 * A_WIDTH:4 * A_WIDTH + 2 * LANES]
    wqm_ref = w_ref.at[:, 4 * A_WIDTH + 2 * LANES:]

    @pl.when(pl.program_id(0) == 0)
    def _():
        ext_ref[0:tail, :] = jnp.zeros((tail, 3 * A_WIDTH), F32)

    xn = _rms(x_ref[...], g_ref[...]).astype(BF16)
    ext_ref[tail:tail + tm, :] = _dot(xn, wqkv_ref[...])
    for b in range(3 * A_HEADS):
        sl = slice(LANES * b, LANES * (b + 1))
        acc = ext_ref[tail:tail + tm, sl] * conv_ref[CONV_K - 1:CONV_K, sl]
        for j in range(1, CONV_K):
            acc = acc + ext_ref[tail - j:tail - j + tm, sl] * conv_ref[CONV_K - 1 - j:CONV_K - j, sl]
        y = acc * _sigmoid(acc)
        if b < 2 * A_HEADS:
            y = y * lax.rsqrt(jnp.sum(y * y, axis=-1, keepdims=True) + EPS)
        if b < A_HEADS:
            q_ref[:, sl] = y * (A_HEAD_DIM ** -0.5)
        elif b < 2 * A_HEADS:
            k_ref[:, LANES * (b - A_HEADS):LANES * (b - A_HEADS + 1)] = y
        else:
            v_ref[:, LANES * (b - 2 * A_HEADS):LANES * (b - 2 * A_HEADS + 1)] = y
    ext_ref[0:tail, :] = ext_ref[tm:tm + tail, :]

    gate = _dot(xn, wgate_ref[...])
    gate_ref[...] = (gate * _sigmoid(gate)).astype(BF16)

    ba = _dot(xn, wba_ref[...])
    beta_ref[...] = _sigmoid(ba[:, :LANES])
    z = ba[:, LANES:] + dtb_ref[...]
    softplus = jnp.maximum(z, 0.0) + jnp.log(1.0 + jnp.exp(-jnp.abs(z)))
    g = -jnp.exp(alog_ref[...]) * softplus
    r = lax.broadcasted_iota(jnp.int32, (tm, tm), 0)
    c = lax.broadcasted_iota(jnp.int32, (tm, tm), 1)
    tri = jnp.where((c <= r) & ((c // CHUNK) == (r // CHUNK)), 1.0, 0.0).astype(BF16)
    g1 = g.astype(BF16)
    g2 = (g - g1.astype(F32)).astype(BF16)
    g3 = (g - g1.astype(F32) - g2.astype(F32)).astype(BF16)
    gcum_ref[...] = _dot(tri, g1) + _dot(tri, g2) + _dot(tri, g3)

    qm = _dot(xn, wqm_ref[...]).astype(BF16)
    om_ref[...] = _mem_attention(qm, kt_ref, vp_ref).astype(BF16)


def _a_in(x, l, i_a, norms, w_a, conv_w, alog_rows, dtb_rows, kt, vp):
    s = x.shape[0]
    tm = A_IN_ROWS
    row_spec = lambda w: pl.BlockSpec((tm, w), lambda i: (i, 0))
    consts = [norms, w_a, conv_w, alog_rows, dtb_rows, kt, vp]
    layer_of = [l, i_a, i_a, i_a, i_a, l, l]
    return pl.pallas_call(
        _a_in_kernel,
        grid=(s // tm,),
        in_specs=[row_spec(D_MODEL)] + [_layer_spec(a, j) for a, j in zip(consts, layer_of)],
        out_specs=[row_spec(A_WIDTH), row_spec(A_WIDTH), row_spec(A_WIDTH), row_spec(A_WIDTH),
                   row_spec(LANES), row_spec(LANES), row_spec(MEM_WIDTH)],
        out_shape=[jax.ShapeDtypeStruct((s, A_WIDTH), F32), jax.ShapeDtypeStruct((s, A_WIDTH), F32),
                   jax.ShapeDtypeStruct((s, A_WIDTH), F32), jax.ShapeDtypeStruct((s, A_WIDTH), BF16),
                   jax.ShapeDtypeStruct((s, LANES), F32), jax.ShapeDtypeStruct((s, LANES), F32),
                   jax.ShapeDtypeStruct((s, MEM_WIDTH), BF16)],
        scratch_shapes=[pltpu.VMEM((tm + SUBLANES, 3 * A_WIDTH), F32)],
        compiler_params=_params("arbitrary"),
        name="a_in",
    )(x, *consts)


def _gdn_kernel(q_ref, k_ref, v_ref, gate_ref, beta_ref, gcum_ref, gain_ref, o_ref, state_ref):
    @pl.when(pl.program_id(0) == 0)
    def _():
        state_ref[...] = jnp.zeros(state_ref.shape, F32)

    rows = GDN_ROWS
    gc_all = gcum_ref[...]
    gc_rows = gc_all.T
    beta_all = beta_ref[...]
    r = lax.broadcasted_iota(jnp.int32, (rows, rows), 0)
    c = lax.broadcasted_iota(jnp.int32, (rows, rows), 1)
    same = (r // CHUNK) == (c // CHUNK)
    causal = same & (c <= r)
    strict = same & (c < r)
    row_chunk = lax.broadcasted_iota(jnp.int32, (rows, A_HEAD_DIM), 0) // CHUNK

    def chunk_columns(x):
        return jnp.concatenate([jnp.where(row_chunk == ci, x, 0.0) for ci in range(GDN_CHUNKS)], axis=1)

    heads = range(A_HEADS)
    sls = [slice(A_HEAD_DIM * h, A_HEAD_DIM * (h + 1)) for h in heads]
    gcs = [gc_all[:, h:h + 1] for h in heads]
    bts = [beta_all[:, h:h + 1] for h in heads]
    g_last = [[gc[CHUNK * (ci + 1) - 1:CHUNK * (ci + 1), :] for ci in range(GDN_CHUNKS)] for gc in gcs]
    ks = [k_ref[:, sl] for sl in sls]
    kbs = [k * bt for k, bt in zip(ks, bts)]
    d1s = [_dot_nt(jnp.concatenate([kb, q_ref[:, sl]], axis=0).astype(BF16), k.astype(BF16))
           for kb, k, sl in zip(kbs, ks, sls)]
    ps, qks, xs = [], [], []
    for h in heads:
        decay = jnp.where(causal, jnp.exp(jnp.where(causal, gcs[h] - gc_rows[h:h + 1, :], 0.0)), 0.0)
        ps.append(jnp.where(strict, d1s[h][:rows] * decay, 0.0).astype(BF16))
        qks.append((d1s[h][rows:] * decay).astype(BF16))
        xs.append(jnp.concatenate([v_ref[:, sls[h]] * bts[h], kbs[h] * jnp.exp(gcs[h])], axis=1))
    sign = -1.0
    pw = 1
    while 2 * pw < CHUNK:
        ds = [_dot(ps[h], jnp.concatenate([xs[h].astype(BF16), ps[h]], axis=1)) for h in heads]
        xs = [xs[h] + sign * ds[h][:, :2 * A_HEAD_DIM] for h in heads]
        ps = [ds[h][:, 2 * A_HEAD_DIM:].astype(BF16) for h in heads]
        sign = 1.0
        pw *= 2
    x_bfs = [(xs[h] + _dot(ps[h], xs[h].astype(BF16))).astype(BF16) for h in heads]
    d2s = [_dot(qks[h], x_bfs[h]) for h in heads]
    d3s = []
    for h in heads:
        g_last_rows = jnp.concatenate([jnp.broadcast_to(g, (CHUNK, 1)) for g in g_last[h]], axis=0)
        k_dec = ks[h] * jnp.exp(g_last_rows - gcs[h])
        d3s.append(_dot_tn(chunk_columns(k_dec).astype(BF16), x_bfs[h]))
    sts = [state_ref[h] for h in heads]
    starts = [[] for _ in heads]
    for ci in range(GDN_CHUNKS):
        for h in heads:
            st_bf = sts[h].astype(BF16)
            starts[h].append(st_bf)
            blk = d3s[h][A_HEAD_DIM * ci:A_HEAD_DIM * (ci + 1)]
            sts[h] = (sts[h] * jnp.exp(g_last[h][ci]) + blk[:, :A_HEAD_DIM]
                      - _dot(blk[:, A_HEAD_DIM:].astype(BF16), st_bf))
    for h in heads:
        state_ref[h] = sts[h]
        q_eff = q_ref[:, sls[h]] * jnp.exp(gcs[h]) - d2s[h][:, A_HEAD_DIM:]
        out = d2s[h][:, :A_HEAD_DIM] + _dot(chunk_columns(q_eff).astype(BF16), jnp.concatenate(starts[h], axis=0))
        o = _rms(out, gain_ref[...]) * gate_ref[:, sls[h]].astype(F32)
        o_ref[:, sls[h]] = o.astype(BF16)


def _gdn(q, k, v, gate, beta, gcum, i_a, gain_rows):
    s = q.shape[0]
    row_spec = lambda w: pl.BlockSpec((GDN_ROWS, w), lambda i: (i, 0))
    return pl.pallas_call(
        _gdn_kernel,
        grid=(s // GDN_ROWS,),
        in_specs=[row_spec(A_WIDTH)] * 4 + [row_spec(LANES)] * 2 + [_layer_spec(gain_rows, i_a)],
        out_specs=row_spec(A_WIDTH),
        out_shape=jax.ShapeDtypeStruct((s, A_WIDTH), BF16),
        scratch_shapes=[pltpu.VMEM((A_HEADS, A_HEAD_DIM, A_HEAD_DIM), F32)],
        compiler_params=_params("arbitrary"),
        name="gdn",
    )(q, k, v, gate, beta, gcum, gain_rows)


def _kv_kernel(x_ref, g_ref, wd_ref, lg_ref, wu_ref, cos_ref, sin_ref, k_ref, vt_ref):
    xn = _rms(x_ref[...], g_ref[...]).astype(BF16)
    ckr = _dot(xn, wd_ref[...])
    cn = _rms(ckr[:, :KV_LORA], lg_ref[...]).astype(BF16)
    kr = ckr[:, KV_LORA:KV_LORA + LANES] * cos_ref[...] + ckr[:, KV_LORA + LANES:] * sin_ref[...]
    kv = _dot(cn, wu_ref[...])
    ones_row = jnp.where(lax.broadcasted_iota(jnp.int32, (V_ROWS - V_HEAD, KV_ROWS), 0) == 0, 1.0, 0.0)
    kr = kr.astype(BF16)
    for h in range(B_HEADS):
        k_ref[h, :, :QK_NOPE] = kv[:, QK_NOPE * h:QK_NOPE * (h + 1)].astype(BF16)
        k_ref[h, :, QK_NOPE:] = kr
        v_h = kv[:, B_HEADS * QK_NOPE + V_HEAD * h:B_HEADS * QK_NOPE + V_HEAD * (h + 1)]
        vt_ref[h] = jnp.concatenate([v_h.T, ones_row], axis=0).astype(BF16)


def _kv(x, norm_row, wd, lat_row, wu, cos_t, sin_t):
    s = x.shape[0]
    tm = KV_ROWS
    consts1 = [norm_row, wd, lat_row, wu]
    return pl.pallas_call(
        _kv_kernel,
        grid=(s // tm,),
        in_specs=[pl.BlockSpec((tm, D_MODEL), lambda i: (i, 0))]
        + [_const_spec(a.shape) for a in consts1]
        + [pl.BlockSpec((tm, LANES), lambda i: (i, 0))] * 2,
        out_specs=[pl.BlockSpec((B_HEADS, tm, QK_PAD), lambda i: (0, i, 0)),
                   pl.BlockSpec((B_HEADS, V_ROWS, tm), lambda i: (0, 0, i))],
        out_shape=[jax.ShapeDtypeStruct((B_HEADS, s, QK_PAD), BF16),
                   jax.ShapeDtypeStruct((B_HEADS, V_ROWS, s), BF16)],
        compiler_params=_params("parallel"),
        name="mla_kv",
    )(x, *consts1, cos_t, sin_t)


def _b_in_kernel(x_ref, g_ref, win_ref, qg_ref, wuq_ref, wrot_ref, cos_ref, sin_ref, kt_ref, vp_ref,
                 qt_ref, om_ref):
    scale = (QK_NOPE + QK_ROPE) ** -0.5 * LOG2E
    xn = _rms(x_ref[...], g_ref[...]).astype(BF16)
    h_in = _dot(xn, win_ref[...])
    cqn = _rms(h_in[:, :Q_LORA], qg_ref[...]).astype(BF16)
    qa = _dot(cqn, wuq_ref[...])
    qb = _dot(cqn, wrot_ref[...])
    cos_t = cos_ref[...]
    sin_t = sin_ref[...]
    for h in range(B_HEADS):
        lo = qa[:, QK_PAD * h:QK_PAD * h + QK_NOPE]
        hi = qa[:, QK_PAD * h + QK_NOPE:QK_PAD * (h + 1)] * cos_t + qb[:, LANES * h:LANES * (h + 1)] * sin_t
        qt_ref[h] = (jnp.concatenate([lo, hi], axis=1) * scale).T.astype(BF16)
    qm = h_in[:, Q_LORA:].astype(BF16)
    om_ref[...] = _mem_attention(qm, kt_ref, vp_ref).astype(BF16)


def _b_in(x, l, j_b, norms, win, qg_rows, wuq, wrot, cos_t, sin_t, kt, vp):
    s = x.shape[0]
    tm = B_IN_ROWS
    consts1 = [norms, win, qg_rows, wuq, wrot]
    consts2 = [kt, vp]
    return pl.pallas_call(
        _b_in_kernel,
        grid=(s // tm,),
        in_specs=[pl.BlockSpec((tm, D_MODEL), lambda i: (i, 0))]
        + [_layer_spec(a, jj) for a, jj in zip(consts1, [l, j_b, j_b, j_b, j_b])]
        + [pl.BlockSpec((tm, LANES), lambda i: (i, 0))] * 2
        + [_layer_spec(a, l) for a in consts2],
        out_specs=[pl.BlockSpec((B_HEADS, QK_PAD, tm), lambda i: (0, 0, i)),
                   pl.BlockSpec((tm, MEM_WIDTH), lambda i: (i, 0))],
        out_shape=[jax.ShapeDtypeStruct((B_HEADS, QK_PAD, s), BF16),
                   jax.ShapeDtypeStruct((s, MEM_WIDTH), BF16)],
        compiler_params=_params("parallel"),
        name="b_in",
    )(x, *consts1, cos_t, sin_t, *consts2)


def _attn_kernel(qt_ref, k_ref, vt_ref, o_ref, m_ref, acc_ref, s_ref, p_ref):
    i = pl.program_id(1)
    qt = qt_ref[0]
    m_ref[...] = jnp.full(m_ref.shape, -jnp.inf, F32)
    acc_ref[...] = jnp.zeros(acc_ref.shape, F32)
    strips = ATT_Q // LANES

    def pair(pi, masked):
        starts = [pl.multiple_of((ATT_BLOCKS * pi + b) * ATT_K, ATT_K) for b in range(ATT_BLOCKS)]
        offs = [b * ATT_K if masked else 0 for b in range(ATT_BLOCKS)]
        for b, (st, off) in enumerate(zip(starts, offs)):
            s_ref[b, :, off:] = _dot(k_ref[0, pl.ds(st, ATT_K), :], qt[:, off:])
        ms = [m_ref[:, LANES * j:LANES * (j + 1)] for j in range(strips)]
        for b, (st, off) in enumerate(zip(starts, offs)):
            alphas = []
            for j in range(off // LANES, strips):
                lo = LANES * j
                s = s_ref[b, :, lo:lo + LANES]
                if masked and lo < off + ATT_K:
                    kc = lax.broadcasted_iota(jnp.int32, s.shape, 0)
                    qc = lax.broadcasted_iota(jnp.int32, s.shape, 1) + (lo - off)
                    s = jnp.where((kc // CHUNK) <= (qc // CHUNK), s, -jnp.inf)
                m_new = jnp.maximum(ms[j], jnp.max(s, axis=0, keepdims=True))
                alphas.append(jnp.exp2(ms[j] - m_new))
                p_ref[b, :, lo:lo + LANES] = jnp.exp2(s - m_new).astype(BF16)
                ms[j] = m_new
            pv = _dot(vt_ref[0, :, pl.ds(st, ATT_K)], p_ref[b, :, off:])
            acc_ref[:, off:] = acc_ref[:, off:] * jnp.concatenate(alphas, axis=1) + pv
        for j in range(strips):
            m_ref[:, LANES * j:LANES * (j + 1)] = ms[j]

    lax.fori_loop(0, i, lambda pi, _: pair(pi, False), None)
    pair(i, True)
    acc = acc_ref[...]
    o_ref[...] = (acc[:V_HEAD] / acc[V_HEAD:V_HEAD + 1]).T.astype(BF16)


def _attn(qt, k, vt):
    h, _, s = qt.shape
    assert ATT_Q == ATT_BLOCKS * ATT_K and ATT_K % CHUNK == 0
    return pl.pallas_call(
        _attn_kernel,
        grid=(h, s // ATT_Q),
        in_specs=[pl.BlockSpec((1, QK_PAD, ATT_Q), lambda hh, i: (hh, 0, i)),
                  pl.BlockSpec((1, s, QK_PAD), lambda hh, i: (hh, 0, 0), pipeline_mode=pl.Buffered(1)),
                  pl.BlockSpec((1, V_ROWS, s), lambda hh, i: (hh, 0, 0), pipeline_mode=pl.Buffered(1))],
        out_specs=pl.BlockSpec((ATT_Q, V_HEAD), lambda hh, i: (i, hh)),
        out_shape=jax.ShapeDtypeStruct((s, h * V_HEAD), BF16),
        scratch_shapes=[pltpu.VMEM((1, ATT_Q), F32), pltpu.VMEM((V_ROWS, ATT_Q), F32),
                        pltpu.VMEM((ATT_BLOCKS, ATT_K, ATT_Q), F32),
                        pltpu.VMEM((ATT_BLOCKS, ATT_K, ATT_Q), BF16)],
        compiler_params=_params("parallel", "arbitrary"),
        name="mla_attn",
    )(qt, k, vt)


def _rot_cols(w):
    half = w.shape[-1] // 2
    return jnp.concatenate([-w[..., half:], w[..., :half]], axis=-1)


def _pad_cols(w, width):
    return jnp.pad(w, [(0, 0)] * (w.ndim - 1) + [(0, width - w.shape[-1])])


def _mem_layout(mem_kv_all):
    k = mem_kv_all[:, :, :MEM_WIDTH]
    v = mem_kv_all[:, :, MEM_WIDTH:]
    head_of = jnp.arange(MEM_WIDTH) // MEM_HEAD_DIM
    sel = (head_of[None, :] == jnp.arange(MEM_HEADS)[:, None]).astype(F32)
    kt = (jnp.swapaxes(k, 1, 2)[:, None, :, :] * sel[None, :, :, None]).astype(BF16)
    vp = (v[:, None, :, :] * sel[None, :, None, :]).astype(BF16)
    return kt, vp


def kernel(x, mem, positions, ffn1_norm, ffn1_w_gu, ffn1_w_down, mix_norm, ffn2_norm, ffn2_w_gu,
           ffn2_w_down, w_out, mem_norm, w_mem_kv, a_w_in, a_conv, a_A_log, a_dt_bias, a_out_norm,
           b_w_in, b_q_norm, b_w_uq, kv_in_norm, w_dkv, kv_lat_norm, w_ukv, final_norm):
    b, s, d = x.shape
    assert (b, s, d) == (1, SEQ, D_MODEL)
    xs = x.reshape(s, d)
    row = lambda v: v.reshape(1, -1).astype(F32)

    inv = ROPE_THETA ** (-jnp.arange(0, QK_ROPE, 2, dtype=F32) / QK_ROPE)
    inv_row = _pad_cols(jnp.concatenate([inv, inv])[None, :], LANES)
    cos_t, sin_t = _rope_tables(positions.reshape(s, 1), inv_row)

    mem_kv_all = _mem_kv(mem.reshape(N_MEM, d), row(mem_norm), w_mem_kv.astype(BF16))
    kt, vp = _mem_layout(mem_kv_all)

    rows = lambda v: v.astype(F32)[:, None, :]
    ffn1_n, ffn2_n, mix_n = rows(ffn1_norm), rows(ffn2_norm), rows(mix_norm)
    ffn1_gu, ffn1_d = ffn1_w_gu.astype(BF16), ffn1_w_down.astype(BF16)
    ffn2_gu, ffn2_d = ffn2_w_gu.astype(BF16), ffn2_w_down.astype(BF16)
    w_out_bf = w_out.astype(BF16)

    b_off = 4 * A_WIDTH
    w_a = jnp.concatenate(
        [a_w_in[:, :, :b_off], _pad_cols(a_w_in[:, :, b_off:b_off + A_HEADS], LANES),
         _pad_cols(a_w_in[:, :, b_off + A_HEADS:b_off + 2 * A_HEADS], LANES),
         a_w_in[:, :, b_off + 2 * A_HEADS:]], axis=2).astype(BF16)
    conv_w = a_conv.astype(F32)
    alog_rows = _pad_cols(rows(a_A_log), LANES)
    dtb_rows = _pad_cols(rows(a_dt_bias), LANES)
    gain_rows = rows(a_out_norm)

    for i in range(N_A_LAYERS):
        l = i
        xs = _ffn(xs, l, ffn1_n, ffn1_gu, ffn1_d)
        q, k, v, gate, beta, gcum, o_m = _a_in(xs, l, i, mix_n, w_a, conv_w, alog_rows, dtb_rows, kt, vp)
        o_a = _gdn(q, k, v, gate, beta, gcum, i, gain_rows)
        xs = _ffn(xs, l, ffn2_n, ffn2_gu, ffn2_d, proj=(o_a, o_m, w_out_bf))

    w_c = w_dkv[:, :KV_LORA]
    w_r = w_dkv[:, KV_LORA:]
    wd = jnp.concatenate([w_c, _pad_cols(w_r, LANES), _pad_cols(_rot_cols(w_r), LANES)], axis=1).astype(BF16)
    w_ukv3 = w_ukv.reshape(KV_LORA, B_HEADS, QK_NOPE + V_HEAD)
    wu = jnp.concatenate([w_ukv3[:, :, :QK_NOPE].reshape(KV_LORA, B_HEADS * QK_NOPE),
                          w_ukv3[:, :, QK_NOPE:].reshape(KV_LORA, B_HEADS * V_HEAD)], axis=1).astype(BF16)
    k_all, v_all = _kv(xs, row(kv_in_norm), wd, row(kv_lat_norm), wu, cos_t, sin_t)

    w_uq4 = b_w_uq.reshape(N_B_LAYERS, Q_LORA, B_HEADS, QK_NOPE + QK_ROPE)
    wuq = _pad_cols(w_uq4, QK_PAD).reshape(N_B_LAYERS, Q_LORA, B_HEADS * QK_PAD).astype(BF16)
    wrot = _pad_cols(_rot_cols(w_uq4[..., QK_NOPE:]), LANES).reshape(N_B_LAYERS, Q_LORA, B_HEADS * LANES).astype(BF16)
    b_win = b_w_in.astype(BF16)
    qg_rows = rows(b_q_norm)

    for j in range(N_B_LAYERS):
        l = N_A_LAYERS + j
        xs = _ffn(xs, l, ffn1_n, ffn1_gu, ffn1_d)
        q_all, o_m = _b_in(xs, l, j, mix_n, b_win, qg_rows, wuq, wrot, cos_t, sin_t, kt, vp)
        o_b = _attn(q_all, k_all, v_all)
        last = j == N_B_LAYERS - 1
        xs = _ffn(xs, l, ffn2_n, ffn2_gu, ffn2_d, proj=(o_b, o_m, w_out_bf),
                  final_row=row(final_norm) if last else None)

    return xs.reshape(b, s, d)
```

```python
import functools

import jax
import jax.numpy as jnp
from jax import lax
from jax.experimental import pallas as pl
from jax.experimental.pallas import tpu as pltpu

F32 = jnp.float32
BF16 = jnp.bfloat16

D_MODEL = 1024
SEQ = 16384
DEPTH = 4
CHUNK = 64
EPS = 1e-6
N_A_LAYERS = DEPTH // 2
N_B_LAYERS = DEPTH - N_A_LAYERS

A_HEADS = 6
A_HEAD_DIM = 128
A_WIDTH = A_HEADS * A_HEAD_DIM
CONV_K = 4

B_HEADS = 6
QK_NOPE = 128
QK_ROPE = 64
V_HEAD = 128
Q_LORA = 256
KV_LORA = 256
B_WIDTH = B_HEADS * V_HEAD
ROPE_THETA = 10000.0

N_MEM = 256
MEM_HEADS = 4
MEM_HEAD_DIM = 64
MEM_WIDTH = MEM_HEADS * MEM_HEAD_DIM

D_FF = 2816

LANES = 128
SUBLANES = 8
QK_PAD = 256
VMEM_LIMIT = 56 * 1024 * 1024

FFN_ROWS = 1024
FFN_COLS = 512
A_IN_ROWS = 256
GDN_CHUNKS = 4
GDN_ROWS = GDN_CHUNKS * CHUNK
B_IN_ROWS = 512
KV_ROWS = 512
ATT_Q = 1024
ATT_K = 256
ATT_BLOCKS = ATT_Q // ATT_K
BF16_SUBLANES = 16
V_ROWS = V_HEAD + BF16_SUBLANES
LOG2E = 1.4426950408889634
ROPE_ROWS = 2048


def _params(*sem):
    return pltpu.CompilerParams(dimension_semantics=sem, vmem_limit_bytes=VMEM_LIMIT)


def _const_spec(shape):
    n = len(shape)
    return pl.BlockSpec(shape, lambda *_: (0,) * n, pipeline_mode=pl.Buffered(1))


def _layer_spec(stacked, l):
    n = stacked.ndim - 1
    return pl.BlockSpec((None,) + tuple(stacked.shape[1:]), lambda *_: (l,) + (0,) * n,
                        pipeline_mode=pl.Buffered(1))


def _rms(x, g):
    return x * lax.rsqrt(jnp.mean(x * x, axis=-1, keepdims=True) + EPS) * g


def _sigmoid(x):
    return 1.0 / (1.0 + jnp.exp(-x))


def _dot(a, b):
    return jnp.dot(a, b, preferred_element_type=F32)


def _dot_nt(a, b):
    return lax.dot_general(a, b, (((1,), (1,)), ((), ())), preferred_element_type=F32)


def _dot_tn(a, b):
    return lax.dot_general(a, b, (((0,), (0,)), ((), ())), preferred_element_type=F32)


def _softmax_rows(s):
    m = jnp.max(s, axis=-1, keepdims=True)
    p = jnp.exp(s - m)
    return p / jnp.sum(p, axis=-1, keepdims=True)


def _mem_attention(qm, kt_ref, vp_ref):
    out = None
    for h in range(MEM_HEADS):
        s = _dot(qm, kt_ref[h]) * (MEM_HEAD_DIM ** -0.5)
        p = _softmax_rows(s).astype(BF16)
        o = _dot(p, vp_ref[h])
        out = o if out is None else out + o
    return out


def _rope_kernel(pos_ref, inv_ref, cos_ref, sin_ref):
    ang = pos_ref[...].astype(F32) * inv_ref[...]
    cos_ref[...] = jnp.cos(ang)
    sin_ref[...] = jnp.sin(ang)


def _rope_tables(pos_col, inv_row):
    s = pos_col.shape[0]
    return pl.pallas_call(
        _rope_kernel,
        grid=(s // ROPE_ROWS,),
        in_specs=[pl.BlockSpec((ROPE_ROWS, 1), lambda i: (i, 0)),
                  pl.BlockSpec((1, LANES), lambda i: (0, 0))],
        out_specs=[pl.BlockSpec((ROPE_ROWS, LANES), lambda i: (i, 0))] * 2,
        out_shape=[jax.ShapeDtypeStruct((s, LANES), F32)] * 2,
        compiler_params=_params("parallel"),
        name="rope_tables",
    )(pos_col, inv_row)


def _mem_kv_kernel(mem_ref, g_ref, w_ref, o_ref):
    mn = _rms(mem_ref[...], g_ref[...]).astype(BF16)
    o_ref[0] = _dot(mn, w_ref[0])


def _mem_kv(mem2d, mem_norm_row, w_mem_kv_bf):
    return pl.pallas_call(
        _mem_kv_kernel,
        grid=(DEPTH,),
        in_specs=[pl.BlockSpec((N_MEM, D_MODEL), lambda l: (0, 0)),
                  pl.BlockSpec((1, D_MODEL), lambda l: (0, 0)),
                  pl.BlockSpec((1, D_MODEL, 2 * MEM_WIDTH), lambda l: (l, 0, 0))],
        out_specs=pl.BlockSpec((1, N_MEM, 2 * MEM_WIDTH), lambda l: (l, 0, 0)),
        out_shape=jax.ShapeDtypeStruct((DEPTH, N_MEM, 2 * MEM_WIDTH), F32),
        compiler_params=_params("parallel"),
        name="mem_kv",
    )(mem2d, mem_norm_row, w_mem_kv_bf)


def _ffn_kernel(*refs, has_proj, has_final):
    refs = list(refs)
    x_ref = refs.pop(0)
    if has_proj:
        oa_ref, om_ref, wout_ref = refs[:3]
        refs = refs[3:]
    g_ref, wgu_ref, wd_ref = refs[:3]
    refs = refs[3:]
    if has_final:
        fg_ref = refs.pop(0)
    o_ref, act_ref = refs

    x = x_ref[...]
    if has_proj:
        mix_w = oa_ref.shape[1]
        x = x + _dot(oa_ref[...], wout_ref[:mix_w]) + _dot(om_ref[...], wout_ref[mix_w:])
    xn = _rms(x, g_ref[...]).astype(BF16)
    for c in range(0, D_FF, FFN_COLS):
        w = min(FFN_COLS, D_FF - c)
        gate = _dot(xn, wgu_ref[:, c:c + w])
        up = _dot(xn, wgu_ref[:, D_FF + c:D_FF + c + w])
        act_ref[:, c:c + w] = (gate * _sigmoid(gate) * up).astype(BF16)
    y = x + 0.5 * _dot(act_ref[...], wd_ref[...])
    if has_final:
        y = _rms(y, fg_ref[...])
    o_ref[...] = y


def _ffn(x, l, norms, wgu, wd, proj=None, final_row=None):
    s = x.shape[0]
    row_spec = lambda w: pl.BlockSpec((FFN_ROWS, w), lambda i: (i, 0))
    args, specs = [x], [row_spec(D_MODEL)]
    if proj is not None:
        oa, om, wout = proj
        args += [oa, om, wout]
        specs += [row_spec(oa.shape[1]), row_spec(om.shape[1]), _layer_spec(wout, l)]
    args += [norms, wgu, wd]
    specs += [_layer_spec(norms, l), _layer_spec(wgu, l), _layer_spec(wd, l)]
    if final_row is not None:
        args.append(final_row)
        specs.append(_const_spec(final_row.shape))
    return pl.pallas_call(
        functools.partial(_ffn_kernel, has_proj=proj is not None, has_final=final_row is not None),
        grid=(s // FFN_ROWS,),
        in_specs=specs,
        out_specs=row_spec(D_MODEL),
        out_shape=jax.ShapeDtypeStruct((s, D_MODEL), F32),
        scratch_shapes=[pltpu.VMEM((FFN_ROWS, D_FF), BF16)],
        compiler_params=_params("parallel"),
        name="ffn",
    )(*args)


def _a_in_kernel(x_ref, g_ref, w_ref, conv_ref, alog_ref, dtb_ref, kt_ref, vp_ref,
                 q_ref, k_ref, v_ref, gate_ref, beta_ref, gcum_ref, om_ref, ext_ref):
    tm = A_IN_ROWS
    tail = SUBLANES
    wqkv_ref = w_ref.at[:, :3 * A_WIDTH]
    wgate_ref = w_ref.at[:, 3 * A_WIDTH:4 * A_WIDTH]
    wba_ref = w_ref.at[:, 4 * A_WIDTH:4 * A_WIDTH + 2 * LANES]
    wqm_ref = w_ref.at[:, 4 * A_WIDTH + 2 * LANES:]

    @pl.when(pl.program_id(0) == 0)
    def _():
        ext_ref[0:tail, :] = jnp.zeros((tail, 3 * A_WIDTH), F32)

    xn = _rms(x_ref[...], g_ref[...]).astype(BF16)
    ext_ref[tail:tail + tm, :] = _dot(xn, wqkv_ref[...])
    for b in range(3 * A_HEADS):
        sl = slice(LANES * b, LANES * (b + 1))
        acc = ext_ref[tail:tail + tm, sl] * conv_ref[CONV_K - 1:CONV_K, sl]
        for j in range(1, CONV_K):
            acc = acc + ext_ref[tail - j:tail - j + tm, sl] * conv_ref[CONV_K - 1 - j:CONV_K - j, sl]
        y = acc * _sigmoid(acc)
        if b < 2 * A_HEADS:
            y = y * lax.rsqrt(jnp.sum(y * y, axis=-1, keepdims=True) + EPS)
        if b < A_HEADS:
            q_ref[:, sl] = y * (A_HEAD_DIM ** -0.5)
        elif b < 2 * A_HEADS:
            k_ref[:, LANES * (b - A_HEADS):LANES * (b - A_HEADS + 1)] = y
        else:
            v_ref[:, LANES * (b - 2 * A_HEADS):LANES * (b - 2 * A_HEADS + 1)] = y
    ext_ref[0:tail, :] = ext_ref[tm:tm + tail, :]

    gate = _dot(xn, wgate_ref[...])
    gate_ref[...] = (gate * _sigmoid(gate)).astype(BF16)

    ba = _dot(xn, wba_ref[...])
    beta_ref[...] = _sigmoid(ba[:, :LANES])
    z = ba[:, LANES:] + dtb_ref[...]
    softplus = jnp.maximum(z, 0.0) + jnp.log(1.0 + jnp.exp(-jnp.abs(z)))
    g = -jnp.exp(alog_ref[...]) * softplus
    r = lax.broadcasted_iota(jnp.int32, (tm, tm), 0)
    c = lax.broadcasted_iota(jnp.int32, (tm, tm), 1)
    tri = jnp.where((c <= r) & ((c // CHUNK) == (r // CHUNK)), 1.0, 0.0).astype(BF16)
    g1 = g.astype(BF16)
    g2 = (g - g1.astype(F32)).astype(BF16)
    g3 = (g - g1.astype(F32) - g2.astype(F32)).astype(BF16)
    gcum_ref[...] = _dot(tri, g1) + _dot(tri, g2) + _dot(tri, g3)

    qm = _dot(xn, wqm_ref[...]).astype(BF16)
    om_ref[...] = _mem_attention(qm, kt_ref, vp_ref).astype(BF16)


def _a_in(x, l, i_a, norms, w_a, conv_w, alog_rows, dtb_rows, kt, vp):
    s = x.shape[0]
    tm = A_IN_ROWS
    row_spec = lambda w: pl.BlockSpec((tm, w), lambda i: (i, 0))
    consts = [norms, w_a, conv_w, alog_rows, dtb_rows, kt, vp]
    layer_of = [l, i_a, i_a, i_a, i_a, l, l]
    return pl.pallas_call(
        _a_in_kernel,
        grid=(s // tm,),
        in_specs=[row_spec(D_MODEL)] + [_layer_spec(a, j) for a, j in zip(consts, layer_of)],
        out_specs=[row_spec(A_WIDTH), row_spec(A_WIDTH), row_spec(A_WIDTH), row_spec(A_WIDTH),
                   row_spec(LANES), row_spec(LANES), row_spec(MEM_WIDTH)],
        out_shape=[jax.ShapeDtypeStruct((s, A_WIDTH), F32), jax.ShapeDtypeStruct((s, A_WIDTH), F32),
                   jax.ShapeDtypeStruct((s, A_WIDTH), F32), jax.ShapeDtypeStruct((s, A_WIDTH), BF16),
                   jax.ShapeDtypeStruct((s, LANES), F32), jax.ShapeDtypeStruct((s, LANES), F32),
                   jax.ShapeDtypeStruct((s, MEM_WIDTH), BF16)],
        scratch_shapes=[pltpu.VMEM((tm + SUBLANES, 3 * A_WIDTH), F32)],
        compiler_params=_params("arbitrary"),
        name="a_in",
    )(x, *consts)


def _gdn_kernel(q_ref, k_ref, v_ref, gate_ref, beta_ref, gcum_ref, gain_ref, o_ref, state_ref):
    @pl.when(pl.program_id(0) == 0)
    def _():
        state_ref[...] = jnp.zeros(state_ref.shape, F32)

    rows = GDN_ROWS
    gc_all = gcum_ref[...]
    gc_rows = gc_all.T
    beta_all = beta_ref[...]
    r = lax.broadcasted_iota(jnp.int32, (rows, rows), 0)
    c = lax.broadcasted_iota(jnp.int32, (rows, rows), 1)
    same = (r // CHUNK) == (c // CHUNK)
    causal = same & (c <= r)
    strict = same & (c < r)
    row_chunk = lax.broadcasted_iota(jnp.int32, (rows, A_HEAD_DIM), 0) // CHUNK

    def chunk_columns(x):
        return jnp.concatenate([jnp.where(row_chunk == ci, x, 0.0) for ci in range(GDN_CHUNKS)], axis=1)

    heads = range(A_HEADS)
    sls = [slice(A_HEAD_DIM * h, A_HEAD_DIM * (h + 1)) for h in heads]
    gcs = [gc_all[:, h:h + 1] for h in heads]
    bts = [beta_all[:, h:h + 1] for h in heads]
    g_last = [[gc[CHUNK * (ci + 1) - 1:CHUNK * (ci + 1), :] for ci in range(GDN_CHUNKS)] for gc in gcs]
    ks = [k_ref[:, sl] for sl in sls]
    kbs = [k * bt for k, bt in zip(ks, bts)]
    d1s = [_dot_nt(jnp.concatenate([kb, q_ref[:, sl]], axis=0).astype(BF16), k.astype(BF16))
           for kb, k, sl in zip(kbs, ks, sls)]
    ps, qks, xs = [], [], []
    for h in heads:
        decay = jnp.where(causal, jnp.exp(jnp.where(causal, gcs[h] - gc_rows[h:h + 1, :], 0.0)), 0.0)
        ps.append(jnp.where(strict, d1s[h][:rows] * decay, 0.0).astype(BF16))
        qks.append((d1s[h][rows:] * decay).astype(BF16))
        xs.append(jnp.concatenate([v_ref[:, sls[h]] * bts[h], kbs[h] * jnp.exp(gcs[h])], axis=1))
    sign = -1.0
    pw = 1
    while 2 * pw < CHUNK:
        ds = [_dot(ps[h], jnp.concatenate([xs[h].astype(BF16), ps[h]], axis=1)) for h in heads]
        xs = [xs[h] + sign * ds[h][:, :2 * A_HEAD_DIM] for h in heads]
        ps = [ds[h][:, 2 * A_HEAD_DIM:].astype(BF16) for h in heads]
        sign = 1.0
        pw *= 2
    x_bfs = [(xs[h] + _dot(ps[h], xs[h].astype(BF16))).astype(BF16) for h in heads]
    d2s = [_dot(qks[h], x_bfs[h]) for h in heads]
    d3s = []
    for h in heads:
        g_last_rows = jnp.concatenate([jnp.broadcast_to(g, (CHUNK, 1)) for g in g_last[h]], axis=0)
        k_dec = ks[h] * jnp.exp(g_last_rows - gcs[h])
        d3s.append(_dot_tn(chunk_columns(k_dec).astype(BF16), x_bfs[h]))
    sts = [state_ref[h] for h in heads]
    starts = [[] for _ in heads]
    for ci in range(GDN_CHUNKS):
        for h in heads:
            st_bf = sts[h].astype(BF16)
            starts[h].append(st_bf)
            blk = d3s[h][A_HEAD_DIM * ci:A_HEAD_DIM * (ci + 1)]
            sts[h] = (sts[h] * jnp.exp(g_last[h][ci]) + blk[:, :A_HEAD_DIM]
                      - _dot(blk[:, A_HEAD_DIM:].astype(BF16), st_bf))
    for h in heads:
        state_ref[h] = sts[h]
        q_eff = q_ref[:, sls[h]] * jnp.exp(gcs[h]) - d2s[h][:, A_HEAD_DIM:]
        out = d2s[h][:, :A_HEAD_DIM] + _dot(chunk_columns(q_eff).astype(BF16), jnp.concatenate(starts[h], axis=0))
        o = _rms(out, gain_ref[...]) * gate_ref[:, sls[h]].astype(F32)
        o_ref[:, sls[h]] = o.astype(BF16)


def _gdn(q, k, v, gate, beta, gcum, i_a, gain_rows):
    s = q.shape[0]
    row_spec = lambda w: pl.BlockSpec((GDN_ROWS, w), lambda i: (i, 0))
    return pl.pallas_call(
        _gdn_kernel,
        grid=(s // GDN_ROWS,),
        in_specs=[row_spec(A_WIDTH)] * 4 + [row_spec(LANES)] * 2 + [_layer_spec(gain_rows, i_a)],
        out_specs=row_spec(A_WIDTH),
        out_shape=jax.ShapeDtypeStruct((s, A_WIDTH), BF16),
        scratch_shapes=[pltpu.VMEM((A_HEADS, A_HEAD_DIM, A_HEAD_DIM), F32)],
        compiler_params=_params("arbitrary"),
        name="gdn",
    )(q, k, v, gate, beta, gcum, gain_rows)


def _kv_kernel(x_ref, g_ref, wd_ref, lg_ref, wu_ref, cos_ref, sin_ref, k_ref, vt_ref):
    xn = _rms(x_ref[...], g_ref[...]).astype(BF16)
    ckr = _dot(xn, wd_ref[...])
    cn = _rms(ckr[:, :KV_LORA], lg_ref[...]).astype(BF16)
    kr = ckr[:, KV_LORA:KV_LORA + LANES] * cos_ref[...] + ckr[:, KV_LORA + LANES:] * sin_ref[...]
    kv = _dot(cn, wu_ref[...])
    ones_row = jnp.where(lax.broadcasted_iota(jnp.int32, (V_ROWS - V_HEAD, KV_ROWS), 0) == 0, 1.0, 0.0)
    kr = kr.astype(BF16)
    for h in range(B_HEADS):
        k_ref[h, :, :QK_NOPE] = kv[:, QK_NOPE * h:QK_NOPE * (h + 1)].astype(BF16)
        k_ref[h, :, QK_NOPE:] = kr
        v_h = kv[:, B_HEADS * QK_NOPE + V_HEAD * h:B_HEADS * QK_NOPE + V_HEAD * (h + 1)]
        vt_ref[h] = jnp.concatenate([v_h.T, ones_row], axis=0).astype(BF16)


def _kv(x, norm_row, wd, lat_row, wu, cos_t, sin_t):
    s = x.shape[0]
    tm = KV_ROWS
    consts1 = [norm_row, wd, lat_row, wu]
    return pl.pallas_call(
        _kv_kernel,
        grid=(s // tm,),
        in_specs=[pl.BlockSpec((tm, D_MODEL), lambda i: (i, 0))]
        + [_const_spec(a.shape) for a in consts1]
        + [pl.BlockSpec((tm, LANES), lambda i: (i, 0))] * 2,
        out_specs=[pl.BlockSpec((B_HEADS, tm, QK_PAD), lambda i: (0, i, 0)),
                   pl.BlockSpec((B_HEADS, V_ROWS, tm), lambda i: (0, 0, i))],
        out_shape=[jax.ShapeDtypeStruct((B_HEADS, s, QK_PAD), BF16),
                   jax.ShapeDtypeStruct((B_HEADS, V_ROWS, s), BF16)],
        compiler_params=_params("parallel"),
        name="mla_kv",
    )(x, *consts1, cos_t, sin_t)


def _b_in_kernel(x_ref, g_ref, win_ref, qg_ref, wuq_ref, wrot_ref, cos_ref, sin_ref, kt_ref, vp_ref,
                 qt_ref, om_ref):
    scale = (QK_NOPE + QK_ROPE) ** -0.5 * LOG2E
    xn = _rms(x_ref[...], g_ref[...]).astype(BF16)
    h_in = _dot(xn, win_ref[...])
    cqn = _rms(h_in[:, :Q_LORA], qg_ref[...]).astype(BF16)
    qa = _dot(cqn, wuq_ref[...])
    qb = _dot(cqn, wrot_ref[...])
    cos_t = cos_ref[...]
    sin_t = sin_ref[...]
    for h in range(B_HEADS):
        lo = qa[:, QK_PAD * h:QK_PAD * h + QK_NOPE]
        hi = qa[:, QK_PAD * h + QK_NOPE:QK_PAD * (h + 1)] * cos_t + qb[:, LANES * h:LANES * (h + 1)] * sin_t
        qt_ref[h] = (jnp.concatenate([lo, hi], axis=1) * scale).T.astype(BF16)
    qm = h_in[:, Q_LORA:].astype(BF16)
    om_ref[...] = _mem_attention(qm, kt_ref, vp_ref).astype(BF16)


def _b_in(x, l, j_b, norms, win, qg_rows, wuq, wrot, cos_t, sin_t, kt, vp):
    s = x.shape[0]
    tm = B_IN_ROWS
    consts1 = [norms, win, qg_rows, wuq, wrot]
    consts2 = [kt, vp]
    return pl.pallas_call(
        _b_in_kernel,
        grid=(s // tm,),
        in_specs=[pl.BlockSpec((tm, D_MODEL), lambda i: (i, 0))]
        + [_layer_spec(a, jj) for a, jj in zip(consts1, [l, j_b, j_b, j_b, j_b])]
        + [pl.BlockSpec((tm, LANES), lambda i: (i, 0))] * 2
        + [_layer_spec(a, l) for a in consts2],
        out_specs=[pl.BlockSpec((B_HEADS, QK_PAD, tm), lambda i: (0, 0, i)),
                   pl.BlockSpec((tm, MEM_WIDTH), lambda i: (i, 0))],
        out_shape=[jax.ShapeDtypeStruct((B_HEADS, QK_PAD, s), BF16),
                   jax.ShapeDtypeStruct((s, MEM_WIDTH), BF16)],
        compiler_params=_params("parallel"),
        name="b_in",
    )(x, *consts1, cos_t, sin_t, *consts2)


def _attn_kernel(qt_ref, k_ref, vt_ref, o_ref, m_ref, acc_ref):
    i = pl.program_id(1)
    qt = qt_ref[0]
    m_ref[...] = jnp.full(m_ref.shape, -jnp.inf, F32)
    acc_ref[...] = jnp.zeros(acc_ref.shape, F32)

    def pair(pi, masked):
        m = m_ref[...]
        acc = acc_ref[...]
        starts = [pl.multiple_of((ATT_BLOCKS * pi + b) * ATT_K, ATT_K) for b in range(ATT_BLOCKS)]
        offs = [b * ATT_K if masked else 0 for b in range(ATT_BLOCKS)]
        ss = [_dot(k_ref[0, pl.ds(st, ATT_K), :], qt[:, off:]) for st, off in zip(starts, offs)]
        ps, alphas = [], []
        for b, off in enumerate(offs):
            s = ss[b]
            if masked:
                kc = lax.broadcasted_iota(jnp.int32, s.shape, 0)
                qc = lax.broadcasted_iota(jnp.int32, s.shape, 1)
                s = jnp.where((kc // CHUNK) <= (qc // CHUNK), s, -jnp.inf)
            m_old = m[:, off:]
            m_new = jnp.maximum(m_old, jnp.max(s, axis=0, keepdims=True))
            alphas.append(jnp.exp2(m_old - m_new))
            ps.append(jnp.exp2((s - m_new).astype(BF16)))
            m = m_new if off == 0 else jnp.concatenate([m[:, :off], m_new], axis=1)
        for b, off in enumerate(offs):
            pv = _dot(vt_ref[0, :, pl.ds(starts[b], ATT_K)], ps[b])
            upd = acc[:, off:] * alphas[b] + pv
            acc = upd if off == 0 else jnp.concatenate([acc[:, :off], upd], axis=1)
        m_ref[...] = m
        acc_ref[...] = acc

    lax.fori_loop(0, i, lambda pi, _: pair(pi, False), None)
    pair(i, True)
    acc = acc_ref[...]
    o_ref[...] = (acc[:V_HEAD] / acc[V_HEAD:V_HEAD + 1]).T.astype(BF16)


def _attn(qt, k, vt):
    h, _, s = qt.shape
    assert ATT_Q == ATT_BLOCKS * ATT_K and ATT_K % CHUNK == 0
    return pl.pallas_call(
        _attn_kernel,
        grid=(h, s // ATT_Q),
        in_specs=[pl.BlockSpec((1, QK_PAD, ATT_Q), lambda hh, i: (hh, 0, i)),
                  pl.BlockSpec((1, s, QK_PAD), lambda hh, i: (hh, 0, 0), pipeline_mode=pl.Buffered(1)),
                  pl.BlockSpec((1, V_ROWS, s), lambda hh, i: (hh, 0, 0), pipeline_mode=pl.Buffered(1))],
        out_specs=pl.BlockSpec((ATT_Q, V_HEAD), lambda hh, i: (i, hh)),
        out_shape=jax.ShapeDtypeStruct((s, h * V_HEAD), BF16),
        scratch_shapes=[pltpu.VMEM((1, ATT_Q), F32), pltpu.VMEM((V_ROWS, ATT_Q), F32)],
        compiler_params=_params("parallel", "arbitrary"),
        name="mla_attn",
    )(qt, k, vt)


def _rot_cols(w):
    half = w.shape[-1] // 2
    return jnp.concatenate([-w[..., half:], w[..., :half]], axis=-1)


def _pad_cols(w, width):
    return jnp.pad(w, [(0, 0)] * (w.ndim - 1) + [(0, width - w.shape[-1])])


def _mem_layout(mem_kv_all):
    k = mem_kv_all[:, :, :MEM_WIDTH]
    v = mem_kv_all[:, :, MEM_WIDTH:]
    head_of = jnp.arange(MEM_WIDTH) // MEM_HEAD_DIM
    sel = (head_of[None, :] == jnp.arange(MEM_HEADS)[:, None]).astype(F32)
    kt = (jnp.swapaxes(k, 1, 2)[:, None, :, :] * sel[None, :, :, None]).astype(BF16)
    vp = (v[:, None, :, :] * sel[None, :, None, :]).astype(BF16)
    return kt, vp


def kernel(x, mem, positions, ffn1_norm, ffn1_w_gu, ffn1_w_down, mix_norm, ffn2_norm, ffn2_w_gu,
           ffn2_w_down, w_out, mem_norm, w_mem_kv, a_w_in, a_conv, a_A_log, a_dt_bias, a_out_norm,
           b_w_in, b_q_norm, b_w_uq, kv_in_norm, w_dkv, kv_lat_norm, w_ukv, final_norm):
    b, s, d = x.shape
    assert (b, s, d) == (1, SEQ, D_MODEL)
    xs = x.reshape(s, d)
    row = lambda v: v.reshape(1, -1).astype(F32)

    inv = ROPE_THETA ** (-jnp.arange(0, QK_ROPE, 2, dtype=F32) / QK_ROPE)
    inv_row = _pad_cols(jnp.concatenate([inv, inv])[None, :], LANES)
    cos_t, sin_t = _rope_tables(positions.reshape(s, 1), inv_row)

    mem_kv_all = _mem_kv(mem.reshape(N_MEM, d), row(mem_norm), w_mem_kv.astype(BF16))
    kt, vp = _mem_layout(mem_kv_all)

    rows = lambda v: v.astype(F32)[:, None, :]
    ffn1_n, ffn2_n, mix_n = rows(ffn1_norm), rows(ffn2_norm), rows(mix_norm)
    ffn1_gu, ffn1_d = ffn1_w_gu.astype(BF16), ffn1_w_down.astype(BF16)
    ffn2_gu, ffn2_d = ffn2_w_gu.astype(BF16), ffn2_w_down.astype(BF16)
    w_out_bf = w_out.astype(BF16)

    b_off = 4 * A_WIDTH
    w_a = jnp.concatenate(
        [a_w_in[:, :, :b_off], _pad_cols(a_w_in[:, :, b_off:b_off + A_HEADS], LANES),
         _pad_cols(a_w_in[:, :, b_off + A_HEADS:b_off + 2 * A_HEADS], LANES),
         a_w_in[:, :, b_off + 2 * A_HEADS:]], axis=2).astype(BF16)
    conv_w = a_conv.astype(F32)
    alog_rows = _pad_cols(rows(a_A_log), LANES)
    dtb_rows = _pad_cols(rows(a_dt_bias), LANES)
    gain_rows = rows(a_out_norm)

    for i in range(N_A_LAYERS):
        l = i
        xs = _ffn(xs, l, ffn1_n, ffn1_gu, ffn1_d)
        q, k, v, gate, beta, gcum, o_m = _a_in(xs, l, i, mix_n, w_a, conv_w, alog_rows, dtb_rows, kt, vp)
        o_a = _gdn(q, k, v, gate, beta, gcum, i, gain_rows)
        xs = _ffn(xs, l, ffn2_n, ffn2_gu, ffn2_d, proj=(o_a, o_m, w_out_bf))

    w_c = w_dkv[:, :KV_LORA]
    w_r = w_dkv[:, KV_LORA:]
    wd = jnp.concatenate([w_c, _pad_cols(w_r, LANES), _pad_cols(_rot_cols(w_r), LANES)], axis=1).astype(BF16)
    w_ukv3 = w_ukv.reshape(KV_LORA, B_HEADS, QK_NOPE + V_HEAD)
    wu = jnp.concatenate([w_ukv3[:, :, :QK_NOPE].reshape(KV_LORA, B_HEADS * QK_NOPE),
                          w_ukv3[:, :, QK_NOPE:].reshape(KV_LORA, B_HEADS * V_HEAD)], axis=1).astype(BF16)
    k_all, v_all = _kv(xs, row(kv_in_norm), wd, row(kv_lat_norm), wu, cos_t, sin_t)

    w_uq4 = b_w_uq.reshape(N_B_LAYERS, Q_LORA, B_HEADS, QK_NOPE + QK_ROPE)
    wuq = _pad_cols(w_uq4, QK_PAD).reshape(N_B_LAYERS, Q_LORA, B_HEADS * QK_PAD).astype(BF16)
    wrot = _pad_cols(_rot_cols(w_uq4[..., QK_NOPE:]), LANES).reshape(N_B_LAYERS, Q_LORA, B_HEADS * LANES).astype(BF16)
    b_win = b_w_in.astype(BF16)
    qg_rows = rows(b_q_norm)

    for j in range(N_B_LAYERS):
        l = N_A_LAYERS + j
        xs = _ffn(xs, l, ffn1_n, ffn1_gu, ffn1_d)
        q_all, o_m = _b_in(xs, l, j, mix_n, b_win, qg_rows, wuq, wrot, cos_t, sin_t, kt, vp)
        o_b = _attn(q_all, k_all, v_all)
        last = j == N_B_LAYERS - 1
        xs = _ffn(xs, l, ffn2_n, ffn2_gu, ffn2_d, proj=(o_b, o_m, w_out_bf),
                  final_row=row(final_norm) if last else None)

    return xs.reshape(b, s, d)
```

```python
import functools

import jax
import jax.numpy as jnp
from jax import lax
from jax.experimental import pallas as pl
from jax.experimental.pallas import tpu as pltpu

F32 = jnp.float32
BF16 = jnp.bfloat16

D_MODEL = 1024
SEQ = 16384
DEPTH = 4
CHUNK = 64
EPS = 1e-6
N_A_LAYERS = DEPTH // 2
N_B_LAYERS = DEPTH - N_A_LAYERS

A_HEADS = 6
A_HEAD_DIM = 128
A_WIDTH = A_HEADS * A_HEAD_DIM
CONV_K = 4

B_HEADS = 6
QK_NOPE = 128
QK_ROPE = 64
V_HEAD = 128
Q_LORA = 256
KV_LORA = 256
B_WIDTH = B_HEADS * V_HEAD
ROPE_THETA = 10000.0

N_MEM = 256
MEM_HEADS = 4
MEM_HEAD_DIM = 64
MEM_WIDTH = MEM_HEADS * MEM_HEAD_DIM

D_FF = 2816

LANES = 128
SUBLANES = 8
QK_PAD = 256
VMEM_LIMIT = 56 * 1024 * 1024

FFN_ROWS = 1024
FFN_COLS = 512
A_IN_ROWS = 256
GDN_CHUNKS = 4
GDN_ROWS = GDN_CHUNKS * CHUNK
B_IN_ROWS = 512
KV_ROWS = 512
ATT_Q = 1024
ATT_K = 512
ATT_BLOCKS = ATT_Q // ATT_K
V_PAD = 256
LOG2E = 1.4426950408889634
ROPE_ROWS = 2048


def _params(*sem):
    return pltpu.CompilerParams(dimension_semantics=sem, vmem_limit_bytes=VMEM_LIMIT)


def _const_spec(shape):
    n = len(shape)
    return pl.BlockSpec(shape, lambda *_: (0,) * n, pipeline_mode=pl.Buffered(1))


def _layer_spec(stacked, l):
    n = stacked.ndim - 1
    return pl.BlockSpec((None,) + tuple(stacked.shape[1:]), lambda *_: (l,) + (0,) * n,
                        pipeline_mode=pl.Buffered(1))


def _rms(x, g):
    return x * lax.rsqrt(jnp.mean(x * x, axis=-1, keepdims=True) + EPS) * g


def _sigmoid(x):
    return 1.0 / (1.0 + jnp.exp(-x))


def _dot(a, b):
    return jnp.dot(a, b, preferred_element_type=F32)


def _dot_nt(a, b):
    return lax.dot_general(a, b, (((1,), (1,)), ((), ())), preferred_element_type=F32)


def _dot_tn(a, b):
    return lax.dot_general(a, b, (((0,), (0,)), ((), ())), preferred_element_type=F32)


def _softmax_rows(s):
    m = jnp.max(s, axis=-1, keepdims=True)
    p = jnp.exp(s - m)
    return p / jnp.sum(p, axis=-1, keepdims=True)


def _mem_attention(qm, kt_ref, vp_ref):
    out = None
    for h in range(MEM_HEADS):
        s = _dot(qm, kt_ref[h]) * (MEM_HEAD_DIM ** -0.5)
        p = _softmax_rows(s).astype(BF16)
        o = _dot(p, vp_ref[h])
        out = o if out is None else out + o
    return out


def _rope_kernel(pos_ref, inv_ref, cos_ref, sin_ref):
    ang = pos_ref[...].astype(F32) * inv_ref[...]
    cos_ref[...] = jnp.cos(ang)
    sin_ref[...] = jnp.sin(ang)


def _rope_tables(pos_col, inv_row):
    s = pos_col.shape[0]
    return pl.pallas_call(
        _rope_kernel,
        grid=(s // ROPE_ROWS,),
        in_specs=[pl.BlockSpec((ROPE_ROWS, 1), lambda i: (i, 0)),
                  pl.BlockSpec((1, LANES), lambda i: (0, 0))],
        out_specs=[pl.BlockSpec((ROPE_ROWS, LANES), lambda i: (i, 0))] * 2,
        out_shape=[jax.ShapeDtypeStruct((s, LANES), F32)] * 2,
        compiler_params=_params("parallel"),
        name="rope_tables",
    )(pos_col, inv_row)


def _mem_kv_kernel(mem_ref, g_ref, w_ref, o_ref):
    mn = _rms(mem_ref[...], g_ref[...]).astype(BF16)
    o_ref[0] = _dot(mn, w_ref[0])


def _mem_kv(mem2d, mem_norm_row, w_mem_kv_bf):
    return pl.pallas_call(
        _mem_kv_kernel,
        grid=(DEPTH,),
        in_specs=[pl.BlockSpec((N_MEM, D_MODEL), lambda l: (0, 0)),
                  pl.BlockSpec((1, D_MODEL), lambda l: (0, 0)),
                  pl.BlockSpec((1, D_MODEL, 2 * MEM_WIDTH), lambda l: (l, 0, 0))],
        out_specs=pl.BlockSpec((1, N_MEM, 2 * MEM_WIDTH), lambda l: (l, 0, 0)),
        out_shape=jax.ShapeDtypeStruct((DEPTH, N_MEM, 2 * MEM_WIDTH), F32),
        compiler_params=_params("parallel"),
        name="mem_kv",
    )(mem2d, mem_norm_row, w_mem_kv_bf)


def _ffn_kernel(*refs, has_proj, has_final):
    refs = list(refs)
    x_ref = refs.pop(0)
    if has_proj:
        oa_ref, om_ref, wout_ref = refs[:3]
        refs = refs[3:]
    g_ref, wgu_ref, wd_ref = refs[:3]
    refs = refs[3:]
    if has_final:
        fg_ref = refs.pop(0)
    o_ref, act_ref = refs

    x = x_ref[...]
    if has_proj:
        mix_w = oa_ref.shape[1]
        x = x + _dot(oa_ref[...], wout_ref[:mix_w]) + _dot(om_ref[...], wout_ref[mix_w:])
    xn = _rms(x, g_ref[...]).astype(BF16)
    for c in range(0, D_FF, FFN_COLS):
        w = min(FFN_COLS, D_FF - c)
        gate = _dot(xn, wgu_ref[:, c:c + w])
        up = _dot(xn, wgu_ref[:, D_FF + c:D_FF + c + w])
        act_ref[:, c:c + w] = (gate * _sigmoid(gate) * up).astype(BF16)
    y = x + 0.5 * _dot(act_ref[...], wd_ref[...])
    if has_final:
        y = _rms(y, fg_ref[...])
    o_ref[...] = y


def _ffn(x, l, norms, wgu, wd, proj=None, final_row=None):
    s = x.shape[0]
    row_spec = lambda w: pl.BlockSpec((FFN_ROWS, w), lambda i: (i, 0))
    args, specs = [x], [row_spec(D_MODEL)]
    if proj is not None:
        oa, om, wout = proj
        args += [oa, om, wout]
        specs += [row_spec(oa.shape[1]), row_spec(om.shape[1]), _layer_spec(wout, l)]
    args += [norms, wgu, wd]
    specs += [_layer_spec(norms, l), _layer_spec(wgu, l), _layer_spec(wd, l)]
    if final_row is not None:
        args.append(final_row)
        specs.append(_const_spec(final_row.shape))
    return pl.pallas_call(
        functools.partial(_ffn_kernel, has_proj=proj is not None, has_final=final_row is not None),
        grid=(s // FFN_ROWS,),
        in_specs=specs,
        out_specs=row_spec(D_MODEL),
        out_shape=jax.ShapeDtypeStruct((s, D_MODEL), F32),
        scratch_shapes=[pltpu.VMEM((FFN_ROWS, D_FF), BF16)],
        compiler_params=_params("parallel"),
        name="ffn",
    )(*args)


def _a_in_kernel(x_ref, g_ref, w_ref, conv_ref, alog_ref, dtb_ref, kt_ref, vp_ref,
                 q_ref, k_ref, v_ref, gate_ref, beta_ref, gcum_ref, om_ref, ext_ref):
    tm = A_IN_ROWS
    tail = SUBLANES
    wqkv_ref = w_ref.at[:, :3 * A_WIDTH]
    wgate_ref = w_ref.at[:, 3 * A_WIDTH:4 * A_WIDTH]
    wba_ref = w_ref.at[:, 4 * A_WIDTH:4 * A_WIDTH + 2 * LANES]
    wqm_ref = w_ref.at[:, 4 * A_WIDTH + 2 * LANES:]

    @pl.when(pl.program_id(0) == 0)
    def _():
        ext_ref[0:tail, :] = jnp.zeros((tail, 3 * A_WIDTH), F32)

    xn = _rms(x_ref[...], g_ref[...]).astype(BF16)
    ext_ref[tail:tail + tm, :] = _dot(xn, wqkv_ref[...])
    for b in range(3 * A_HEADS):
        sl = slice(LANES * b, LANES * (b + 1))
        acc = ext_ref[tail:tail + tm, sl] * conv_ref[CONV_K - 1:CONV_K, sl]
        for j in range(1, CONV_K):
            acc = acc + ext_ref[tail - j:tail - j + tm, sl] * conv_ref[CONV_K - 1 - j:CONV_K - j, sl]
        y = acc * _sigmoid(acc)
        if b < 2 * A_HEADS:
            y = y * lax.rsqrt(jnp.sum(y * y, axis=-1, keepdims=True) + EPS)
        if b < A_HEADS:
            q_ref[:, sl] = y * (A_HEAD_DIM ** -0.5)
        elif b < 2 * A_HEADS:
            k_ref[:, LANES * (b - A_HEADS):LANES * (b - A_HEADS + 1)] = y
        else:
            v_ref[:, LANES * (b - 2 * A_HEADS):LANES * (b - 2 * A_HEADS + 1)] = y
    ext_ref[0:tail, :] = ext_ref[tm:tm + tail, :]

    gate = _dot(xn, wgate_ref[...])
    gate_ref[...] = (gate * _sigmoid(gate)).astype(BF16)

    ba = _dot(xn, wba_ref[...])
    beta_ref[...] = _sigmoid(ba[:, :LANES])
    z = ba[:, LANES:] + dtb_ref[...]
    softplus = jnp.maximum(z, 0.0) + jnp.log(1.0 + jnp.exp(-jnp.abs(z)))
    g = -jnp.exp(alog_ref[...]) * softplus
    r = lax.broadcasted_iota(jnp.int32, (tm, tm), 0)
    c = lax.broadcasted_iota(jnp.int32, (tm, tm), 1)
    tri = jnp.where((c <= r) & ((c // CHUNK) == (r // CHUNK)), 1.0, 0.0).astype(BF16)
    g1 = g.astype(BF16)
    g2 = (g - g1.astype(F32)).astype(BF16)
    g3 = (g - g1.astype(F32) - g2.astype(F32)).astype(BF16)
    gcum_ref[...] = _dot(tri, g1) + _dot(tri, g2) + _dot(tri, g3)

    qm = _dot(xn, wqm_ref[...]).astype(BF16)
    om_ref[...] = _mem_attention(qm, kt_ref, vp_ref).astype(BF16)


def _a_in(x, l, i_a, norms, w_a, conv_w, alog_rows, dtb_rows, kt, vp):
    s = x.shape[0]
    tm = A_IN_ROWS
    row_spec = lambda w: pl.BlockSpec((tm, w), lambda i: (i, 0))
    consts = [norms, w_a, conv_w, alog_rows, dtb_rows, kt, vp]
    layer_of = [l, i_a, i_a, i_a, i_a, l, l]
    return pl.pallas_call(
        _a_in_kernel,
        grid=(s // tm,),
        in_specs=[row_spec(D_MODEL)] + [_layer_spec(a, j) for a, j in zip(consts, layer_of)],
        out_specs=[row_spec(A_WIDTH), row_spec(A_WIDTH), row_spec(A_WIDTH), row_spec(A_WIDTH),
                   row_spec(LANES), row_spec(LANES), row_spec(MEM_WIDTH)],
        out_shape=[jax.ShapeDtypeStruct((s, A_WIDTH), F32), jax.ShapeDtypeStruct((s, A_WIDTH), F32),
                   jax.ShapeDtypeStruct((s, A_WIDTH), F32), jax.ShapeDtypeStruct((s, A_WIDTH), BF16),
                   jax.ShapeDtypeStruct((s, LANES), F32), jax.ShapeDtypeStruct((s, LANES), F32),
                   jax.ShapeDtypeStruct((s, MEM_WIDTH), BF16)],
        scratch_shapes=[pltpu.VMEM((tm + SUBLANES, 3 * A_WIDTH), F32)],
        compiler_params=_params("arbitrary"),
        name="a_in",
    )(x, *consts)


def _gdn_kernel(q_ref, k_ref, v_ref, gate_ref, beta_ref, gcum_ref, gain_ref, o_ref, state_ref):
    @pl.when(pl.program_id(0) == 0)
    def _():
        state_ref[...] = jnp.zeros(state_ref.shape, F32)

    rows = GDN_ROWS
    gc_all = gcum_ref[...]
    gc_rows = gc_all.T
    beta_all = beta_ref[...]
    r = lax.broadcasted_iota(jnp.int32, (rows, rows), 0)
    c = lax.broadcasted_iota(jnp.int32, (rows, rows), 1)
    same = (r // CHUNK) == (c // CHUNK)
    causal = same & (c <= r)
    strict = same & (c < r)
    row_chunk = lax.broadcasted_iota(jnp.int32, (rows, A_HEAD_DIM), 0) // CHUNK

    def chunk_columns(x):
        return jnp.concatenate([jnp.where(row_chunk == ci, x, 0.0) for ci in range(GDN_CHUNKS)], axis=1)

    heads = range(A_HEADS)
    sls = [slice(A_HEAD_DIM * h, A_HEAD_DIM * (h + 1)) for h in heads]
    gcs = [gc_all[:, h:h + 1] for h in heads]
    bts = [beta_all[:, h:h + 1] for h in heads]
    g_last = [[gc[CHUNK * (ci + 1) - 1:CHUNK * (ci + 1), :] for ci in range(GDN_CHUNKS)] for gc in gcs]
    ks = [k_ref[:, sl] for sl in sls]
    kbs = [k * bt for k, bt in zip(ks, bts)]
    d1s = [_dot_nt(jnp.concatenate([kb, q_ref[:, sl]], axis=0).astype(BF16), k.astype(BF16))
           for kb, k, sl in zip(kbs, ks, sls)]
    ps, qks, xs = [], [], []
    for h in heads:
        decay = jnp.where(causal, jnp.exp(jnp.where(causal, gcs[h] - gc_rows[h:h + 1, :], 0.0)), 0.0)
        ps.append(jnp.where(strict, d1s[h][:rows] * decay, 0.0).astype(BF16))
        qks.append((d1s[h][rows:] * decay).astype(BF16))
        xs.append(jnp.concatenate([v_ref[:, sls[h]] * bts[h], kbs[h] * jnp.exp(gcs[h])], axis=1))
    sign = -1.0
    pw = 1
    while 2 * pw < CHUNK:
        ds = [_dot(ps[h], jnp.concatenate([xs[h].astype(BF16), ps[h]], axis=1)) for h in heads]
        xs = [xs[h] + sign * ds[h][:, :2 * A_HEAD_DIM] for h in heads]
        ps = [ds[h][:, 2 * A_HEAD_DIM:].astype(BF16) for h in heads]
        sign = 1.0
        pw *= 2
    x_bfs = [(xs[h] + _dot(ps[h], xs[h].astype(BF16))).astype(BF16) for h in heads]
    d2s = [_dot(qks[h], x_bfs[h]) for h in heads]
    d3s = []
    for h in heads:
        g_last_rows = jnp.concatenate([jnp.broadcast_to(g, (CHUNK, 1)) for g in g_last[h]], axis=0)
        k_dec = ks[h] * jnp.exp(g_last_rows - gcs[h])
        d3s.append(_dot_tn(chunk_columns(k_dec).astype(BF16), x_bfs[h]))
    sts = [state_ref[h] for h in heads]
    starts = [[] for _ in heads]
    for ci in range(GDN_CHUNKS):
        for h in heads:
            st_bf = sts[h].astype(BF16)
            starts[h].append(st_bf)
            blk = d3s[h][A_HEAD_DIM * ci:A_HEAD_DIM * (ci + 1)]
            sts[h] = (sts[h] * jnp.exp(g_last[h][ci]) + blk[:, :A_HEAD_DIM]
                      - _dot(blk[:, A_HEAD_DIM:].astype(BF16), st_bf))
    for h in heads:
        state_ref[h] = sts[h]
        q_eff = q_ref[:, sls[h]] * jnp.exp(gcs[h]) - d2s[h][:, A_HEAD_DIM:]
        out = d2s[h][:, :A_HEAD_DIM] + _dot(chunk_columns(q_eff).astype(BF16), jnp.concatenate(starts[h], axis=0))
        o = _rms(out, gain_ref[...]) * gate_ref[:, sls[h]].astype(F32)
        o_ref[:, sls[h]] = o.astype(BF16)


def _gdn(q, k, v, gate, beta, gcum, i_a, gain_rows):
    s = q.shape[0]
    row_spec = lambda w: pl.BlockSpec((GDN_ROWS, w), lambda i: (i, 0))
    return pl.pallas_call(
        _gdn_kernel,
        grid=(s // GDN_ROWS,),
        in_specs=[row_spec(A_WIDTH)] * 4 + [row_spec(LANES)] * 2 + [_layer_spec(gain_rows, i_a)],
        out_specs=row_spec(A_WIDTH),
        out_shape=jax.ShapeDtypeStruct((s, A_WIDTH), BF16),
        scratch_shapes=[pltpu.VMEM((A_HEADS, A_HEAD_DIM, A_HEAD_DIM), F32)],
        compiler_params=_params("arbitrary"),
        name="gdn",
    )(q, k, v, gate, beta, gcum, gain_rows)


def _kv_kernel(x_ref, g_ref, wd_ref, lg_ref, wu_ref, cos_ref, sin_ref, kt_ref, v_ref):
    xn = _rms(x_ref[...], g_ref[...]).astype(BF16)
    ckr = _dot(xn, wd_ref[...])
    cn = _rms(ckr[:, :KV_LORA], lg_ref[...]).astype(BF16)
    kr = ckr[:, KV_LORA:KV_LORA + LANES] * cos_ref[...] + ckr[:, KV_LORA + LANES:] * sin_ref[...]
    kv = _dot(cn, wu_ref[...])
    ones_col = jnp.where(lax.broadcasted_iota(jnp.int32, (KV_ROWS, V_PAD - V_HEAD), 1) == 0, 1.0, 0.0)
    for h in range(B_HEADS):
        k_full = jnp.concatenate([kv[:, QK_NOPE * h:QK_NOPE * (h + 1)], kr], axis=1)
        kt_ref[h] = k_full.T.astype(BF16)
        v_h = kv[:, B_HEADS * QK_NOPE + V_HEAD * h:B_HEADS * QK_NOPE + V_HEAD * (h + 1)]
        v_ref[h] = jnp.concatenate([v_h, ones_col], axis=1).astype(BF16)


def _kv(x, norm_row, wd, lat_row, wu, cos_t, sin_t):
    s = x.shape[0]
    tm = KV_ROWS
    consts1 = [norm_row, wd, lat_row, wu]
    return pl.pallas_call(
        _kv_kernel,
        grid=(s // tm,),
        in_specs=[pl.BlockSpec((tm, D_MODEL), lambda i: (i, 0))]
        + [_const_spec(a.shape) for a in consts1]
        + [pl.BlockSpec((tm, LANES), lambda i: (i, 0))] * 2,
        out_specs=[pl.BlockSpec((B_HEADS, QK_PAD, tm), lambda i: (0, 0, i)),
                   pl.BlockSpec((B_HEADS, tm, V_PAD), lambda i: (0, i, 0))],
        out_shape=[jax.ShapeDtypeStruct((B_HEADS, QK_PAD, s), BF16),
                   jax.ShapeDtypeStruct((B_HEADS, s, V_PAD), BF16)],
        compiler_params=_params("parallel"),
        name="mla_kv",
    )(x, *consts1, cos_t, sin_t)


def _b_in_kernel(x_ref, g_ref, win_ref, qg_ref, wuq_ref, wrot_ref, cos_ref, sin_ref, kt_ref, vp_ref,
                 q_ref, om_ref):
    scale = (QK_NOPE + QK_ROPE) ** -0.5 * LOG2E
    xn = _rms(x_ref[...], g_ref[...]).astype(BF16)
    h_in = _dot(xn, win_ref[...])
    cqn = _rms(h_in[:, :Q_LORA], qg_ref[...]).astype(BF16)
    qa = _dot(cqn, wuq_ref[...])
    qb = _dot(cqn, wrot_ref[...])
    cos_t = cos_ref[...]
    sin_t = sin_ref[...]
    for h in range(B_HEADS):
        q_ref[h, :, :QK_NOPE] = (qa[:, QK_PAD * h:QK_PAD * h + QK_NOPE] * scale).astype(BF16)
        hi = qa[:, QK_PAD * h + QK_NOPE:QK_PAD * (h + 1)] * cos_t + qb[:, LANES * h:LANES * (h + 1)] * sin_t
        q_ref[h, :, QK_NOPE:] = (hi * scale).astype(BF16)
    qm = h_in[:, Q_LORA:].astype(BF16)
    om_ref[...] = _mem_attention(qm, kt_ref, vp_ref).astype(BF16)


def _b_in(x, l, j_b, norms, win, qg_rows, wuq, wrot, cos_t, sin_t, kt, vp):
    s = x.shape[0]
    tm = B_IN_ROWS
    consts1 = [norms, win, qg_rows, wuq, wrot]
    consts2 = [kt, vp]
    return pl.pallas_call(
        _b_in_kernel,
        grid=(s // tm,),
        in_specs=[pl.BlockSpec((tm, D_MODEL), lambda i: (i, 0))]
        + [_layer_spec(a, jj) for a, jj in zip(consts1, [l, j_b, j_b, j_b, j_b])]
        + [pl.BlockSpec((tm, LANES), lambda i: (i, 0))] * 2
        + [_layer_spec(a, l) for a in consts2],
        out_specs=[pl.BlockSpec((B_HEADS, tm, QK_PAD), lambda i: (0, i, 0)),
                   pl.BlockSpec((tm, MEM_WIDTH), lambda i: (i, 0))],
        out_shape=[jax.ShapeDtypeStruct((B_HEADS, s, QK_PAD), BF16),
                   jax.ShapeDtypeStruct((s, MEM_WIDTH), BF16)],
        compiler_params=_params("parallel"),
        name="b_in",
    )(x, *consts1, cos_t, sin_t, *consts2)


def _attn_kernel(q_ref, kt_ref, v_ref, o_ref, m_ref, acc_ref):
    i = pl.program_id(1)
    q = q_ref[0]
    m_ref[...] = jnp.full(m_ref.shape, -jnp.inf, F32)
    acc_ref[...] = jnp.zeros(acc_ref.shape, F32)
    lane_tiles = ATT_K // LANES

    def step(pi, masked):
        m = m_ref[...]
        acc = acc_ref[...]
        starts = [pl.multiple_of((ATT_BLOCKS * pi + b) * ATT_K, ATT_K) for b in range(ATT_BLOCKS)]
        offs = [b * ATT_K if masked else 0 for b in range(ATT_BLOCKS)]
        ss = [_dot(q[off:], kt_ref[0, :, pl.ds(st, ATT_K)]) for st, off in zip(starts, offs)]
        ps, alphas = [], []
        for b, off in enumerate(offs):
            s = ss[b]
            if masked:
                qc = lax.broadcasted_iota(jnp.int32, s.shape, 0)
                kc = lax.broadcasted_iota(jnp.int32, s.shape, 1)
                s = jnp.where((kc // CHUNK) <= (qc // CHUNK), s, -jnp.inf)
            mx = s[:, :LANES]
            for t in range(1, lane_tiles):
                mx = jnp.maximum(mx, s[:, LANES * t:LANES * (t + 1)])
            mx = jnp.broadcast_to(jnp.max(mx, axis=-1, keepdims=True), mx.shape)
            m_old = m[off:]
            m_new = jnp.maximum(m_old, mx)
            alphas.append(jnp.exp2(m_old - m_new))
            ps.append(jnp.exp2(s - jnp.concatenate([m_new] * lane_tiles, axis=1)).astype(BF16))
            m = m_new if off == 0 else jnp.concatenate([m[:off], m_new], axis=0)
        for b, off in enumerate(offs):
            pv = _dot(ps[b], v_ref[0, pl.ds(starts[b], ATT_K), :])
            upd = acc[off:] * jnp.concatenate([alphas[b]] * (V_PAD // LANES), axis=1) + pv
            acc = upd if off == 0 else jnp.concatenate([acc[:off], upd], axis=0)
        m_ref[...] = m
        acc_ref[...] = acc

    lax.fori_loop(0, i, lambda pi, _: step(pi, False), None)
    step(i, True)
    acc = acc_ref[...]
    o_ref[...] = (acc[:, :V_HEAD] / acc[:, V_HEAD:V_HEAD + 1]).astype(BF16)


def _attn(q, kt, v):
    h, s, _ = q.shape
    assert ATT_Q == ATT_BLOCKS * ATT_K and ATT_K % CHUNK == 0
    return pl.pallas_call(
        _attn_kernel,
        grid=(h, s // ATT_Q),
        in_specs=[pl.BlockSpec((1, ATT_Q, QK_PAD), lambda hh, i: (hh, i, 0)),
                  pl.BlockSpec((1, QK_PAD, s), lambda hh, i: (hh, 0, 0), pipeline_mode=pl.Buffered(1)),
                  pl.BlockSpec((1, s, V_PAD), lambda hh, i: (hh, 0, 0), pipeline_mode=pl.Buffered(1))],
        out_specs=pl.BlockSpec((ATT_Q, V_HEAD), lambda hh, i: (i, hh)),
        out_shape=jax.ShapeDtypeStruct((s, h * V_HEAD), BF16),
        scratch_shapes=[pltpu.VMEM((ATT_Q, LANES), F32), pltpu.VMEM((ATT_Q, V_PAD), F32)],
        compiler_params=_params("parallel", "arbitrary"),
        name="mla_attn",
    )(q, kt, v)


def _rot_cols(w):
    half = w.shape[-1] // 2
    return jnp.concatenate([-w[..., half:], w[..., :half]], axis=-1)


def _pad_cols(w, width):
    return jnp.pad(w, [(0, 0)] * (w.ndim - 1) + [(0, width - w.shape[-1])])


def _mem_layout(mem_kv_all):
    k = mem_kv_all[:, :, :MEM_WIDTH]
    v = mem_kv_all[:, :, MEM_WIDTH:]
    head_of = jnp.arange(MEM_WIDTH) // MEM_HEAD_DIM
    sel = (head_of[None, :] == jnp.arange(MEM_HEADS)[:, None]).astype(F32)
    kt = (jnp.swapaxes(k, 1, 2)[:, None, :, :] * sel[None, :, :, None]).astype(BF16)
    vp = (v[:, None, :, :] * sel[None, :, None, :]).astype(BF16)
    return kt, vp


def kernel(x, mem, positions, ffn1_norm, ffn1_w_gu, ffn1_w_down, mix_norm, ffn2_norm, ffn2_w_gu,
           ffn2_w_down, w_out, mem_norm, w_mem_kv, a_w_in, a_conv, a_A_log, a_dt_bias, a_out_norm,
           b_w_in, b_q_norm, b_w_uq, kv_in_norm, w_dkv, kv_lat_norm, w_ukv, final_norm):
    b, s, d = x.shape
    assert (b, s, d) == (1, SEQ, D_MODEL)
    xs = x.reshape(s, d)
    row = lambda v: v.reshape(1, -1).astype(F32)

    inv = ROPE_THETA ** (-jnp.arange(0, QK_ROPE, 2, dtype=F32) / QK_ROPE)
    inv_row = _pad_cols(jnp.concatenate([inv, inv])[None, :], LANES)
    cos_t, sin_t = _rope_tables(positions.reshape(s, 1), inv_row)

    mem_kv_all = _mem_kv(mem.reshape(N_MEM, d), row(mem_norm), w_mem_kv.astype(BF16))
    kt, vp = _mem_layout(mem_kv_all)

    rows = lambda v: v.astype(F32)[:, None, :]
    ffn1_n, ffn2_n, mix_n = rows(ffn1_norm), rows(ffn2_norm), rows(mix_norm)
    ffn1_gu, ffn1_d = ffn1_w_gu.astype(BF16), ffn1_w_down.astype(BF16)
    ffn2_gu, ffn2_d = ffn2_w_gu.astype(BF16), ffn2_w_down.astype(BF16)
    w_out_bf = w_out.astype(BF16)

    b_off = 4 * A_WIDTH
    w_a = jnp.concatenate(
        [a_w_in[:, :, :b_off], _pad_cols(a_w_in[:, :, b_off:b_off + A_HEADS], LANES),
         _pad_cols(a_w_in[:, :, b_off + A_HEADS:b_off + 2 * A_HEADS], LANES),
         a_w_in[:, :, b_off + 2 * A_HEADS:]], axis=2).astype(BF16)
    conv_w = a_conv.astype(F32)
    alog_rows = _pad_cols(rows(a_A_log), LANES)
    dtb_rows = _pad_cols(rows(a_dt_bias), LANES)
    gain_rows = rows(a_out_norm)

    for i in range(N_A_LAYERS):
        l = i
        xs = _ffn(xs, l, ffn1_n, ffn1_gu, ffn1_d)
        q, k, v, gate, beta, gcum, o_m = _a_in(xs, l, i, mix_n, w_a, conv_w, alog_rows, dtb_rows, kt, vp)
        o_a = _gdn(q, k, v, gate, beta, gcum, i, gain_rows)
        xs = _ffn(xs, l, ffn2_n, ffn2_gu, ffn2_d, proj=(o_a, o_m, w_out_bf))

    w_c = w_dkv[:, :KV_LORA]
    w_r = w_dkv[:, KV_LORA:]
    wd = jnp.concatenate([w_c, _pad_cols(w_r, LANES), _pad_cols(_rot_cols(w_r), LANES)], axis=1).astype(BF16)
    w_ukv3 = w_ukv.reshape(KV_LORA, B_HEADS, QK_NOPE + V_HEAD)
    wu = jnp.concatenate([w_ukv3[:, :, :QK_NOPE].reshape(KV_LORA, B_HEADS * QK_NOPE),
                          w_ukv3[:, :, QK_NOPE:].reshape(KV_LORA, B_HEADS * V_HEAD)], axis=1).astype(BF16)
    k_all, v_all = _kv(xs, row(kv_in_norm), wd, row(kv_lat_norm), wu, cos_t, sin_t)

    w_uq4 = b_w_uq.reshape(N_B_LAYERS, Q_LORA, B_HEADS, QK_NOPE + QK_ROPE)
    wuq = _pad_cols(w_uq4, QK_PAD).reshape(N_B_LAYERS, Q_LORA, B_HEADS * QK_PAD).astype(BF16)
    wrot = _pad_cols(_rot_cols(w_uq4[..., QK_NOPE:]), LANES).reshape(N_B_LAYERS, Q_LORA, B_HEADS * LANES).astype(BF16)
    b_win = b_w_in.astype(BF16)
    qg_rows = rows(b_q_norm)

    for j in range(N_B_LAYERS):
        l = N_A_LAYERS + j
        xs = _ffn(xs, l, ffn1_n, ffn1_gu, ffn1_d)
        q_all, o_m = _b_in(xs, l, j, mix_n, b_win, qg_rows, wuq, wrot, cos_t, sin_t, kt, vp)
        o_b = _attn(q_all, k_all, v_all)
        last = j == N_B_LAYERS - 1
        xs = _ffn(xs, l, ffn2_n, ffn2_gu, ffn2_d, proj=(o_b, o_m, w_out_bf),
                  final_row=row(final_norm) if last else None)

    return xs.reshape(b, s, d)
```

```python
import functools

import jax
import jax.numpy as jnp
from jax import lax
from jax.experimental import pallas as pl
from jax.experimental.pallas import tpu as pltpu

F32 = jnp.float32
BF16 = jnp.bfloat16

D_MODEL = 1024
SEQ = 16384
DEPTH = 4
CHUNK = 64
EPS = 1e-6
N_A_LAYERS = DEPTH // 2
N_B_LAYERS = DEPTH - N_A_LAYERS

A_HEADS = 6
A_HEAD_DIM = 128
A_WIDTH = A_HEADS * A_HEAD_DIM
CONV_K = 4

B_HEADS = 6
QK_NOPE = 128
QK_ROPE = 64
V_HEAD = 128
Q_LORA = 256
KV_LORA = 256
B_WIDTH = B_HEADS * V_HEAD
ROPE_THETA = 10000.0

N_MEM = 256
MEM_HEADS = 4
MEM_HEAD_DIM = 64
MEM_WIDTH = MEM_HEADS * MEM_HEAD_DIM

D_FF = 2816

LANES = 128
SUBLANES = 8
QK_PAD = 256
VMEM_LIMIT = 56 * 1024 * 1024

FFN_ROWS = 1024
FFN_COLS = 512
A_IN_ROWS = 256
GDN_CHUNKS = 4
GDN_ROWS = GDN_CHUNKS * CHUNK
B_IN_ROWS = 512
KV_ROWS = 512
ATT_Q = 1024
ATT_K = 512
ATT_BLOCKS = ATT_Q // ATT_K
ATT_UNROLL = 2
V_PAD = 256
LOG2E = 1.4426950408889634
ROPE_ROWS = 2048


def _params(*sem):
    return pltpu.CompilerParams(dimension_semantics=sem, vmem_limit_bytes=VMEM_LIMIT)


def _const_spec(shape):
    n = len(shape)
    return pl.BlockSpec(shape, lambda *_: (0,) * n, pipeline_mode=pl.Buffered(1))


def _layer_spec(stacked, l):
    n = stacked.ndim - 1
    return pl.BlockSpec((None,) + tuple(stacked.shape[1:]), lambda *_: (l,) + (0,) * n,
                        pipeline_mode=pl.Buffered(1))


def _rms(x, g):
    return x * lax.rsqrt(jnp.mean(x * x, axis=-1, keepdims=True) + EPS) * g


def _sigmoid(x):
    return 1.0 / (1.0 + jnp.exp(-x))


def _dot(a, b):
    return jnp.dot(a, b, preferred_element_type=F32)


def _dot_nt(a, b):
    return lax.dot_general(a, b, (((1,), (1,)), ((), ())), preferred_element_type=F32)


def _dot_tn(a, b):
    return lax.dot_general(a, b, (((0,), (0,)), ((), ())), preferred_element_type=F32)


def _softmax_rows(s):
    m = jnp.max(s, axis=-1, keepdims=True)
    p = jnp.exp(s - m)
    return p / jnp.sum(p, axis=-1, keepdims=True)


def _mem_attention(qm, kt_ref, vp_ref):
    out = None
    for h in range(MEM_HEADS):
        s = _dot(qm, kt_ref[h]) * (MEM_HEAD_DIM ** -0.5)
        p = _softmax_rows(s).astype(BF16)
        o = _dot(p, vp_ref[h])
        out = o if out is None else out + o
    return out


def _rope_kernel(pos_ref, inv_ref, cos_ref, sin_ref):
    ang = pos_ref[...].astype(F32) * inv_ref[...]
    cos_ref[...] = jnp.cos(ang)
    sin_ref[...] = jnp.sin(ang)


def _rope_tables(pos_col, inv_row):
    s = pos_col.shape[0]
    return pl.pallas_call(
        _rope_kernel,
        grid=(s // ROPE_ROWS,),
        in_specs=[pl.BlockSpec((ROPE_ROWS, 1), lambda i: (i, 0)),
                  pl.BlockSpec((1, LANES), lambda i: (0, 0))],
        out_specs=[pl.BlockSpec((ROPE_ROWS, LANES), lambda i: (i, 0))] * 2,
        out_shape=[jax.ShapeDtypeStruct((s, LANES), F32)] * 2,
        compiler_params=_params("parallel"),
        name="rope_tables",
    )(pos_col, inv_row)


def _mem_kv_kernel(mem_ref, g_ref, w_ref, o_ref):
    mn = _rms(mem_ref[...], g_ref[...]).astype(BF16)
    o_ref[0] = _dot(mn, w_ref[0])


def _mem_kv(mem2d, mem_norm_row, w_mem_kv_bf):
    return pl.pallas_call(
        _mem_kv_kernel,
        grid=(DEPTH,),
        in_specs=[pl.BlockSpec((N_MEM, D_MODEL), lambda l: (0, 0)),
                  pl.BlockSpec((1, D_MODEL), lambda l: (0, 0)),
                  pl.BlockSpec((1, D_MODEL, 2 * MEM_WIDTH), lambda l: (l, 0, 0))],
        out_specs=pl.BlockSpec((1, N_MEM, 2 * MEM_WIDTH), lambda l: (l, 0, 0)),
        out_shape=jax.ShapeDtypeStruct((DEPTH, N_MEM, 2 * MEM_WIDTH), F32),
        compiler_params=_params("parallel"),
        name="mem_kv",
    )(mem2d, mem_norm_row, w_mem_kv_bf)


def _ffn_kernel(*refs, has_proj, has_final):
    refs = list(refs)
    x_ref = refs.pop(0)
    if has_proj:
        oa_ref, om_ref, wout_ref = refs[:3]
        refs = refs[3:]
    g_ref, wgu_ref, wd_ref = refs[:3]
    refs = refs[3:]
    if has_final:
        fg_ref = refs.pop(0)
    o_ref, act_ref = refs

    x = x_ref[...]
    if has_proj:
        mix_w = oa_ref.shape[1]
        x = x + _dot(oa_ref[...], wout_ref[:mix_w]) + _dot(om_ref[...], wout_ref[mix_w:])
    xn = _rms(x, g_ref[...]).astype(BF16)
    for c in range(0, D_FF, FFN_COLS):
        w = min(FFN_COLS, D_FF - c)
        gate = _dot(xn, wgu_ref[:, c:c + w])
        up = _dot(xn, wgu_ref[:, D_FF + c:D_FF + c + w])
        act_ref[:, c:c + w] = (gate * _sigmoid(gate) * up).astype(BF16)
    y = x + 0.5 * _dot(act_ref[...], wd_ref[...])
    if has_final:
        y = _rms(y, fg_ref[...])
    o_ref[...] = y


def _ffn(x, l, norms, wgu, wd, proj=None, final_row=None):
    s = x.shape[0]
    row_spec = lambda w: pl.BlockSpec((FFN_ROWS, w), lambda i: (i, 0))
    args, specs = [x], [row_spec(D_MODEL)]
    if proj is not None:
        oa, om, wout = proj
        args += [oa, om, wout]
        specs += [row_spec(oa.shape[1]), row_spec(om.shape[1]), _layer_spec(wout, l)]
    args += [norms, wgu, wd]
    specs += [_layer_spec(norms, l), _layer_spec(wgu, l), _layer_spec(wd, l)]
    if final_row is not None:
        args.append(final_row)
        specs.append(_const_spec(final_row.shape))
    return pl.pallas_call(
        functools.partial(_ffn_kernel, has_proj=proj is not None, has_final=final_row is not None),
        grid=(s // FFN_ROWS,),
        in_specs=specs,
        out_specs=row_spec(D_MODEL),
        out_shape=jax.ShapeDtypeStruct((s, D_MODEL), F32),
        scratch_shapes=[pltpu.VMEM((FFN_ROWS, D_FF), BF16)],
        compiler_params=_params("parallel"),
        name="ffn",
    )(*args)


def _a_in_kernel(x_ref, g_ref, w_ref, wt_ref, conv_ref, alog_ref, dtb_ref, kt_ref, vp_ref,
                 q_ref, k_ref, v_ref, gate_ref, beta_ref, gcum_ref, om_ref, ext_ref):
    tm = A_IN_ROWS
    tail = SUBLANES
    wqkv_ref = w_ref.at[:, :3 * A_WIDTH]
    wgate_ref = w_ref.at[:, 3 * A_WIDTH:]
    wba_ref = wt_ref.at[:, :2 * LANES]
    wqm_ref = wt_ref.at[:, 2 * LANES:]

    @pl.when(pl.program_id(0) == 0)
    def _():
        ext_ref[0:tail, :] = jnp.zeros((tail, 3 * A_WIDTH), F32)

    xn = _rms(x_ref[...], g_ref[...]).astype(BF16)
    ext_ref[tail:tail + tm, :] = _dot(xn, wqkv_ref[...])
    for b in range(3 * A_HEADS):
        sl = slice(LANES * b, LANES * (b + 1))
        acc = ext_ref[tail:tail + tm, sl] * conv_ref[CONV_K - 1:CONV_K, sl]
        for j in range(1, CONV_K):
            acc = acc + ext_ref[tail - j:tail - j + tm, sl] * conv_ref[CONV_K - 1 - j:CONV_K - j, sl]
        y = acc * _sigmoid(acc)
        if b < 2 * A_HEADS:
            y = y * lax.rsqrt(jnp.sum(y * y, axis=-1, keepdims=True) + EPS)
        if b < A_HEADS:
            q_ref[:, sl] = y * (A_HEAD_DIM ** -0.5)
        elif b < 2 * A_HEADS:
            k_ref[:, LANES * (b - A_HEADS):LANES * (b - A_HEADS + 1)] = y
        else:
            v_ref[:, LANES * (b - 2 * A_HEADS):LANES * (b - 2 * A_HEADS + 1)] = y
    ext_ref[0:tail, :] = ext_ref[tm:tm + tail, :]

    gate = _dot(xn, wgate_ref[...])
    gate_ref[...] = (gate * _sigmoid(gate)).astype(BF16)

    ba = _dot(xn, wba_ref[...])
    beta_ref[...] = _sigmoid(ba[:, :LANES])
    z = ba[:, LANES:] + dtb_ref[...]
    softplus = jnp.maximum(z, 0.0) + jnp.log(1.0 + jnp.exp(-jnp.abs(z)))
    g = -jnp.exp(alog_ref[...]) * softplus
    r = lax.broadcasted_iota(jnp.int32, (tm, tm), 0)
    c = lax.broadcasted_iota(jnp.int32, (tm, tm), 1)
    tri = jnp.where((c <= r) & ((c // CHUNK) == (r // CHUNK)), 1.0, 0.0).astype(BF16)
    g1 = g.astype(BF16)
    g2 = (g - g1.astype(F32)).astype(BF16)
    g3 = (g - g1.astype(F32) - g2.astype(F32)).astype(BF16)
    gcum_ref[...] = _dot(tri, g1) + _dot(tri, g2) + _dot(tri, g3)

    qm = _dot(xn, wqm_ref[...]).astype(BF16)
    om_ref[...] = _mem_attention(qm, kt_ref, vp_ref).astype(BF16)


def _a_in(x, l, i_a, norms, w_a, w_a_tail, conv_w, alog_rows, dtb_rows, kt, vp):
    s = x.shape[0]
    tm = A_IN_ROWS
    row_spec = lambda w: pl.BlockSpec((tm, w), lambda i: (i, 0))
    consts = [norms, w_a, w_a_tail, conv_w, alog_rows, dtb_rows, kt, vp]
    layer_of = [l, i_a, i_a, i_a, i_a, i_a, l, l]
    return pl.pallas_call(
        _a_in_kernel,
        grid=(s // tm,),
        in_specs=[row_spec(D_MODEL)] + [_layer_spec(a, j) for a, j in zip(consts, layer_of)],
        out_specs=[row_spec(A_WIDTH), row_spec(A_WIDTH), row_spec(A_WIDTH), row_spec(A_WIDTH),
                   row_spec(LANES), row_spec(LANES), row_spec(MEM_WIDTH)],
        out_shape=[jax.ShapeDtypeStruct((s, A_WIDTH), F32), jax.ShapeDtypeStruct((s, A_WIDTH), F32),
                   jax.ShapeDtypeStruct((s, A_WIDTH), F32), jax.ShapeDtypeStruct((s, A_WIDTH), BF16),
                   jax.ShapeDtypeStruct((s, LANES), F32), jax.ShapeDtypeStruct((s, LANES), F32),
                   jax.ShapeDtypeStruct((s, MEM_WIDTH), BF16)],
        scratch_shapes=[pltpu.VMEM((tm + SUBLANES, 3 * A_WIDTH), F32)],
        compiler_params=_params("arbitrary"),
        name="a_in",
    )(x, *consts)


def _gdn_kernel(q_ref, k_ref, v_ref, gate_ref, beta_ref, gcum_ref, gain_ref, o_ref, state_ref):
    @pl.when(pl.program_id(0) == 0)
    def _():
        state_ref[...] = jnp.zeros(state_ref.shape, F32)

    rows = GDN_ROWS
    gc_all = gcum_ref[...]
    gc_rows = gc_all.T
    beta_all = beta_ref[...]
    r = lax.broadcasted_iota(jnp.int32, (rows, rows), 0)
    c = lax.broadcasted_iota(jnp.int32, (rows, rows), 1)
    same = (r // CHUNK) == (c // CHUNK)
    causal = same & (c <= r)
    strict = same & (c < r)
    row_chunk = lax.broadcasted_iota(jnp.int32, (rows, A_HEAD_DIM), 0) // CHUNK

    def chunk_columns(x):
        return jnp.concatenate([jnp.where(row_chunk == ci, x, 0.0) for ci in range(GDN_CHUNKS)], axis=1)

    heads = range(A_HEADS)
    sls = [slice(A_HEAD_DIM * h, A_HEAD_DIM * (h + 1)) for h in heads]
    gcs = [gc_all[:, h:h + 1] for h in heads]
    bts = [beta_all[:, h:h + 1] for h in heads]
    g_last = [[gc[CHUNK * (ci + 1) - 1:CHUNK * (ci + 1), :] for ci in range(GDN_CHUNKS)] for gc in gcs]
    ks = [k_ref[:, sl] for sl in sls]
    kbs = [k * bt for k, bt in zip(ks, bts)]
    d1s = [_dot_nt(jnp.concatenate([kb, q_ref[:, sl]], axis=0).astype(BF16), k.astype(BF16))
           for kb, k, sl in zip(kbs, ks, sls)]
    ps, qks, xs = [], [], []
    for h in heads:
        decay = jnp.where(causal, jnp.exp(jnp.where(causal, gcs[h] - gc_rows[h:h + 1, :], 0.0)), 0.0)
        ps.append(jnp.where(strict, d1s[h][:rows] * decay, 0.0).astype(BF16))
        qks.append((d1s[h][rows:] * decay).astype(BF16))
        xs.append(jnp.concatenate([v_ref[:, sls[h]] * bts[h], kbs[h] * jnp.exp(gcs[h])], axis=1))
    sign = -1.0
    pw = 1
    while 2 * pw < CHUNK:
        ds = [_dot(ps[h], jnp.concatenate([xs[h].astype(BF16), ps[h]], axis=1)) for h in heads]
        xs = [xs[h] + sign * ds[h][:, :2 * A_HEAD_DIM] for h in heads]
        ps = [ds[h][:, 2 * A_HEAD_DIM:].astype(BF16) for h in heads]
        sign = 1.0
        pw *= 2
    x_bfs = [(xs[h] + _dot(ps[h], xs[h].astype(BF16))).astype(BF16) for h in heads]
    d2s = [_dot(qks[h], x_bfs[h]) for h in heads]
    d3s = []
    for h in heads:
        g_last_rows = jnp.concatenate([jnp.broadcast_to(g, (CHUNK, 1)) for g in g_last[h]], axis=0)
        k_dec = ks[h] * jnp.exp(g_last_rows - gcs[h])
        d3s.append(_dot_tn(chunk_columns(k_dec).astype(BF16), x_bfs[h]))
    sts = [state_ref[h] for h in heads]
    starts = [[] for _ in heads]
    for ci in range(GDN_CHUNKS):
        for h in heads:
            st_bf = sts[h].astype(BF16)
            starts[h].append(st_bf)
            blk = d3s[h][A_HEAD_DIM * ci:A_HEAD_DIM * (ci + 1)]
            sts[h] = (sts[h] * jnp.exp(g_last[h][ci]) + blk[:, :A_HEAD_DIM]
                      - _dot(blk[:, A_HEAD_DIM:].astype(BF16), st_bf))
    for h in heads:
        state_ref[h] = sts[h]
        q_eff = q_ref[:, sls[h]] * jnp.exp(gcs[h]) - d2s[h][:, A_HEAD_DIM:]
        out = d2s[h][:, :A_HEAD_DIM] + _dot(chunk_columns(q_eff).astype(BF16), jnp.concatenate(starts[h], axis=0))
        o = _rms(out, gain_ref[...]) * gate_ref[:, sls[h]].astype(F32)
        o_ref[:, sls[h]] = o.astype(BF16)


def _gdn(q, k, v, gate, beta, gcum, i_a, gain_rows):
    s = q.shape[0]
    row_spec = lambda w: pl.BlockSpec((GDN_ROWS, w), lambda i: (i, 0))
    return pl.pallas_call(
        _gdn_kernel,
        grid=(s // GDN_ROWS,),
        in_specs=[row_spec(A_WIDTH)] * 4 + [row_spec(LANES)] * 2 + [_layer_spec(gain_rows, i_a)],
        out_specs=row_spec(A_WIDTH),
        out_shape=jax.ShapeDtypeStruct((s, A_WIDTH), BF16),
        scratch_shapes=[pltpu.VMEM((A_HEADS, A_HEAD_DIM, A_HEAD_DIM), F32)],
        compiler_params=_params("arbitrary"),
        name="gdn",
    )(q, k, v, gate, beta, gcum, gain_rows)


def _kv_kernel(x_ref, g_ref, wd_ref, lg_ref, wu_ref, cos_ref, sin_ref, kt_ref, v_ref):
    xn = _rms(x_ref[...], g_ref[...]).astype(BF16)
    ckr = _dot(xn, wd_ref[...])
    cn = _rms(ckr[:, :KV_LORA], lg_ref[...]).astype(BF16)
    kr = ckr[:, KV_LORA:KV_LORA + LANES] * cos_ref[...] + ckr[:, KV_LORA + LANES:] * sin_ref[...]
    kv = _dot(cn, wu_ref[...])
    ones_col = jnp.where(lax.broadcasted_iota(jnp.int32, (KV_ROWS, V_PAD - V_HEAD), 1) == 0, 1.0, 0.0)
    for h in range(B_HEADS):
        k_full = jnp.concatenate([kv[:, QK_NOPE * h:QK_NOPE * (h + 1)], kr], axis=1)
        kt_ref[h] = k_full.T.astype(BF16)
        v_h = kv[:, B_HEADS * QK_NOPE + V_HEAD * h:B_HEADS * QK_NOPE + V_HEAD * (h + 1)]
        v_ref[h] = jnp.concatenate([v_h, ones_col], axis=1).astype(BF16)


def _kv(x, norm_row, wd, lat_row, wu, cos_t, sin_t):
    s = x.shape[0]
    tm = KV_ROWS
    consts1 = [norm_row, wd, lat_row, wu]
    return pl.pallas_call(
        _kv_kernel,
        grid=(s // tm,),
        in_specs=[pl.BlockSpec((tm, D_MODEL), lambda i: (i, 0))]
        + [_const_spec(a.shape) for a in consts1]
        + [pl.BlockSpec((tm, LANES), lambda i: (i, 0))] * 2,
        out_specs=[pl.BlockSpec((B_HEADS, QK_PAD, tm), lambda i: (0, 0, i)),
                   pl.BlockSpec((B_HEADS, tm, V_PAD), lambda i: (0, i, 0))],
        out_shape=[jax.ShapeDtypeStruct((B_HEADS, QK_PAD, s), BF16),
                   jax.ShapeDtypeStruct((B_HEADS, s, V_PAD), BF16)],
        compiler_params=_params("parallel"),
        name="mla_kv",
    )(x, *consts1, cos_t, sin_t)


def _b_in_kernel(x_ref, g_ref, win_ref, qg_ref, wuq_ref, wrot_ref, cos_ref, sin_ref, kt_ref, vp_ref,
                 q_ref, om_ref):
    scale = (QK_NOPE + QK_ROPE) ** -0.5 * LOG2E
    xn = _rms(x_ref[...], g_ref[...]).astype(BF16)
    h_in = _dot(xn, win_ref[...])
    cqn = _rms(h_in[:, :Q_LORA], qg_ref[...]).astype(BF16)
    qa = _dot(cqn, wuq_ref[...])
    qb = _dot(cqn, wrot_ref[...])
    cos_t = cos_ref[...]
    sin_t = sin_ref[...]
    for h in range(B_HEADS):
        q_ref[h, :, :QK_NOPE] = (qa[:, QK_PAD * h:QK_PAD * h + QK_NOPE] * scale).astype(BF16)
        hi = qa[:, QK_PAD * h + QK_NOPE:QK_PAD * (h + 1)] * cos_t + qb[:, LANES * h:LANES * (h + 1)] * sin_t
        q_ref[h, :, QK_NOPE:] = (hi * scale).astype(BF16)
    qm = h_in[:, Q_LORA:].astype(BF16)
    om_ref[...] = _mem_attention(qm, kt_ref, vp_ref).astype(BF16)


def _b_in(x, l, j_b, norms, win, qg_rows, wuq, wrot, cos_t, sin_t, kt, vp):
    s = x.shape[0]
    tm = B_IN_ROWS
    consts1 = [norms, win, qg_rows, wuq, wrot]
    consts2 = [kt, vp]
    return pl.pallas_call(
        _b_in_kernel,
        grid=(s // tm,),
        in_specs=[pl.BlockSpec((tm, D_MODEL), lambda i: (i, 0))]
        + [_layer_spec(a, jj) for a, jj in zip(consts1, [l, j_b, j_b, j_b, j_b])]
        + [pl.BlockSpec((tm, LANES), lambda i: (i, 0))] * 2
        + [_layer_spec(a, l) for a in consts2],
        out_specs=[pl.BlockSpec((B_HEADS, tm, QK_PAD), lambda i: (0, i, 0)),
                   pl.BlockSpec((tm, MEM_WIDTH), lambda i: (i, 0))],
        out_shape=[jax.ShapeDtypeStruct((B_HEADS, s, QK_PAD), BF16),
                   jax.ShapeDtypeStruct((s, MEM_WIDTH), BF16)],
        compiler_params=_params("parallel"),
        name="b_in",
    )(x, *consts1, cos_t, sin_t, *consts2)


def _attn_kernel(q_ref, kt_ref, v_ref, o_ref, m_ref, acc_ref):
    i = pl.program_id(1)
    q = q_ref[0]
    m_ref[...] = jnp.full(m_ref.shape, -jnp.inf, F32)
    acc_ref[...] = jnp.zeros(acc_ref.shape, F32)
    lane_tiles = ATT_K // LANES

    def run(first_block, n_full, n_diag):
        m = m_ref[...]
        acc = acc_ref[...]
        n = n_full + n_diag
        starts = [pl.multiple_of((first_block + b) * ATT_K, ATT_K) for b in range(n)]
        offs = [0] * n_full + [j * ATT_K for j in range(n_diag)]
        ss = [_dot(q[off:], kt_ref[0, :, pl.ds(st, ATT_K)]) for st, off in zip(starts, offs)]
        ps, alphas = [], []
        for b, off in enumerate(offs):
            s = ss[b]
            if b >= n_full:
                qc = lax.broadcasted_iota(jnp.int32, s.shape, 0)
                kc = lax.broadcasted_iota(jnp.int32, s.shape, 1)
                s = jnp.where((kc // CHUNK) <= (qc // CHUNK), s, -jnp.inf)
            mx = s[:, :LANES]
            for t in range(1, lane_tiles):
                mx = jnp.maximum(mx, s[:, LANES * t:LANES * (t + 1)])
            mx = jnp.broadcast_to(jnp.max(mx, axis=-1, keepdims=True), mx.shape)
            m_old = m[off:]
            m_new = jnp.maximum(m_old, mx)
            alphas.append(jnp.exp2(m_old - m_new))
            ps.append(jnp.exp2(s - jnp.concatenate([m_new] * lane_tiles, axis=1)).astype(BF16))
            m = m_new if off == 0 else jnp.concatenate([m[:off], m_new], axis=0)
        for b, off in enumerate(offs):
            pv = _dot(ps[b], v_ref[0, pl.ds(starts[b], ATT_K), :])
            upd = acc[off:] * jnp.concatenate([alphas[b]] * (V_PAD // LANES), axis=1) + pv
            acc = upd if off == 0 else jnp.concatenate([acc[:off], upd], axis=0)
        m_ref[...] = m
        acc_ref[...] = acc

    big = ATT_UNROLL * ATT_BLOCKS
    lax.fori_loop(0, i // ATT_UNROLL, lambda t, _: run(t * big, big, 0), None)

    @pl.when(i % ATT_UNROLL == 1)
    def _():
        run((i - 1) * ATT_BLOCKS, ATT_BLOCKS, ATT_BLOCKS)

    @pl.when(i % ATT_UNROLL == 0)
    def _():
        run(i * ATT_BLOCKS, 0, ATT_BLOCKS)

    acc = acc_ref[...]
    o_ref[...] = (acc[:, :V_HEAD] / acc[:, V_HEAD:V_HEAD + 1]).astype(BF16)


def _attn(q, kt, v):
    h, s, _ = q.shape
    assert ATT_Q == ATT_BLOCKS * ATT_K and ATT_K % CHUNK == 0 and ATT_UNROLL == 2
    return pl.pallas_call(
        _attn_kernel,
        grid=(h, s // ATT_Q),
        in_specs=[pl.BlockSpec((1, ATT_Q, QK_PAD), lambda hh, i: (hh, i, 0)),
                  pl.BlockSpec((1, QK_PAD, s), lambda hh, i: (hh, 0, 0), pipeline_mode=pl.Buffered(1)),
                  pl.BlockSpec((1, s, V_PAD), lambda hh, i: (hh, 0, 0), pipeline_mode=pl.Buffered(1))],
        out_specs=pl.BlockSpec((ATT_Q, V_HEAD), lambda hh, i: (i, hh)),
        out_shape=jax.ShapeDtypeStruct((s, h * V_HEAD), BF16),
        scratch_shapes=[pltpu.VMEM((ATT_Q, LANES), F32), pltpu.VMEM((ATT_Q, V_PAD), F32)],
        compiler_params=_params("parallel", "arbitrary"),
        name="mla_attn",
    )(q, kt, v)


def _rot_cols(w):
    half = w.shape[-1] // 2
    return jnp.concatenate([-w[..., half:], w[..., :half]], axis=-1)


def _pad_cols(w, width):
    return jnp.pad(w, [(0, 0)] * (w.ndim - 1) + [(0, width - w.shape[-1])])


def _mem_layout(mem_kv_all):
    k = mem_kv_all[:, :, :MEM_WIDTH]
    v = mem_kv_all[:, :, MEM_WIDTH:]
    head_of = jnp.arange(MEM_WIDTH) // MEM_HEAD_DIM
    sel = (head_of[None, :] == jnp.arange(MEM_HEADS)[:, None]).astype(F32)
    kt = (jnp.swapaxes(k, 1, 2)[:, None, :, :] * sel[None, :, :, None]).astype(BF16)
    vp = (v[:, None, :, :] * sel[None, :, None, :]).astype(BF16)
    return kt, vp


def kernel(x, mem, positions, ffn1_norm, ffn1_w_gu, ffn1_w_down, mix_norm, ffn2_norm, ffn2_w_gu,
           ffn2_w_down, w_out, mem_norm, w_mem_kv, a_w_in, a_conv, a_A_log, a_dt_bias, a_out_norm,
           b_w_in, b_q_norm, b_w_uq, kv_in_norm, w_dkv, kv_lat_norm, w_ukv, final_norm):
    b, s, d = x.shape
    assert (b, s, d) == (1, SEQ, D_MODEL)
    xs = x.reshape(s, d)
    row = lambda v: v.reshape(1, -1).astype(F32)

    inv = ROPE_THETA ** (-jnp.arange(0, QK_ROPE, 2, dtype=F32) / QK_ROPE)
    inv_row = _pad_cols(jnp.concatenate([inv, inv])[None, :], LANES)
    cos_t, sin_t = _rope_tables(positions.reshape(s, 1), inv_row)

    mem_kv_all = _mem_kv(mem.reshape(N_MEM, d), row(mem_norm), w_mem_kv.astype(BF16))
    kt, vp = _mem_layout(mem_kv_all)

    rows = lambda v: v.astype(F32)[:, None, :]
    ffn1_n, ffn2_n, mix_n = rows(ffn1_norm), rows(ffn2_norm), rows(mix_norm)
    ffn1_gu, ffn1_d = ffn1_w_gu.astype(BF16), ffn1_w_down.astype(BF16)
    ffn2_gu, ffn2_d = ffn2_w_gu.astype(BF16), ffn2_w_down.astype(BF16)
    w_out_bf = w_out.astype(BF16)

    b_off = 4 * A_WIDTH
    w_a = a_w_in[:, :, :b_off].astype(BF16)
    w_a_tail = jnp.concatenate(
        [_pad_cols(a_w_in[:, :, b_off:b_off + A_HEADS], LANES),
         _pad_cols(a_w_in[:, :, b_off + A_HEADS:b_off + 2 * A_HEADS], LANES),
         a_w_in[:, :, b_off + 2 * A_HEADS:]], axis=2).astype(BF16)
    conv_w = a_conv.astype(F32)
    alog_rows = _pad_cols(rows(a_A_log), LANES)
    dtb_rows = _pad_cols(rows(a_dt_bias), LANES)
    gain_rows = rows(a_out_norm)

    for i in range(N_A_LAYERS):
        l = i
        xs = _ffn(xs, l, ffn1_n, ffn1_gu, ffn1_d)
        q, k, v, gate, beta, gcum, o_m = _a_in(xs, l, i, mix_n, w_a, w_a_tail, conv_w, alog_rows, dtb_rows,
                                               kt, vp)
        o_a = _gdn(q, k, v, gate, beta, gcum, i, gain_rows)
        xs = _ffn(xs, l, ffn2_n, ffn2_gu, ffn2_d, proj=(o_a, o_m, w_out_bf))

    w_c = w_dkv[:, :KV_LORA]
    w_r = w_dkv[:, KV_LORA:]
    wd = jnp.concatenate([w_c, _pad_cols(w_r, LANES), _pad_cols(_rot_cols(w_r), LANES)], axis=1).astype(BF16)
    w_ukv3 = w_ukv.reshape(KV_LORA, B_HEADS, QK_NOPE + V_HEAD)
    wu = jnp.concatenate([w_ukv3[:, :, :QK_NOPE].reshape(KV_LORA, B_HEADS * QK_NOPE),
                          w_ukv3[:, :, QK_NOPE:].reshape(KV_LORA, B_HEADS * V_HEAD)], axis=1).astype(BF16)
    k_all, v_all = _kv(xs, row(kv_in_norm), wd, row(kv_lat_norm), wu, cos_t, sin_t)

    w_uq4 = b_w_uq.reshape(N_B_LAYERS, Q_LORA, B_HEADS, QK_NOPE + QK_ROPE)
    wuq = _pad_cols(w_uq4, QK_PAD).reshape(N_B_LAYERS, Q_LORA, B_HEADS * QK_PAD).astype(BF16)
    wrot = _pad_cols(_rot_cols(w_uq4[..., QK_NOPE:]), LANES).reshape(N_B_LAYERS, Q_LORA, B_HEADS * LANES).astype(BF16)
    b_win = b_w_in.astype(BF16)
    qg_rows = rows(b_q_norm)

    for j in range(N_B_LAYERS):
        l = N_A_LAYERS + j
        xs = _ffn(xs, l, ffn1_n, ffn1_gu, ffn1_d)
        q_all, o_m = _b_in(xs, l, j, mix_n, b_win, qg_rows, wuq, wrot, cos_t, sin_t, kt, vp)
        o_b = _attn(q_all, k_all, v_all)
        last = j == N_B_LAYERS - 1
        xs = _ffn(xs, l, ffn2_n, ffn2_gu, ffn2_d, proj=(o_b, o_m, w_out_bf),
                  final_row=row(final_norm) if last else None)

    return xs.reshape(b, s, d)
```

```python
import functools

import jax
import jax.numpy as jnp
from jax import lax
from jax.experimental import pallas as pl
from jax.experimental.pallas import tpu as pltpu

F32 = jnp.float32
BF16 = jnp.bfloat16

D_MODEL = 1024
SEQ = 16384
DEPTH = 4
CHUNK = 64
EPS = 1e-6
N_A_LAYERS = DEPTH // 2
N_B_LAYERS = DEPTH - N_A_LAYERS

A_HEADS = 6
A_HEAD_DIM = 128
A_WIDTH = A_HEADS * A_HEAD_DIM
CONV_K = 4

B_HEADS = 6
QK_NOPE = 128
QK_ROPE = 64
V_HEAD = 128
Q_LORA = 256
KV_LORA = 256
B_WIDTH = B_HEADS * V_HEAD
ROPE_THETA = 10000.0

N_MEM = 256
MEM_HEADS = 4
MEM_HEAD_DIM = 64
MEM_WIDTH = MEM_HEADS * MEM_HEAD_DIM

D_FF = 2816

LANES = 128
SUBLANES = 8
QK_PAD = 256
VMEM_LIMIT = 56 * 1024 * 1024

FFN_ROWS = 1024
FFN_COLS = 512
A_IN_ROWS = 256
GDN_CHUNKS = 4
GDN_ROWS = GDN_CHUNKS * CHUNK
B_IN_ROWS = 512
KV_ROWS = 512
ATT_Q = 1024
ATT_K = 512
ATT_BLOCKS = ATT_Q // ATT_K
ATT_UNROLL = 2
V_PAD = 256
LOG2E = 1.4426950408889634
ROPE_ROWS = 2048


def _params(*sem):
    return pltpu.CompilerParams(dimension_semantics=sem, vmem_limit_bytes=VMEM_LIMIT)


def _const_spec(shape):
    n = len(shape)
    return pl.BlockSpec(shape, lambda *_: (0,) * n, pipeline_mode=pl.Buffered(1))


def _layer_spec(stacked, l):
    n = stacked.ndim - 1
    return pl.BlockSpec((None,) + tuple(stacked.shape[1:]), lambda *_: (l,) + (0,) * n,
                        pipeline_mode=pl.Buffered(1))


def _rms(x, g):
    return x * lax.rsqrt(jnp.mean(x * x, axis=-1, keepdims=True) + EPS) * g


def _sigmoid(x):
    return 1.0 / (1.0 + jnp.exp(-x))


def _dot(a, b):
    return jnp.dot(a, b, preferred_element_type=F32)


def _dot_nt(a, b):
    return lax.dot_general(a, b, (((1,), (1,)), ((), ())), preferred_element_type=F32)


def _dot_tn(a, b):
    return lax.dot_general(a, b, (((0,), (0,)), ((), ())), preferred_element_type=F32)


def _softmax_rows(s):
    m = jnp.max(s, axis=-1, keepdims=True)
    p = jnp.exp(s - m)
    return p / jnp.sum(p, axis=-1, keepdims=True)


def _mem_attention(qm, kt_ref, vp_ref):
    out = None
    for h in range(MEM_HEADS):
        s = _dot(qm, kt_ref[h]) * (MEM_HEAD_DIM ** -0.5)
        p = _softmax_rows(s).astype(BF16)
        o = _dot(p, vp_ref[h])
        out = o if out is None else out + o
    return out


def _rope_kernel(pos_ref, inv_ref, cos_ref, sin_ref):
    ang = pos_ref[...].astype(F32) * inv_ref[...]
    cos_ref[...] = jnp.cos(ang)
    sin_ref[...] = jnp.sin(ang)


def _rope_tables(pos_col, inv_row):
    s = pos_col.shape[0]
    return pl.pallas_call(
        _rope_kernel,
        grid=(s // ROPE_ROWS,),
        in_specs=[pl.BlockSpec((ROPE_ROWS, 1), lambda i: (i, 0)),
                  pl.BlockSpec((1, LANES), lambda i: (0, 0))],
        out_specs=[pl.BlockSpec((ROPE_ROWS, LANES), lambda i: (i, 0))] * 2,
        out_shape=[jax.ShapeDtypeStruct((s, LANES), F32)] * 2,
        compiler_params=_params("parallel"),
        name="rope_tables",
    )(pos_col, inv_row)


def _mem_kv_kernel(mem_ref, g_ref, w_ref, o_ref):
    mn = _rms(mem_ref[...], g_ref[...]).astype(BF16)
    o_ref[0] = _dot(mn, w_ref[0])


def _mem_kv(mem2d, mem_norm_row, w_mem_kv_bf):
    return pl.pallas_call(
        _mem_kv_kernel,
        grid=(DEPTH,),
        in_specs=[pl.BlockSpec((N_MEM, D_MODEL), lambda l: (0, 0)),
                  pl.BlockSpec((1, D_MODEL), lambda l: (0, 0)),
                  pl.BlockSpec((1, D_MODEL, 2 * MEM_WIDTH), lambda l: (l, 0, 0))],
        out_specs=pl.BlockSpec((1, N_MEM, 2 * MEM_WIDTH), lambda l: (l, 0, 0)),
        out_shape=jax.ShapeDtypeStruct((DEPTH, N_MEM, 2 * MEM_WIDTH), F32),
        compiler_params=_params("parallel"),
        name="mem_kv",
    )(mem2d, mem_norm_row, w_mem_kv_bf)


def _ffn_kernel(*refs, has_proj, has_final):
    refs = list(refs)
    x_ref = refs.pop(0)
    if has_proj:
        oa_ref, om_ref, wout_ref = refs[:3]
        refs = refs[3:]
    g_ref, wgu_ref, wd_ref = refs[:3]
    refs = refs[3:]
    if has_final:
        fg_ref = refs.pop(0)
    o_ref, act_ref = refs

    x = x_ref[...]
    if has_proj:
        mix_w = oa_ref.shape[1]
        x = x + _dot(oa_ref[...], wout_ref[:mix_w]) + _dot(om_ref[...], wout_ref[mix_w:])
    xn = _rms(x, g_ref[...]).astype(BF16)
    for c in range(0, D_FF, FFN_COLS):
        w = min(FFN_COLS, D_FF - c)
        gate = _dot(xn, wgu_ref[:, c:c + w])
        up = _dot(xn, wgu_ref[:, D_FF + c:D_FF + c + w])
        act_ref[:, c:c + w] = (gate * _sigmoid(gate) * up).astype(BF16)
    y = x + 0.5 * _dot(act_ref[...], wd_ref[...])
    if has_final:
        y = _rms(y, fg_ref[...])
    o_ref[...] = y


def _ffn(x, l, norms, wgu, wd, proj=None, final_row=None):
    s = x.shape[0]
    row_spec = lambda w: pl.BlockSpec((FFN_ROWS, w), lambda i: (i, 0))
    args, specs = [x], [row_spec(D_MODEL)]
    if proj is not None:
        oa, om, wout = proj
        args += [oa, om, wout]
        specs += [row_spec(oa.shape[1]), row_spec(om.shape[1]), _layer_spec(wout, l)]
    args += [norms, wgu, wd]
    specs += [_layer_spec(norms, l), _layer_spec(wgu, l), _layer_spec(wd, l)]
    if final_row is not None:
        args.append(final_row)
        specs.append(_const_spec(final_row.shape))
    return pl.pallas_call(
        functools.partial(_ffn_kernel, has_proj=proj is not None, has_final=final_row is not None),
        grid=(s // FFN_ROWS,),
        in_specs=specs,
        out_specs=row_spec(D_MODEL),
        out_shape=jax.ShapeDtypeStruct((s, D_MODEL), F32),
        scratch_shapes=[pltpu.VMEM((FFN_ROWS, D_FF), BF16)],
        compiler_params=_params("parallel"),
        name="ffn",
    )(*args)


def _a_in_kernel(x_ref, g_ref, w_ref, conv_ref, alog_ref, dtb_ref, kt_ref, vp_ref,
                 q_ref, k_ref, v_ref, gate_ref, beta_ref, gcum_ref, om_ref, ext_ref):
    tm = A_IN_ROWS
    tail = SUBLANES
    wqkv_ref = w_ref.at[:, :3 * A_WIDTH]
    wgate_ref = w_ref.at[:, 3 * A_WIDTH:4 * A_WIDTH]
    wba_ref = w_ref.at[:, 4 * A_WIDTH:4 * A_WIDTH + 2 * LANES]
    wqm_ref = w_ref.at[:, 4 * A_WIDTH + 2 * LANES:]

    @pl.when(pl.program_id(0) == 0)
    def _():
        ext_ref[0:tail, :] = jnp.zeros((tail, 3 * A_WIDTH), F32)

    xn = _rms(x_ref[...], g_ref[...]).astype(BF16)
    ext_ref[tail:tail + tm, :] = _dot(xn, wqkv_ref[...])
    for b in range(3 * A_HEADS):
        sl = slice(LANES * b, LANES * (b + 1))
        acc = ext_ref[tail:tail + tm, sl] * conv_ref[CONV_K - 1:CONV_K, sl]
        for j in range(1, CONV_K):
            acc = acc + ext_ref[tail - j:tail - j + tm, sl] * conv_ref[CONV_K - 1 - j:CONV_K - j, sl]
        y = acc * _sigmoid(acc)
        if b < 2 * A_HEADS:
            y = y * lax.rsqrt(jnp.sum(y * y, axis=-1, keepdims=True) + EPS)
        if b < A_HEADS:
            q_ref[:, sl] = y * (A_HEAD_DIM ** -0.5)
        elif b < 2 * A_HEADS:
            k_ref[:, LANES * (b - A_HEADS):LANES * (b - A_HEADS + 1)] = y
        else:
            v_ref[:, LANES * (b - 2 * A_HEADS):LANES * (b - 2 * A_HEADS + 1)] = y
    ext_ref[0:tail, :] = ext_ref[tm:tm + tail, :]

    gate = _dot(xn, wgate_ref[...])
    gate_ref[...] = (gate * _sigmoid(gate)).astype(BF16)

    ba = _dot(xn, wba_ref[...])
    beta_ref[...] = _sigmoid(ba[:, :LANES])
    z = ba[:, LANES:] + dtb_ref[...]
    softplus = jnp.maximum(z, 0.0) + jnp.log(1.0 + jnp.exp(-jnp.abs(z)))
    g = -jnp.exp(alog_ref[...]) * softplus
    r = lax.broadcasted_iota(jnp.int32, (tm, tm), 0)
    c = lax.broadcasted_iota(jnp.int32, (tm, tm), 1)
    tri = jnp.where((c <= r) & ((c // CHUNK) == (r // CHUNK)), 1.0, 0.0).astype(BF16)
    g1 = g.astype(BF16)
    g2 = (g - g1.astype(F32)).astype(BF16)
    g3 = (g - g1.astype(F32) - g2.astype(F32)).astype(BF16)
    gcum_ref[...] = _dot(tri, g1) + _dot(tri, g2) + _dot(tri, g3)

    qm = _dot(xn, wqm_ref[...]).astype(BF16)
    om_ref[...] = _mem_attention(qm, kt_ref, vp_ref).astype(BF16)


def _a_in(x, l, i_a, norms, w_a, conv_w, alog_rows, dtb_rows, kt, vp):
    s = x.shape[0]
    tm = A_IN_ROWS
    row_spec = lambda w: pl.BlockSpec((tm, w), lambda i: (i, 0))
    consts = [norms, w_a, conv_w, alog_rows, dtb_rows, kt, vp]
    layer_of = [l, i_a, i_a, i_a, i_a, l, l]
    return pl.pallas_call(
        _a_in_kernel,
        grid=(s // tm,),
        in_specs=[row_spec(D_MODEL)] + [_layer_spec(a, j) for a, j in zip(consts, layer_of)],
        out_specs=[row_spec(A_WIDTH), row_spec(A_WIDTH), row_spec(A_WIDTH), row_spec(A_WIDTH),
                   row_spec(LANES), row_spec(LANES), row_spec(MEM_WIDTH)],
        out_shape=[jax.ShapeDtypeStruct((s, A_WIDTH), F32), jax.ShapeDtypeStruct((s, A_WIDTH), F32),
                   jax.ShapeDtypeStruct((s, A_WIDTH), F32), jax.ShapeDtypeStruct((s, A_WIDTH), BF16),
                   jax.ShapeDtypeStruct((s, LANES), F32), jax.ShapeDtypeStruct((s, LANES), F32),
                   jax.ShapeDtypeStruct((s, MEM_WIDTH), BF16)],
        scratch_shapes=[pltpu.VMEM((tm + SUBLANES, 3 * A_WIDTH), F32)],
        compiler_params=_params("arbitrary"),
        name="a_in",
    )(x, *consts)


def _gdn_kernel(q_ref, k_ref, v_ref, gate_ref, beta_ref, gcum_ref, gain_ref, o_ref, state_ref):
    @pl.when(pl.program_id(0) == 0)
    def _():
        state_ref[...] = jnp.zeros(state_ref.shape, F32)

    rows = GDN_ROWS
    gc_all = gcum_ref[...]
    gc_rows = gc_all.T
    beta_all = beta_ref[...]
    r = lax.broadcasted_iota(jnp.int32, (rows, rows), 0)
    c = lax.broadcasted_iota(jnp.int32, (rows, rows), 1)
    same = (r // CHUNK) == (c // CHUNK)
    causal = same & (c <= r)
    strict = same & (c < r)
    row_chunk = lax.broadcasted_iota(jnp.int32, (rows, A_HEAD_DIM), 0) // CHUNK

    def chunk_columns(x):
        return jnp.concatenate([jnp.where(row_chunk == ci, x, 0.0) for ci in range(GDN_CHUNKS)], axis=1)

    heads = range(A_HEADS)
    sls = [slice(A_HEAD_DIM * h, A_HEAD_DIM * (h + 1)) for h in heads]
    gcs = [gc_all[:, h:h + 1] for h in heads]
    bts = [beta_all[:, h:h + 1] for h in heads]
    g_last = [[gc[CHUNK * (ci + 1) - 1:CHUNK * (ci + 1), :] for ci in range(GDN_CHUNKS)] for gc in gcs]
    ks = [k_ref[:, sl] for sl in sls]
    kbs = [k * bt for k, bt in zip(ks, bts)]
    d1s = [_dot_nt(jnp.concatenate([kb, q_ref[:, sl]], axis=0).astype(BF16), k.astype(BF16))
           for kb, k, sl in zip(kbs, ks, sls)]
    ps, qks, xs = [], [], []
    for h in heads:
        decay = jnp.where(causal, jnp.exp(jnp.where(causal, gcs[h] - gc_rows[h:h + 1, :], 0.0)), 0.0)
        ps.append(jnp.where(strict, d1s[h][:rows] * decay, 0.0).astype(BF16))
        qks.append((d1s[h][rows:] * decay).astype(BF16))
        xs.append(jnp.concatenate([v_ref[:, sls[h]] * bts[h], kbs[h] * jnp.exp(gcs[h])], axis=1))
    sign = -1.0
    pw = 1
    while 2 * pw < CHUNK:
        ds = [_dot(ps[h], jnp.concatenate([xs[h].astype(BF16), ps[h]], axis=1)) for h in heads]
        xs = [xs[h] + sign * ds[h][:, :2 * A_HEAD_DIM] for h in heads]
        ps = [ds[h][:, 2 * A_HEAD_DIM:].astype(BF16) for h in heads]
        sign = 1.0
        pw *= 2
    x_bfs = [(xs[h] + _dot(ps[h], xs[h].astype(BF16))).astype(BF16) for h in heads]
    d2s = [_dot(qks[h], x_bfs[h]) for h in heads]
    d3s = []
    for h in heads:
        g_last_rows = jnp.concatenate([jnp.broadcast_to(g, (CHUNK, 1)) for g in g_last[h]], axis=0)
        k_dec = ks[h] * jnp.exp(g_last_rows - gcs[h])
        d3s.append(_dot_tn(chunk_columns(k_dec).astype(BF16), x_bfs[h]))
    sts = [state_ref[h] for h in heads]
    starts = [[] for _ in heads]
    for ci in range(GDN_CHUNKS):
        for h in heads:
            st_bf = sts[h].astype(BF16)
            starts[h].append(st_bf)
            blk = d3s[h][A_HEAD_DIM * ci:A_HEAD_DIM * (ci + 1)]
            sts[h] = (sts[h] * jnp.exp(g_last[h][ci]) + blk[:, :A_HEAD_DIM]
                      - _dot(blk[:, A_HEAD_DIM:].astype(BF16), st_bf))
    for h in heads:
        state_ref[h] = sts[h]
        q_eff = q_ref[:, sls[h]] * jnp.exp(gcs[h]) - d2s[h][:, A_HEAD_DIM:]
        out = d2s[h][:, :A_HEAD_DIM] + _dot(chunk_columns(q_eff).astype(BF16), jnp.concatenate(starts[h], axis=0))
        o = _rms(out, gain_ref[...]) * gate_ref[:, sls[h]].astype(F32)
        o_ref[:, sls[h]] = o.astype(BF16)


def _gdn(q, k, v, gate, beta, gcum, i_a, gain_rows):
    s = q.shape[0]
    row_spec = lambda w: pl.BlockSpec((GDN_ROWS, w), lambda i: (i, 0))
    return pl.pallas_call(
        _gdn_kernel,
        grid=(s // GDN_ROWS,),
        in_specs=[row_spec(A_WIDTH)] * 4 + [row_spec(LANES)] * 2 + [_layer_spec(gain_rows, i_a)],
        out_specs=row_spec(A_WIDTH),
        out_shape=jax.ShapeDtypeStruct((s, A_WIDTH), BF16),
        scratch_shapes=[pltpu.VMEM((A_HEADS, A_HEAD_DIM, A_HEAD_DIM), F32)],
        compiler_params=_params("arbitrary"),
        name="gdn",
    )(q, k, v, gate, beta, gcum, gain_rows)


def _kv_kernel(x_ref, g_ref, wd_ref, lg_ref, wu_ref, cos_ref, sin_ref, kt_ref, v_ref):
    xn = _rms(x_ref[...], g_ref[...]).astype(BF16)
    ckr = _dot(xn, wd_ref[...])
    cn = _rms(ckr[:, :KV_LORA], lg_ref[...]).astype(BF16)
    kr = ckr[:, KV_LORA:KV_LORA + LANES] * cos_ref[...] + ckr[:, KV_LORA + LANES:] * sin_ref[...]
    kv = _dot(cn, wu_ref[...])
    ones_col = jnp.where(lax.broadcasted_iota(jnp.int32, (KV_ROWS, V_PAD - V_HEAD), 1) == 0, 1.0, 0.0)
    for h in range(B_HEADS):
        k_full = jnp.concatenate([kv[:, QK_NOPE * h:QK_NOPE * (h + 1)], kr], axis=1)
        kt_ref[h] = k_full.T.astype(BF16)
        v_h = kv[:, B_HEADS * QK_NOPE + V_HEAD * h:B_HEADS * QK_NOPE + V_HEAD * (h + 1)]
        v_ref[h] = jnp.concatenate([v_h, ones_col], axis=1).astype(BF16)


def _kv(x, norm_row, wd, lat_row, wu, cos_t, sin_t):
    s = x.shape[0]
    tm = KV_ROWS
    consts1 = [norm_row, wd, lat_row, wu]
    return pl.pallas_call(
        _kv_kernel,
        grid=(s // tm,),
        in_specs=[pl.BlockSpec((tm, D_MODEL), lambda i: (i, 0))]
        + [_const_spec(a.shape) for a in consts1]
        + [pl.BlockSpec((tm, LANES), lambda i: (i, 0))] * 2,
        out_specs=[pl.BlockSpec((B_HEADS, QK_PAD, tm), lambda i: (0, 0, i)),
                   pl.BlockSpec((B_HEADS, tm, V_PAD), lambda i: (0, i, 0))],
        out_shape=[jax.ShapeDtypeStruct((B_HEADS, QK_PAD, s), BF16),
                   jax.ShapeDtypeStruct((B_HEADS, s, V_PAD), BF16)],
        compiler_params=_params("parallel"),
        name="mla_kv",
    )(x, *consts1, cos_t, sin_t)


def _b_in_kernel(x_ref, g_ref, win_ref, qg_ref, wuq_ref, wrot_ref, cos_ref, sin_ref, kt_ref, vp_ref,
                 q_ref, om_ref):
    scale = (QK_NOPE + QK_ROPE) ** -0.5 * LOG2E
    xn = _rms(x_ref[...], g_ref[...]).astype(BF16)
    h_in = _dot(xn, win_ref[...])
    cqn = _rms(h_in[:, :Q_LORA], qg_ref[...]).astype(BF16)
    qa = _dot(cqn, wuq_ref[...])
    qb = _dot(cqn, wrot_ref[...])
    cos_t = cos_ref[...]
    sin_t = sin_ref[...]
    for h in range(B_HEADS):
        q_ref[h, :, :QK_NOPE] = (qa[:, QK_PAD * h:QK_PAD * h + QK_NOPE] * scale).astype(BF16)
        hi = qa[:, QK_PAD * h + QK_NOPE:QK_PAD * (h + 1)] * cos_t + qb[:, LANES * h:LANES * (h + 1)] * sin_t
        q_ref[h, :, QK_NOPE:] = (hi * scale).astype(BF16)
    qm = h_in[:, Q_LORA:].astype(BF16)
    om_ref[...] = _mem_attention(qm, kt_ref, vp_ref).astype(BF16)


def _b_in(x, l, j_b, norms, win, qg_rows, wuq, wrot, cos_t, sin_t, kt, vp):
    s = x.shape[0]
    tm = B_IN_ROWS
    consts1 = [norms, win, qg_rows, wuq, wrot]
    consts2 = [kt, vp]
    return pl.pallas_call(
        _b_in_kernel,
        grid=(s // tm,),
        in_specs=[pl.BlockSpec((tm, D_MODEL), lambda i: (i, 0))]
        + [_layer_spec(a, jj) for a, jj in zip(consts1, [l, j_b, j_b, j_b, j_b])]
        + [pl.BlockSpec((tm, LANES), lambda i: (i, 0))] * 2
        + [_layer_spec(a, l) for a in consts2],
        out_specs=[pl.BlockSpec((B_HEADS, tm, QK_PAD), lambda i: (0, i, 0)),
                   pl.BlockSpec((tm, MEM_WIDTH), lambda i: (i, 0))],
        out_shape=[jax.ShapeDtypeStruct((B_HEADS, s, QK_PAD), BF16),
                   jax.ShapeDtypeStruct((s, MEM_WIDTH), BF16)],
        compiler_params=_params("parallel"),
        name="b_in",
    )(x, *consts1, cos_t, sin_t, *consts2)


def _attn_kernel(q_ref, kt_ref, v_ref, o_ref, m_ref, acc_ref):
    i = pl.program_id(1)
    q = q_ref[0]
    m_ref[...] = jnp.full(m_ref.shape, -jnp.inf, F32)
    acc_ref[...] = jnp.zeros(acc_ref.shape, F32)
    lane_tiles = ATT_K // LANES

    def run(first_block, n_full, n_diag):
        m = m_ref[...]
        acc = acc_ref[...]
        n = n_full + n_diag
        starts = [pl.multiple_of((first_block + b) * ATT_K, ATT_K) for b in range(n)]
        offs = [0] * n_full + [j * ATT_K for j in range(n_diag)]
        ss = [_dot(q[off:], kt_ref[0, :, pl.ds(st, ATT_K)]) for st, off in zip(starts, offs)]
        ps, alphas = [], []
        for b, off in enumerate(offs):
            s = ss[b]
            if b >= n_full:
                qc = lax.broadcasted_iota(jnp.int32, s.shape, 0)
                kc = lax.broadcasted_iota(jnp.int32, s.shape, 1)
                s = jnp.where((kc // CHUNK) <= (qc // CHUNK), s, -jnp.inf)
            mx = s[:, :LANES]
            for t in range(1, lane_tiles):
                mx = jnp.maximum(mx, s[:, LANES * t:LANES * (t + 1)])
            mx = jnp.broadcast_to(jnp.max(mx, axis=-1, keepdims=True), mx.shape)
            m_old = m[off:]
            m_new = jnp.maximum(m_old, mx)
            alphas.append(jnp.exp2(m_old - m_new))
            ps.append(jnp.exp2(s - jnp.concatenate([m_new] * lane_tiles, axis=1)).astype(BF16))
            m = m_new if off == 0 else jnp.concatenate([m[:off], m_new], axis=0)
        for b, off in enumerate(offs):
            pv = _dot(ps[b], v_ref[0, pl.ds(starts[b], ATT_K), :])
            upd = acc[off:] * jnp.concatenate([alphas[b]] * (V_PAD // LANES), axis=1) + pv
            acc = upd if off == 0 else jnp.concatenate([acc[:off], upd], axis=0)
        m_ref[...] = m
        acc_ref[...] = acc

    big = ATT_UNROLL * ATT_BLOCKS
    lax.fori_loop(0, i // ATT_UNROLL, lambda t, _: run(t * big, big, 0), None)

    @pl.when(i % ATT_UNROLL == 1)
    def _():
        run((i - 1) * ATT_BLOCKS, ATT_BLOCKS, ATT_BLOCKS)

    @pl.when(i % ATT_UNROLL == 0)
    def _():
        run(i * ATT_BLOCKS, 0, ATT_BLOCKS)

    acc = acc_ref[...]
    o_ref[...] = (acc[:, :V_HEAD] / acc[:, V_HEAD:V_HEAD + 1]).astype(BF16)


def _attn(q, kt, v):
    h, s, _ = q.shape
    assert ATT_Q == ATT_BLOCKS * ATT_K and ATT_K % CHUNK == 0 and ATT_UNROLL == 2
    return pl.pallas_call(
        _attn_kernel,
        grid=(h, s // ATT_Q),
        in_specs=[pl.BlockSpec((1, ATT_Q, QK_PAD), lambda hh, i: (hh, i, 0)),
                  pl.BlockSpec((1, QK_PAD, s), lambda hh, i: (hh, 0, 0)),
                  pl.BlockSpec((1, s, V_PAD), lambda hh, i: (hh, 0, 0))],
        out_specs=pl.BlockSpec((ATT_Q, V_HEAD), lambda hh, i: (i, hh)),
        out_shape=jax.ShapeDtypeStruct((s, h * V_HEAD), BF16),
        scratch_shapes=[pltpu.VMEM((ATT_Q, LANES), F32), pltpu.VMEM((ATT_Q, V_PAD), F32)],
        compiler_params=_params("parallel", "arbitrary"),
        name="mla_attn",
    )(q, kt, v)


def _rot_cols(w):
    half = w.shape[-1] // 2
    return jnp.concatenate([-w[..., half:], w[..., :half]], axis=-1)


def _pad_cols(w, width):
    return jnp.pad(w, [(0, 0)] * (w.ndim - 1) + [(0, width - w.shape[-1])])


def _mem_layout(mem_kv_all):
    k = mem_kv_all[:, :, :MEM_WIDTH]
    v = mem_kv_all[:, :, MEM_WIDTH:]
    head_of = jnp.arange(MEM_WIDTH) // MEM_HEAD_DIM
    sel = (head_of[None, :] == jnp.arange(MEM_HEADS)[:, None]).astype(F32)
    kt = (jnp.swapaxes(k, 1, 2)[:, None, :, :] * sel[None, :, :, None]).astype(BF16)
    vp = (v[:, None, :, :] * sel[None, :, None, :]).astype(BF16)
    return kt, vp


def kernel(x, mem, positions, ffn1_norm, ffn1_w_gu, ffn1_w_down, mix_norm, ffn2_norm, ffn2_w_gu,
           ffn2_w_down, w_out, mem_norm, w_mem_kv, a_w_in, a_conv, a_A_log, a_dt_bias, a_out_norm,
           b_w_in, b_q_norm, b_w_uq, kv_in_norm, w_dkv, kv_lat_norm, w_ukv, final_norm):
    b, s, d = x.shape
    assert (b, s, d) == (1, SEQ, D_MODEL)
    xs = x.reshape(s, d)
    row = lambda v: v.reshape(1, -1).astype(F32)

    inv = ROPE_THETA ** (-jnp.arange(0, QK_ROPE, 2, dtype=F32) / QK_ROPE)
    inv_row = _pad_cols(jnp.concatenate([inv, inv])[None, :], LANES)
    cos_t, sin_t = _rope_tables(positions.reshape(s, 1), inv_row)

    mem_kv_all = _mem_kv(mem.reshape(N_MEM, d), row(mem_norm), w_mem_kv.astype(BF16))
    kt, vp = _mem_layout(mem_kv_all)

    rows = lambda v: v.astype(F32)[:, None, :]
    ffn1_n, ffn2_n, mix_n = rows(ffn1_norm), rows(ffn2_norm), rows(mix_norm)
    ffn1_gu, ffn1_d = ffn1_w_gu.astype(BF16), ffn1_w_down.astype(BF16)
    ffn2_gu, ffn2_d = ffn2_w_gu.astype(BF16), ffn2_w_down.astype(BF16)
    w_out_bf = w_out.astype(BF16)

    b_off = 4 * A_WIDTH
    w_a = jnp.concatenate(
        [a_w_in[:, :, :b_off], _pad_cols(a_w_in[:, :, b_off:b_off + A_HEADS], LANES),
         _pad_cols(a_w_in[:, :, b_off + A_HEADS:b_off + 2 * A_HEADS], LANES),
         a_w_in[:, :, b_off + 2 * A_HEADS:]], axis=2).astype(BF16)
    conv_w = a_conv.astype(F32)
    alog_rows = _pad_cols(rows(a_A_log), LANES)
    dtb_rows = _pad_cols(rows(a_dt_bias), LANES)
    gain_rows = rows(a_out_norm)

    for i in range(N_A_LAYERS):
        l = i
        xs = _ffn(xs, l, ffn1_n, ffn1_gu, ffn1_d)
        q, k, v, gate, beta, gcum, o_m = _a_in(xs, l, i, mix_n, w_a, conv_w, alog_rows, dtb_rows, kt, vp)
        o_a = _gdn(q, k, v, gate, beta, gcum, i, gain_rows)
        xs = _ffn(xs, l, ffn2_n, ffn2_gu, ffn2_d, proj=(o_a, o_m, w_out_bf))

    w_c = w_dkv[:, :KV_LORA]
    w_r = w_dkv[:, KV_LORA:]
    wd = jnp.concatenate([w_c, _pad_cols(w_r, LANES), _pad_cols(_rot_cols(w_r), LANES)], axis=1).astype(BF16)
    w_ukv3 = w_ukv.reshape(KV_LORA, B_HEADS, QK_NOPE + V_HEAD)
    wu = jnp.concatenate([w_ukv3[:, :, :QK_NOPE].reshape(KV_LORA, B_HEADS * QK_NOPE),
                          w_ukv3[:, :, QK_NOPE:].reshape(KV_LORA, B_HEADS * V_HEAD)], axis=1).astype(BF16)
    k_all, v_all = _kv(xs, row(kv_in_norm), wd, row(kv_lat_norm), wu, cos_t, sin_t)

    w_uq4 = b_w_uq.reshape(N_B_LAYERS, Q_LORA, B_HEADS, QK_NOPE + QK_ROPE)
    wuq = _pad_cols(w_uq4, QK_PAD).reshape(N_B_LAYERS, Q_LORA, B_HEADS * QK_PAD).astype(BF16)
    wrot = _pad_cols(_rot_cols(w_uq4[..., QK_NOPE:]), LANES).reshape(N_B_LAYERS, Q_LORA, B_HEADS * LANES).astype(BF16)
    b_win = b_w_in.astype(BF16)
    qg_rows = rows(b_q_norm)

    for j in range(N_B_LAYERS):
        l = N_A_LAYERS + j
        xs = _ffn(xs, l, ffn1_n, ffn1_gu, ffn1_d)
        q_all, o_m = _b_in(xs, l, j, mix_n, b_win, qg_rows, wuq, wrot, cos_t, sin_t, kt, vp)
        o_b = _attn(q_all, k_all, v_all)
        last = j == N_B_LAYERS - 1
        xs = _ffn(xs, l, ffn2_n, ffn2_gu, ffn2_d, proj=(o_b, o_m, w_out_bf),
                  final_row=row(final_norm) if last else None)

    return xs.reshape(b, s, d)
```

```python
import functools

import jax
import jax.numpy as jnp
from jax import lax
from jax.experimental import pallas as pl
from jax.experimental.pallas import tpu as pltpu

F32 = jnp.float32
BF16 = jnp.bfloat16

D_MODEL = 1024
SEQ = 16384
DEPTH = 4
CHUNK = 64
EPS = 1e-6
N_A_LAYERS = DEPTH // 2
N_B_LAYERS = DEPTH - N_A_LAYERS

A_HEADS = 6
A_HEAD_DIM = 128
A_WIDTH = A_HEADS * A_HEAD_DIM
CONV_K = 4

B_HEADS = 6
QK_NOPE = 128
QK_ROPE = 64
V_HEAD = 128
Q_LORA = 256
KV_LORA = 256
B_WIDTH = B_HEADS * V_HEAD
ROPE_THETA = 10000.0

N_MEM = 256
MEM_HEADS = 4
MEM_HEAD_DIM = 64
MEM_WIDTH = MEM_HEADS * MEM_HEAD_DIM

D_FF = 2816

LANES = 128
SUBLANES = 8
QK_PAD = 256
VMEM_LIMIT = 56 * 1024 * 1024

FFN_ROWS = 1024
FFN_COLS = 512
A_IN_ROWS = 512
GDN_CHUNKS = 4
GDN_ROWS = GDN_CHUNKS * CHUNK
B_IN_ROWS = 512
KV_ROWS = 512
ATT_Q = 1024
ATT_K = 512
ATT_BLOCKS = ATT_Q // ATT_K
ATT_UNROLL = 2
V_PAD = 256
LOG2E = 1.4426950408889634
ROPE_ROWS = 2048


def _params(*sem):
    return pltpu.CompilerParams(dimension_semantics=sem, vmem_limit_bytes=VMEM_LIMIT)


def _const_spec(shape):
    n = len(shape)
    return pl.BlockSpec(shape, lambda *_: (0,) * n, pipeline_mode=pl.Buffered(1))


def _layer_spec(stacked, l):
    n = stacked.ndim - 1
    return pl.BlockSpec((None,) + tuple(stacked.shape[1:]), lambda *_: (l,) + (0,) * n,
                        pipeline_mode=pl.Buffered(1))


def _rms(x, g):
    return x * lax.rsqrt(jnp.mean(x * x, axis=-1, keepdims=True) + EPS) * g


def _sigmoid(x):
    return 1.0 / (1.0 + jnp.exp(-x))


def _dot(a, b):
    return jnp.dot(a, b, preferred_element_type=F32)


def _dot_nt(a, b):
    return lax.dot_general(a, b, (((1,), (1,)), ((), ())), preferred_element_type=F32)


def _dot_tn(a, b):
    return lax.dot_general(a, b, (((0,), (0,)), ((), ())), preferred_element_type=F32)


def _softmax_rows(s):
    m = jnp.max(s, axis=-1, keepdims=True)
    p = jnp.exp(s - m)
    return p / jnp.sum(p, axis=-1, keepdims=True)


def _mem_attention(qm, kt_ref, vp_ref):
    out = None
    for h in range(MEM_HEADS):
        s = _dot(qm, kt_ref[h]) * (MEM_HEAD_DIM ** -0.5)
        p = _softmax_rows(s).astype(BF16)
        o = _dot(p, vp_ref[h])
        out = o if out is None else out + o
    return out


def _rope_kernel(pos_ref, inv_ref, cos_ref, sin_ref):
    ang = pos_ref[...].astype(F32) * inv_ref[...]
    cos_ref[...] = jnp.cos(ang)
    sin_ref[...] = jnp.sin(ang)


def _rope_tables(pos_col, inv_row):
    s = pos_col.shape[0]
    return pl.pallas_call(
        _rope_kernel,
        grid=(s // ROPE_ROWS,),
        in_specs=[pl.BlockSpec((ROPE_ROWS, 1), lambda i: (i, 0)),
                  pl.BlockSpec((1, LANES), lambda i: (0, 0))],
        out_specs=[pl.BlockSpec((ROPE_ROWS, LANES), lambda i: (i, 0))] * 2,
        out_shape=[jax.ShapeDtypeStruct((s, LANES), F32)] * 2,
        compiler_params=_params("parallel"),
        name="rope_tables",
    )(pos_col, inv_row)


def _mem_kv_kernel(mem_ref, g_ref, w_ref, o_ref):
    mn = _rms(mem_ref[...], g_ref[...]).astype(BF16)
    o_ref[0] = _dot(mn, w_ref[0])


def _mem_kv(mem2d, mem_norm_row, w_mem_kv_bf):
    return pl.pallas_call(
        _mem_kv_kernel,
        grid=(DEPTH,),
        in_specs=[pl.BlockSpec((N_MEM, D_MODEL), lambda l: (0, 0)),
                  pl.BlockSpec((1, D_MODEL), lambda l: (0, 0)),
                  pl.BlockSpec((1, D_MODEL, 2 * MEM_WIDTH), lambda l: (l, 0, 0))],
        out_specs=pl.BlockSpec((1, N_MEM, 2 * MEM_WIDTH), lambda l: (l, 0, 0)),
        out_shape=jax.ShapeDtypeStruct((DEPTH, N_MEM, 2 * MEM_WIDTH), F32),
        compiler_params=_params("parallel"),
        name="mem_kv",
    )(mem2d, mem_norm_row, w_mem_kv_bf)


def _ffn_kernel(*refs, has_proj, has_final):
    refs = list(refs)
    x_ref = refs.pop(0)
    if has_proj:
        oa_ref, om_ref, wout_ref = refs[:3]
        refs = refs[3:]
    g_ref, wgu_ref, wd_ref = refs[:3]
    refs = refs[3:]
    if has_final:
        fg_ref = refs.pop(0)
    o_ref, act_ref = refs

    x = x_ref[...]
    if has_proj:
        mix_w = oa_ref.shape[1]
        x = x + _dot(oa_ref[...], wout_ref[:mix_w]) + _dot(om_ref[...], wout_ref[mix_w:])
    xn = _rms(x, g_ref[...]).astype(BF16)
    for c in range(0, D_FF, FFN_COLS):
        w = min(FFN_COLS, D_FF - c)
        gate = _dot(xn, wgu_ref[:, c:c + w])
        up = _dot(xn, wgu_ref[:, D_FF + c:D_FF + c + w])
        act_ref[:, c:c + w] = (gate * _sigmoid(gate) * up).astype(BF16)
    y = x + 0.5 * _dot(act_ref[...], wd_ref[...])
    if has_final:
        y = _rms(y, fg_ref[...])
    o_ref[...] = y


def _ffn(x, l, norms, wgu, wd, proj=None, final_row=None):
    s = x.shape[0]
    row_spec = lambda w: pl.BlockSpec((FFN_ROWS, w), lambda i: (i, 0))
    args, specs = [x], [row_spec(D_MODEL)]
    if proj is not None:
        oa, om, wout = proj
        args += [oa, om, wout]
        specs += [row_spec(oa.shape[1]), row_spec(om.shape[1]), _layer_spec(wout, l)]
    args += [norms, wgu, wd]
    specs += [_layer_spec(norms, l), _layer_spec(wgu, l), _layer_spec(wd, l)]
    if final_row is not None:
        args.append(final_row)
        specs.append(_const_spec(final_row.shape))
    return pl.pallas_call(
        functools.partial(_ffn_kernel, has_proj=proj is not None, has_final=final_row is not None),
        grid=(s // FFN_ROWS,),
        in_specs=specs,
        out_specs=row_spec(D_MODEL),
        out_shape=jax.ShapeDtypeStruct((s, D_MODEL), F32),
        scratch_shapes=[pltpu.VMEM((FFN_ROWS, D_FF), BF16)],
        compiler_params=_params("parallel"),
        name="ffn",
    )(*args)


def _a_in_kernel(x_ref, g_ref, w_ref, conv_ref, alog_ref, dtb_ref, kt_ref, vp_ref,
                 q_ref, k_ref, v_ref, gate_ref, beta_ref, gcum_ref, om_ref, ext_ref):
    tm = A_IN_ROWS
    tail = SUBLANES
    wqkv_ref = w_ref.at[:, :3 * A_WIDTH]
    wgate_ref = w_ref.at[:, 3 * A_WIDTH:4 * A_WIDTH]
    wba_ref = w_ref.at[:, 4 * A_WIDTH:4 * A_WIDTH + 2 * LANES]
    wqm_ref = w_ref.at[:, 4 * A_WIDTH + 2 * LANES:]

    @pl.when(pl.program_id(0) == 0)
    def _():
        ext_ref[0:tail, :] = jnp.zeros((tail, 3 * A_WIDTH), F32)

    xn = _rms(x_ref[...], g_ref[...]).astype(BF16)
    ext_ref[tail:tail + tm, :] = _dot(xn, wqkv_ref[...])
    for b in range(3 * A_HEADS):
        sl = slice(LANES * b, LANES * (b + 1))
        acc = ext_ref[tail:tail + tm, sl] * conv_ref[CONV_K - 1:CONV_K, sl]
        for j in range(1, CONV_K):
            acc = acc + ext_ref[tail - j:tail - j + tm, sl] * conv_ref[CONV_K - 1 - j:CONV_K - j, sl]
        y = acc * _sigmoid(acc)
        if b < 2 * A_HEADS:
            y = y * lax.rsqrt(jnp.sum(y * y, axis=-1, keepdims=True) + EPS)
        if b < A_HEADS:
            q_ref[:, sl] = y * (A_HEAD_DIM ** -0.5)
        elif b < 2 * A_HEADS:
            k_ref[:, LANES * (b - A_HEADS):LANES * (b - A_HEADS + 1)] = y
        else:
            v_ref[:, LANES * (b - 2 * A_HEADS):LANES * (b - 2 * A_HEADS + 1)] = y
    ext_ref[0:tail, :] = ext_ref[tm:tm + tail, :]

    gate = _dot(xn, wgate_ref[...])
    gate_ref[...] = (gate * _sigmoid(gate)).astype(BF16)

    ba = _dot(xn, wba_ref[...])
    beta_ref[...] = _sigmoid(ba[:, :LANES])
    z = ba[:, LANES:] + dtb_ref[...]
    softplus = jnp.maximum(z, 0.0) + jnp.log(1.0 + jnp.exp(-jnp.abs(z)))
    g = -jnp.exp(alog_ref[...]) * softplus
    r = lax.broadcasted_iota(jnp.int32, (tm, tm), 0)
    c = lax.broadcasted_iota(jnp.int32, (tm, tm), 1)
    tri = jnp.where((c <= r) & ((c // CHUNK) == (r // CHUNK)), 1.0, 0.0).astype(BF16)
    g1 = g.astype(BF16)
    g2 = (g - g1.astype(F32)).astype(BF16)
    g3 = (g - g1.astype(F32) - g2.astype(F32)).astype(BF16)
    gcum_ref[...] = _dot(tri, g1) + _dot(tri, g2) + _dot(tri, g3)

    qm = _dot(xn, wqm_ref[...]).astype(BF16)
    om_ref[...] = _mem_attention(qm, kt_ref, vp_ref).astype(BF16)


def _a_in(x, l, i_a, norms, w_a, conv_w, alog_rows, dtb_rows, kt, vp):
    s = x.shape[0]
    tm = A_IN_ROWS
    row_spec = lambda w: pl.BlockSpec((tm, w), lambda i: (i, 0))
    consts = [norms, w_a, conv_w, alog_rows, dtb_rows, kt, vp]
    layer_of = [l, i_a, i_a, i_a, i_a, l, l]
    return pl.pallas_call(
        _a_in_kernel,
        grid=(s // tm,),
        in_specs=[row_spec(D_MODEL)] + [_layer_spec(a, j) for a, j in zip(consts, layer_of)],
        out_specs=[row_spec(A_WIDTH), row_spec(A_WIDTH), row_spec(A_WIDTH), row_spec(A_WIDTH),
                   row_spec(LANES), row_spec(LANES), row_spec(MEM_WIDTH)],
        out_shape=[jax.ShapeDtypeStruct((s, A_WIDTH), F32), jax.ShapeDtypeStruct((s, A_WIDTH), F32),
                   jax.ShapeDtypeStruct((s, A_WIDTH), F32), jax.ShapeDtypeStruct((s, A_WIDTH), BF16),
                   jax.ShapeDtypeStruct((s, LANES), F32), jax.ShapeDtypeStruct((s, LANES), F32),
                   jax.ShapeDtypeStruct((s, MEM_WIDTH), BF16)],
        scratch_shapes=[pltpu.VMEM((tm + SUBLANES, 3 * A_WIDTH), F32)],
        compiler_params=_params("arbitrary"),
        name="a_in",
    )(x, *consts)


def _gdn_kernel(q_ref, k_ref, v_ref, gate_ref, beta_ref, gcum_ref, gain_ref, o_ref, state_ref):
    @pl.when(pl.program_id(0) == 0)
    def _():
        state_ref[...] = jnp.zeros(state_ref.shape, F32)

    rows = GDN_ROWS
    gc_all = gcum_ref[...]
    gc_rows = gc_all.T
    beta_all = beta_ref[...]
    r = lax.broadcasted_iota(jnp.int32, (rows, rows), 0)
    c = lax.broadcasted_iota(jnp.int32, (rows, rows), 1)
    same = (r // CHUNK) == (c // CHUNK)
    causal = same & (c <= r)
    strict = same & (c < r)
    row_chunk = lax.broadcasted_iota(jnp.int32, (rows, A_HEAD_DIM), 0) // CHUNK

    def chunk_columns(x):
        return jnp.concatenate([jnp.where(row_chunk == ci, x, 0.0) for ci in range(GDN_CHUNKS)], axis=1)

    heads = range(A_HEADS)
    sls = [slice(A_HEAD_DIM * h, A_HEAD_DIM * (h + 1)) for h in heads]
    gcs = [gc_all[:, h:h + 1] for h in heads]
    bts = [beta_all[:, h:h + 1] for h in heads]
    g_last = [[gc[CHUNK * (ci + 1) - 1:CHUNK * (ci + 1), :] for ci in range(GDN_CHUNKS)] for gc in gcs]
    ks = [k_ref[:, sl] for sl in sls]
    kbs = [k * bt for k, bt in zip(ks, bts)]
    d1s = [_dot_nt(jnp.concatenate([kb, q_ref[:, sl]], axis=0).astype(BF16), k.astype(BF16))
           for kb, k, sl in zip(kbs, ks, sls)]
    ps, qks, xs = [], [], []
    for h in heads:
        decay = jnp.where(causal, jnp.exp(jnp.where(causal, gcs[h] - gc_rows[h:h + 1, :], 0.0)), 0.0)
        ps.append(jnp.where(strict, d1s[h][:rows] * decay, 0.0).astype(BF16))
        qks.append((d1s[h][rows:] * decay).astype(BF16))
        xs.append(jnp.concatenate([v_ref[:, sls[h]] * bts[h], kbs[h] * jnp.exp(gcs[h])], axis=1))
    sign = -1.0
    pw = 1
    while 2 * pw < CHUNK:
        ds = [_dot(ps[h], jnp.concatenate([xs[h].astype(BF16), ps[h]], axis=1)) for h in heads]
        xs = [xs[h] + sign * ds[h][:, :2 * A_HEAD_DIM] for h in heads]
        ps = [ds[h][:, 2 * A_HEAD_DIM:].astype(BF16) for h in heads]
        sign = 1.0
        pw *= 2
    x_bfs = [(xs[h] + _dot(ps[h], xs[h].astype(BF16))).astype(BF16) for h in heads]
    d2s = [_dot(qks[h], x_bfs[h]) for h in heads]
    d3s = []
    for h in heads:
        g_last_rows = jnp.concatenate([jnp.broadcast_to(g, (CHUNK, 1)) for g in g_last[h]], axis=0)
        k_dec = ks[h] * jnp.exp(g_last_rows - gcs[h])
        d3s.append(_dot_tn(chunk_columns(k_dec).astype(BF16), x_bfs[h]))
    q_effs = [(q_ref[:, sls[h]] * jnp.exp(gcs[h]) - d2s[h][:, A_HEAD_DIM:]).astype(BF16) for h in heads]
    sts = [state_ref[h] for h in heads]
    inter = [[] for _ in heads]
    for ci in range(GDN_CHUNKS):
        for h in heads:
            st_bf = sts[h].astype(BF16)
            inter[h].append(_dot(q_effs[h][CHUNK * ci:CHUNK * (ci + 1)], st_bf))
            blk = d3s[h][A_HEAD_DIM * ci:A_HEAD_DIM * (ci + 1)]
            sts[h] = (sts[h] * jnp.exp(g_last[h][ci]) + blk[:, :A_HEAD_DIM]
                      - _dot(blk[:, A_HEAD_DIM:].astype(BF16), st_bf))
    for h in heads:
        state_ref[h] = sts[h]
        out = d2s[h][:, :A_HEAD_DIM] + jnp.concatenate(inter[h], axis=0)
        o = _rms(out, gain_ref[...]) * gate_ref[:, sls[h]].astype(F32)
        o_ref[:, sls[h]] = o.astype(BF16)


def _gdn(q, k, v, gate, beta, gcum, i_a, gain_rows):
    s = q.shape[0]
    row_spec = lambda w: pl.BlockSpec((GDN_ROWS, w), lambda i: (i, 0))
    return pl.pallas_call(
        _gdn_kernel,
        grid=(s // GDN_ROWS,),
        in_specs=[row_spec(A_WIDTH)] * 4 + [row_spec(LANES)] * 2 + [_layer_spec(gain_rows, i_a)],
        out_specs=row_spec(A_WIDTH),
        out_shape=jax.ShapeDtypeStruct((s, A_WIDTH), BF16),
        scratch_shapes=[pltpu.VMEM((A_HEADS, A_HEAD_DIM, A_HEAD_DIM), F32)],
        compiler_params=_params("arbitrary"),
        name="gdn",
    )(q, k, v, gate, beta, gcum, gain_rows)


def _kv_kernel(x_ref, g_ref, wd_ref, lg_ref, wu_ref, cos_ref, sin_ref, kt_ref, v_ref):
    xn = _rms(x_ref[...], g_ref[...]).astype(BF16)
    ckr = _dot(xn, wd_ref[...])
    cn = _rms(ckr[:, :KV_LORA], lg_ref[...]).astype(BF16)
    kr = ckr[:, KV_LORA:KV_LORA + LANES] * cos_ref[...] + ckr[:, KV_LORA + LANES:] * sin_ref[...]
    kv = _dot(cn, wu_ref[...])
    ones_col = jnp.where(lax.broadcasted_iota(jnp.int32, (KV_ROWS, V_PAD - V_HEAD), 1) == 0, 1.0, 0.0)
    for h in range(B_HEADS):
        k_full = jnp.concatenate([kv[:, QK_NOPE * h:QK_NOPE * (h + 1)], kr], axis=1)
        kt_ref[h] = k_full.T.astype(BF16)
        v_h = kv[:, B_HEADS * QK_NOPE + V_HEAD * h:B_HEADS * QK_NOPE + V_HEAD * (h + 1)]
        v_ref[h] = jnp.concatenate([v_h, ones_col], axis=1).astype(BF16)


def _kv(x, norm_row, wd, lat_row, wu, cos_t, sin_t):
    s = x.shape[0]
    tm = KV_ROWS
    consts1 = [norm_row, wd, lat_row, wu]
    return pl.pallas_call(
        _kv_kernel,
        grid=(s // tm,),
        in_specs=[pl.BlockSpec((tm, D_MODEL), lambda i: (i, 0))]
        + [_const_spec(a.shape) for a in consts1]
        + [pl.BlockSpec((tm, LANES), lambda i: (i, 0))] * 2,
        out_specs=[pl.BlockSpec((B_HEADS, QK_PAD, tm), lambda i: (0, 0, i)),
                   pl.BlockSpec((B_HEADS, tm, V_PAD), lambda i: (0, i, 0))],
        out_shape=[jax.ShapeDtypeStruct((B_HEADS, QK_PAD, s), BF16),
                   jax.ShapeDtypeStruct((B_HEADS, s, V_PAD), BF16)],
        compiler_params=_params("parallel"),
        name="mla_kv",
    )(x, *consts1, cos_t, sin_t)


def _b_in_kernel(x_ref, g_ref, win_ref, qg_ref, wuq_ref, wrot_ref, cos_ref, sin_ref, kt_ref, vp_ref,
                 q_ref, om_ref):
    scale = (QK_NOPE + QK_ROPE) ** -0.5 * LOG2E
    xn = _rms(x_ref[...], g_ref[...]).astype(BF16)
    h_in = _dot(xn, win_ref[...])
    cqn = _rms(h_in[:, :Q_LORA], qg_ref[...]).astype(BF16)
    qa = _dot(cqn, wuq_ref[...])
    qb = _dot(cqn, wrot_ref[...])
    cos_t = cos_ref[...]
    sin_t = sin_ref[...]
    for h in range(B_HEADS):
        q_ref[h, :, :QK_NOPE] = (qa[:, QK_PAD * h:QK_PAD * h + QK_NOPE] * scale).astype(BF16)
        hi = qa[:, QK_PAD * h + QK_NOPE:QK_PAD * (h + 1)] * cos_t + qb[:, LANES * h:LANES * (h + 1)] * sin_t
        q_ref[h, :, QK_NOPE:] = (hi * scale).astype(BF16)
    qm = h_in[:, Q_LORA:].astype(BF16)
    om_ref[...] = _mem_attention(qm, kt_ref, vp_ref).astype(BF16)


def _b_in(x, l, j_b, norms, win, qg_rows, wuq, wrot, cos_t, sin_t, kt, vp):
    s = x.shape[0]
    tm = B_IN_ROWS
    consts1 = [norms, win, qg_rows, wuq, wrot]
    consts2 = [kt, vp]
    return pl.pallas_call(
        _b_in_kernel,
        grid=(s // tm,),
        in_specs=[pl.BlockSpec((tm, D_MODEL), lambda i: (i, 0))]
        + [_layer_spec(a, jj) for a, jj in zip(consts1, [l, j_b, j_b, j_b, j_b])]
        + [pl.BlockSpec((tm, LANES), lambda i: (i, 0))] * 2
        + [_layer_spec(a, l) for a in consts2],
        out_specs=[pl.BlockSpec((B_HEADS, tm, QK_PAD), lambda i: (0, i, 0)),
                   pl.BlockSpec((tm, MEM_WIDTH), lambda i: (i, 0))],
        out_shape=[jax.ShapeDtypeStruct((B_HEADS, s, QK_PAD), BF16),
                   jax.ShapeDtypeStruct((s, MEM_WIDTH), BF16)],
        compiler_params=_params("parallel"),
        name="b_in",
    )(x, *consts1, cos_t, sin_t, *consts2)


def _attn_kernel(q_ref, kt_ref, v_ref, o_ref, m_ref, acc_ref):
    i = pl.program_id(1)
    q = q_ref[0]
    m_ref[...] = jnp.full(m_ref.shape, -jnp.inf, F32)
    acc_ref[...] = jnp.zeros(acc_ref.shape, F32)
    lane_tiles = ATT_K // LANES

    def run(first_block, n_full, n_diag):
        m = m_ref[...]
        acc = acc_ref[...]
        n = n_full + n_diag
        starts = [pl.multiple_of((first_block + b) * ATT_K, ATT_K) for b in range(n)]
        offs = [0] * n_full + [j * ATT_K for j in range(n_diag)]
        ss = [_dot(q[off:], kt_ref[0, :, pl.ds(st, ATT_K)]) for st, off in zip(starts, offs)]
        ps, alphas = [], []
        for b, off in enumerate(offs):
            s = ss[b]
            if b >= n_full:
                qc = lax.broadcasted_iota(jnp.int32, s.shape, 0)
                kc = lax.broadcasted_iota(jnp.int32, s.shape, 1)
                s = jnp.where((kc // CHUNK) <= (qc // CHUNK), s, -jnp.inf)
            mx = s[:, :LANES]
            for t in range(1, lane_tiles):
                mx = jnp.maximum(mx, s[:, LANES * t:LANES * (t + 1)])
            mx = jnp.broadcast_to(jnp.max(mx, axis=-1, keepdims=True), mx.shape)
            m_old = m[off:]
            m_new = jnp.maximum(m_old, mx)
            alphas.append(jnp.exp2(m_old - m_new))
            ps.append(jnp.exp2(s - jnp.concatenate([m_new] * lane_tiles, axis=1)).astype(BF16))
            m = m_new if off == 0 else jnp.concatenate([m[:off], m_new], axis=0)
        for b, off in enumerate(offs):
            pv = _dot(ps[b], v_ref[0, pl.ds(starts[b], ATT_K), :])
            upd = acc[off:] * jnp.concatenate([alphas[b]] * (V_PAD // LANES), axis=1) + pv
            acc = upd if off == 0 else jnp.concatenate([acc[:off], upd], axis=0)
        m_ref[...] = m
        acc_ref[...] = acc

    big = ATT_UNROLL * ATT_BLOCKS
    lax.fori_loop(0, i // ATT_UNROLL, lambda t, _: run(t * big, big, 0), None)

    @pl.when(i % ATT_UNROLL == 1)
    def _():
        run((i - 1) * ATT_BLOCKS, ATT_BLOCKS, ATT_BLOCKS)

    @pl.when(i % ATT_UNROLL == 0)
    def _():
        run(i * ATT_BLOCKS, 0, ATT_BLOCKS)

    acc = acc_ref[...]
    o_ref[...] = (acc[:, :V_HEAD] / acc[:, V_HEAD:V_HEAD + 1]).astype(BF16)


def _attn(q, kt, v):
    h, s, _ = q.shape
    assert ATT_Q == ATT_BLOCKS * ATT_K and ATT_K % CHUNK == 0 and ATT_UNROLL == 2
    return pl.pallas_call(
        _attn_kernel,
        grid=(h, s // ATT_Q),
        in_specs=[pl.BlockSpec((1, ATT_Q, QK_PAD), lambda hh, i: (hh, i, 0)),
                  pl.BlockSpec((1, QK_PAD, s), lambda hh, i: (hh, 0, 0)),
                  pl.BlockSpec((1, s, V_PAD), lambda hh, i: (hh, 0, 0))],
        out_specs=pl.BlockSpec((ATT_Q, V_HEAD), lambda hh, i: (i, hh)),
        out_shape=jax.ShapeDtypeStruct((s, h * V_HEAD), BF16),
        scratch_shapes=[pltpu.VMEM((ATT_Q, LANES), F32), pltpu.VMEM((ATT_Q, V_PAD), F32)],
        compiler_params=_params("parallel", "arbitrary"),
        name="mla_attn",
    )(q, kt, v)


def _rot_cols(w):
    half = w.shape[-1] // 2
    return jnp.concatenate([-w[..., half:], w[..., :half]], axis=-1)


def _pad_cols(w, width):
    return jnp.pad(w, [(0, 0)] * (w.ndim - 1) + [(0, width - w.shape[-1])])


def _mem_layout(mem_kv_all):
    k = mem_kv_all[:, :, :MEM_WIDTH]
    v = mem_kv_all[:, :, MEM_WIDTH:]
    head_of = jnp.arange(MEM_WIDTH) // MEM_HEAD_DIM
    sel = (head_of[None, :] == jnp.arange(MEM_HEADS)[:, None]).astype(F32)
    kt = (jnp.swapaxes(k, 1, 2)[:, None, :, :] * sel[None, :, :, None]).astype(BF16)
    vp = (v[:, None, :, :] * sel[None, :, None, :]).astype(BF16)
    return kt, vp


def kernel(x, mem, positions, ffn1_norm, ffn1_w_gu, ffn1_w_down, mix_norm, ffn2_norm, ffn2_w_gu,
           ffn2_w_down, w_out, mem_norm, w_mem_kv, a_w_in, a_conv, a_A_log, a_dt_bias, a_out_norm,
           b_w_in, b_q_norm, b_w_uq, kv_in_norm, w_dkv, kv_lat_norm, w_ukv, final_norm):
    b, s, d = x.shape
    assert (b, s, d) == (1, SEQ, D_MODEL)
    xs = x.reshape(s, d)
    row = lambda v: v.reshape(1, -1).astype(F32)

    inv = ROPE_THETA ** (-jnp.arange(0, QK_ROPE, 2, dtype=F32) / QK_ROPE)
    inv_row = _pad_cols(jnp.concatenate([inv, inv])[None, :], LANES)
    cos_t, sin_t = _rope_tables(positions.reshape(s, 1), inv_row)

    mem_kv_all = _mem_kv(mem.reshape(N_MEM, d), row(mem_norm), w_mem_kv.astype(BF16))
    kt, vp = _mem_layout(mem_kv_all)

    rows = lambda v: v.astype(F32)[:, None, :]
    ffn1_n, ffn2_n, mix_n = rows(ffn1_norm), rows(ffn2_norm), rows(mix_norm)
    ffn1_gu, ffn1_d = ffn1_w_gu.astype(BF16), ffn1_w_down.astype(BF16)
    ffn2_gu, ffn2_d = ffn2_w_gu.astype(BF16), ffn2_w_down.astype(BF16)
    w_out_bf = w_out.astype(BF16)

    b_off = 4 * A_WIDTH
    w_a = jnp.concatenate(
        [a_w_in[:, :, :b_off], _pad_cols(a_w_in[:, :, b_off:b_off + A_HEADS], LANES),
         _pad_cols(a_w_in[:, :, b_off + A_HEADS:b_off + 2 * A_HEADS], LANES),
         a_w_in[:, :, b_off + 2 * A_HEADS:]], axis=2).astype(BF16)
    conv_w = a_conv.astype(F32)
    alog_rows = _pad_cols(rows(a_A_log), LANES)
    dtb_rows = _pad_cols(rows(a_dt_bias), LANES)
    gain_rows = rows(a_out_norm)

    for i in range(N_A_LAYERS):
        l = i
        xs = _ffn(xs, l, ffn1_n, ffn1_gu, ffn1_d)
        q, k, v, gate, beta, gcum, o_m = _a_in(xs, l, i, mix_n, w_a, conv_w, alog_rows, dtb_rows, kt, vp)
        o_a = _gdn(q, k, v, gate, beta, gcum, i, gain_rows)
        xs = _ffn(xs, l, ffn2_n, ffn2_gu, ffn2_d, proj=(o_a, o_m, w_out_bf))

    w_c = w_dkv[:, :KV_LORA]
    w_r = w_dkv[:, KV_LORA:]
    wd = jnp.concatenate([w_c, _pad_cols(w_r, LANES), _pad_cols(_rot_cols(w_r), LANES)], axis=1).astype(BF16)
    w_ukv3 = w_ukv.reshape(KV_LORA, B_HEADS, QK_NOPE + V_HEAD)
    wu = jnp.concatenate([w_ukv3[:, :, :QK_NOPE].reshape(KV_LORA, B_HEADS * QK_NOPE),
                          w_ukv3[:, :, QK_NOPE:].reshape(KV_LORA, B_HEADS * V_HEAD)], axis=1).astype(BF16)
    k_all, v_all = _kv(xs, row(kv_in_norm), wd, row(kv_lat_norm), wu, cos_t, sin_t)

    w_uq4 = b_w_uq.reshape(N_B_LAYERS, Q_LORA, B_HEADS, QK_NOPE + QK_ROPE)
    wuq = _pad_cols(w_uq4, QK_PAD).reshape(N_B_LAYERS, Q_LORA, B_HEADS * QK_PAD).astype(BF16)
    wrot = _pad_cols(_rot_cols(w_uq4[..., QK_NOPE:]), LANES).reshape(N_B_LAYERS, Q_LORA, B_HEADS * LANES).astype(BF16)
    b_win = b_w_in.astype(BF16)
    qg_rows = rows(b_q_norm)

    for j in range(N_B_LAYERS):
        l = N_A_LAYERS + j
        xs = _ffn(xs, l, ffn1_n, ffn1_gu, ffn1_d)
        q_all, o_m = _b_in(xs, l, j, mix_n, b_win, qg_rows, wuq, wrot, cos_t, sin_t, kt, vp)
        o_b = _attn(q_all, k_all, v_all)
        last = j == N_B_LAYERS - 1
        xs = _ffn(xs, l, ffn2_n, ffn2_gu, ffn2_d, proj=(o_b, o_m, w_out_bf),
                  final_row=row(final_norm) if last else None)

    return xs.reshape(b, s, d)
```

```python
import functools

import jax
import jax.numpy as jnp
from jax import lax
from jax.experimental import pallas as pl
from jax.experimental.pallas import tpu as pltpu

F32 = jnp.float32
BF16 = jnp.bfloat16

D_MODEL = 1024
SEQ = 16384
DEPTH = 4
CHUNK = 64
EPS = 1e-6
N_A_LAYERS = DEPTH // 2
N_B_LAYERS = DEPTH - N_A_LAYERS

A_HEADS = 6
A_HEAD_DIM = 128
A_WIDTH = A_HEADS * A_HEAD_DIM
CONV_K = 4

B_HEADS = 6
QK_NOPE = 128
QK_ROPE = 64
V_HEAD = 128
Q_LORA = 256
KV_LORA = 256
B_WIDTH = B_HEADS * V_HEAD
ROPE_THETA = 10000.0

N_MEM = 256
MEM_HEADS = 4
MEM_HEAD_DIM = 64
MEM_WIDTH = MEM_HEADS * MEM_HEAD_DIM

D_FF = 2816

LANES = 128
SUBLANES = 8
QK_PAD = 256
VMEM_LIMIT = 56 * 1024 * 1024

FFN_ROWS = 1024
FFN_COLS = 512
A_IN_ROWS = 512
GDN_CHUNKS = 4
GDN_ROWS = GDN_CHUNKS * CHUNK
B_IN_ROWS = 1024
KV_ROWS = 1024
ATT_Q = 1024
ATT_K = 512
ATT_BLOCKS = ATT_Q // ATT_K
ATT_UNROLL = 2
V_PAD = 256
LOG2E = 1.4426950408889634
ROPE_ROWS = 2048


def _params(*sem):
    return pltpu.CompilerParams(dimension_semantics=sem, vmem_limit_bytes=VMEM_LIMIT)


def _const_spec(shape):
    n = len(shape)
    return pl.BlockSpec(shape, lambda *_: (0,) * n, pipeline_mode=pl.Buffered(1))


def _layer_spec(stacked, l):
    n = stacked.ndim - 1
    return pl.BlockSpec((None,) + tuple(stacked.shape[1:]), lambda *_: (l,) + (0,) * n,
                        pipeline_mode=pl.Buffered(1))


def _rms(x, g):
    return x * lax.rsqrt(jnp.mean(x * x, axis=-1, keepdims=True) + EPS) * g


def _sigmoid(x):
    return 1.0 / (1.0 + jnp.exp(-x))


def _dot(a, b):
    return jnp.dot(a, b, preferred_element_type=F32)


def _dot_nt(a, b):
    return lax.dot_general(a, b, (((1,), (1,)), ((), ())), preferred_element_type=F32)


def _dot_tn(a, b):
    return lax.dot_general(a, b, (((0,), (0,)), ((), ())), preferred_element_type=F32)


def _softmax_rows(s):
    m = jnp.max(s, axis=-1, keepdims=True)
    p = jnp.exp(s - m)
    return p / jnp.sum(p, axis=-1, keepdims=True)


def _mem_attention(qm, kt_ref, vp_ref):
    out = None
    for h in range(MEM_HEADS):
        s = _dot(qm, kt_ref[h]) * (MEM_HEAD_DIM ** -0.5)
        p = _softmax_rows(s).astype(BF16)
        o = _dot(p, vp_ref[h])
        out = o if out is None else out + o
    return out


def _rope_kernel(pos_ref, inv_ref, cos_ref, sin_ref):
    ang = pos_ref[...].astype(F32) * inv_ref[...]
    cos_ref[...] = jnp.cos(ang)
    sin_ref[...] = jnp.sin(ang)


def _rope_tables(pos_col, inv_row):
    s = pos_col.shape[0]
    return pl.pallas_call(
        _rope_kernel,
        grid=(s // ROPE_ROWS,),
        in_specs=[pl.BlockSpec((ROPE_ROWS, 1), lambda i: (i, 0)),
                  pl.BlockSpec((1, LANES), lambda i: (0, 0))],
        out_specs=[pl.BlockSpec((ROPE_ROWS, LANES), lambda i: (i, 0))] * 2,
        out_shape=[jax.ShapeDtypeStruct((s, LANES), F32)] * 2,
        compiler_params=_params("parallel"),
        name="rope_tables",
    )(pos_col, inv_row)


def _mem_kv_kernel(mem_ref, g_ref, w_ref, o_ref):
    mn = _rms(mem_ref[...], g_ref[...]).astype(BF16)
    o_ref[0] = _dot(mn, w_ref[0])


def _mem_kv(mem2d, mem_norm_row, w_mem_kv_bf):
    return pl.pallas_call(
        _mem_kv_kernel,
        grid=(DEPTH,),
        in_specs=[pl.BlockSpec((N_MEM, D_MODEL), lambda l: (0, 0)),
                  pl.BlockSpec((1, D_MODEL), lambda l: (0, 0)),
                  pl.BlockSpec((1, D_MODEL, 2 * MEM_WIDTH), lambda l: (l, 0, 0))],
        out_specs=pl.BlockSpec((1, N_MEM, 2 * MEM_WIDTH), lambda l: (l, 0, 0)),
        out_shape=jax.ShapeDtypeStruct((DEPTH, N_MEM, 2 * MEM_WIDTH), F32),
        compiler_params=_params("parallel"),
        name="mem_kv",
    )(mem2d, mem_norm_row, w_mem_kv_bf)


def _ffn_kernel(*refs, has_proj, has_final):
    refs = list(refs)
    x_ref = refs.pop(0)
    if has_proj:
        oa_ref, om_ref, wout_ref = refs[:3]
        refs = refs[3:]
    g_ref, wgu_ref, wd_ref = refs[:3]
    refs = refs[3:]
    if has_final:
        fg_ref = refs.pop(0)
    o_ref, act_ref = refs

    x = x_ref[...]
    if has_proj:
        mix_w = oa_ref.shape[1]
        x = x + _dot(oa_ref[...], wout_ref[:mix_w]) + _dot(om_ref[...], wout_ref[mix_w:])
    xn = _rms(x, g_ref[...]).astype(BF16)
    for c in range(0, D_FF, FFN_COLS):
        w = min(FFN_COLS, D_FF - c)
        gate = _dot(xn, wgu_ref[:, c:c + w])
        up = _dot(xn, wgu_ref[:, D_FF + c:D_FF + c + w])
        act_ref[:, c:c + w] = (gate * _sigmoid(gate) * up).astype(BF16)
    y = x + 0.5 * _dot(act_ref[...], wd_ref[...])
    if has_final:
        y = _rms(y, fg_ref[...])
    o_ref[...] = y


def _ffn(x, l, norms, wgu, wd, proj=None, final_row=None):
    s = x.shape[0]
    row_spec = lambda w: pl.BlockSpec((FFN_ROWS, w), lambda i: (i, 0))
    args, specs = [x], [row_spec(D_MODEL)]
    if proj is not None:
        oa, om, wout = proj
        args += [oa, om, wout]
        specs += [row_spec(oa.shape[1]), row_spec(om.shape[1]), _layer_spec(wout, l)]
    args += [norms, wgu, wd]
    specs += [_layer_spec(norms, l), _layer_spec(wgu, l), _layer_spec(wd, l)]
    if final_row is not None:
        args.append(final_row)
        specs.append(_const_spec(final_row.shape))
    return pl.pallas_call(
        functools.partial(_ffn_kernel, has_proj=proj is not None, has_final=final_row is not None),
        grid=(s // FFN_ROWS,),
        in_specs=specs,
        out_specs=row_spec(D_MODEL),
        out_shape=jax.ShapeDtypeStruct((s, D_MODEL), F32),
        scratch_shapes=[pltpu.VMEM((FFN_ROWS, D_FF), BF16)],
        compiler_params=_params("parallel"),
        name="ffn",
    )(*args)


def _a_in_kernel(x_ref, g_ref, w_ref, conv_ref, alog_ref, dtb_ref, kt_ref, vp_ref,
                 q_ref, k_ref, v_ref, gate_ref, beta_ref, gcum_ref, om_ref, ext_ref):
    tm = A_IN_ROWS
    tail = SUBLANES
    wqkv_ref = w_ref.at[:, :3 * A_WIDTH]
    wgate_ref = w_ref.at[:, 3 * A_WIDTH:4 * A_WIDTH]
    wba_ref = w_ref.at[:, 4 * A_WIDTH:4 * A_WIDTH + 2 * LANES]
    wqm_ref = w_ref.at[:, 4 * A_WIDTH + 2 * LANES:]

    @pl.when(pl.program_id(0) == 0)
    def _():
        ext_ref[0:tail, :] = jnp.zeros((tail, 3 * A_WIDTH), F32)

    xn = _rms(x_ref[...], g_ref[...]).astype(BF16)
    ext_ref[tail:tail + tm, :] = _dot(xn, wqkv_ref[...])
    for b in range(3 * A_HEADS):
        sl = slice(LANES * b, LANES * (b + 1))
        acc = ext_ref[tail:tail + tm, sl] * conv_ref[CONV_K - 1:CONV_K, sl]
        for j in range(1, CONV_K):
            acc = acc + ext_ref[tail - j:tail - j + tm, sl] * conv_ref[CONV_K - 1 - j:CONV_K - j, sl]
        y = acc * _sigmoid(acc)
        if b < 2 * A_HEADS:
            y = y * lax.rsqrt(jnp.sum(y * y, axis=-1, keepdims=True) + EPS)
        if b < A_HEADS:
            q_ref[:, sl] = y * (A_HEAD_DIM ** -0.5)
        elif b < 2 * A_HEADS:
            k_ref[:, LANES * (b - A_HEADS):LANES * (b - A_HEADS + 1)] = y
        else:
            v_ref[:, LANES * (b - 2 * A_HEADS):LANES * (b - 2 * A_HEADS + 1)] = y
    ext_ref[0:tail, :] = ext_ref[tm:tm + tail, :]

    gate = _dot(xn, wgate_ref[...])
    gate_ref[...] = (gate * _sigmoid(gate)).astype(BF16)

    ba = _dot(xn, wba_ref[...])
    beta_ref[...] = _sigmoid(ba[:, :LANES])
    z = ba[:, LANES:] + dtb_ref[...]
    softplus = jnp.maximum(z, 0.0) + jnp.log(1.0 + jnp.exp(-jnp.abs(z)))
    g = -jnp.exp(alog_ref[...]) * softplus
    r = lax.broadcasted_iota(jnp.int32, (tm, tm), 0)
    c = lax.broadcasted_iota(jnp.int32, (tm, tm), 1)
    tri = jnp.where((c <= r) & ((c // CHUNK) == (r // CHUNK)), 1.0, 0.0).astype(BF16)
    g1 = g.astype(BF16)
    g2 = (g - g1.astype(F32)).astype(BF16)
    g3 = (g - g1.astype(F32) - g2.astype(F32)).astype(BF16)
    gcum_ref[...] = _dot(tri, g1) + _dot(tri, g2) + _dot(tri, g3)

    qm = _dot(xn, wqm_ref[...]).astype(BF16)
    om_ref[...] = _mem_attention(qm, kt_ref, vp_ref).astype(BF16)


def _a_in(x, l, i_a, norms, w_a, conv_w, alog_rows, dtb_rows, kt, vp):
    s = x.shape[0]
    tm = A_IN_ROWS
    row_spec = lambda w: pl.BlockSpec((tm, w), lambda i: (i, 0))
    consts = [norms, w_a, conv_w, alog_rows, dtb_rows, kt, vp]
    layer_of = [l, i_a, i_a, i_a, i_a, l, l]
    return pl.pallas_call(
        _a_in_kernel,
        grid=(s // tm,),
        in_specs=[row_spec(D_MODEL)] + [_layer_spec(a, j) for a, j in zip(consts, layer_of)],
        out_specs=[row_spec(A_WIDTH), row_spec(A_WIDTH), row_spec(A_WIDTH), row_spec(A_WIDTH),
                   row_spec(LANES), row_spec(LANES), row_spec(MEM_WIDTH)],
        out_shape=[jax.ShapeDtypeStruct((s, A_WIDTH), F32), jax.ShapeDtypeStruct((s, A_WIDTH), F32),
                   jax.ShapeDtypeStruct((s, A_WIDTH), F32), jax.ShapeDtypeStruct((s, A_WIDTH), BF16),
                   jax.ShapeDtypeStruct((s, LANES), F32), jax.ShapeDtypeStruct((s, LANES), F32),
                   jax.ShapeDtypeStruct((s, MEM_WIDTH), BF16)],
        scratch_shapes=[pltpu.VMEM((tm + SUBLANES, 3 * A_WIDTH), F32)],
        compiler_params=_params("arbitrary"),
        name="a_in",
    )(x, *consts)


def _gdn_kernel(q_ref, k_ref, v_ref, gate_ref, beta_ref, gcum_ref, gain_ref, o_ref, state_ref):
    @pl.when(pl.program_id(0) == 0)
    def _():
        state_ref[...] = jnp.zeros(state_ref.shape, F32)

    rows = GDN_ROWS
    gc_all = gcum_ref[...]
    gc_rows = gc_all.T
    beta_all = beta_ref[...]
    r = lax.broadcasted_iota(jnp.int32, (rows, rows), 0)
    c = lax.broadcasted_iota(jnp.int32, (rows, rows), 1)
    same = (r // CHUNK) == (c // CHUNK)
    causal = same & (c <= r)
    strict = same & (c < r)
    row_chunk = lax.broadcasted_iota(jnp.int32, (rows, A_HEAD_DIM), 0) // CHUNK

    def chunk_columns(x):
        return jnp.concatenate([jnp.where(row_chunk == ci, x, 0.0) for ci in range(GDN_CHUNKS)], axis=1)

    heads = range(A_HEADS)
    sls = [slice(A_HEAD_DIM * h, A_HEAD_DIM * (h + 1)) for h in heads]
    gcs = [gc_all[:, h:h + 1] for h in heads]
    bts = [beta_all[:, h:h + 1] for h in heads]
    g_last = [[gc[CHUNK * (ci + 1) - 1:CHUNK * (ci + 1), :] for ci in range(GDN_CHUNKS)] for gc in gcs]
    ks = [k_ref[:, sl] for sl in sls]
    kbs = [k * bt for k, bt in zip(ks, bts)]
    d1s = [_dot_nt(jnp.concatenate([kb, q_ref[:, sl]], axis=0).astype(BF16), k.astype(BF16))
           for kb, k, sl in zip(kbs, ks, sls)]
    ps, qks, xs = [], [], []
    for h in heads:
        decay = jnp.where(causal, jnp.exp(jnp.where(causal, gcs[h] - gc_rows[h:h + 1, :], 0.0)), 0.0)
        ps.append(jnp.where(strict, d1s[h][:rows] * decay, 0.0).astype(BF16))
        qks.append((d1s[h][rows:] * decay).astype(BF16))
        xs.append(jnp.concatenate([v_ref[:, sls[h]] * bts[h], kbs[h] * jnp.exp(gcs[h])], axis=1))
    sign = -1.0
    pw = 1
    while 2 * pw < CHUNK:
        ds = [_dot(ps[h], jnp.concatenate([xs[h].astype(BF16), ps[h]], axis=1)) for h in heads]
        xs = [xs[h] + sign * ds[h][:, :2 * A_HEAD_DIM] for h in heads]
        ps = [ds[h][:, 2 * A_HEAD_DIM:].astype(BF16) for h in heads]
        sign = 1.0
        pw *= 2
    x_bfs = [(xs[h] + _dot(ps[h], xs[h].astype(BF16))).astype(BF16) for h in heads]
    d2s = [_dot(qks[h], x_bfs[h]) for h in heads]
    d3s = []
    for h in heads:
        g_last_rows = jnp.concatenate([jnp.broadcast_to(g, (CHUNK, 1)) for g in g_last[h]], axis=0)
        k_dec = ks[h] * jnp.exp(g_last_rows - gcs[h])
        d3s.append(_dot_tn(chunk_columns(k_dec).astype(BF16), x_bfs[h]))
    q_effs = [(q_ref[:, sls[h]] * jnp.exp(gcs[h]) - d2s[h][:, A_HEAD_DIM:]).astype(BF16) for h in heads]
    sts = [state_ref[h] for h in heads]
    inter = [[] for _ in heads]
    for ci in range(GDN_CHUNKS):
        for h in heads:
            st_bf = sts[h].astype(BF16)
            inter[h].append(_dot(q_effs[h][CHUNK * ci:CHUNK * (ci + 1)], st_bf))
            blk = d3s[h][A_HEAD_DIM * ci:A_HEAD_DIM * (ci + 1)]
            sts[h] = (sts[h] * jnp.exp(g_last[h][ci]) + blk[:, :A_HEAD_DIM]
                      - _dot(blk[:, A_HEAD_DIM:].astype(BF16), st_bf))
    for h in heads:
        state_ref[h] = sts[h]
        out = d2s[h][:, :A_HEAD_DIM] + jnp.concatenate(inter[h], axis=0)
        o = _rms(out, gain_ref[...]) * gate_ref[:, sls[h]].astype(F32)
        o_ref[:, sls[h]] = o.astype(BF16)


def _gdn(q, k, v, gate, beta, gcum, i_a, gain_rows):
    s = q.shape[0]
    row_spec = lambda w: pl.BlockSpec((GDN_ROWS, w), lambda i: (i, 0))
    return pl.pallas_call(
        _gdn_kernel,
        grid=(s // GDN_ROWS,),
        in_specs=[row_spec(A_WIDTH)] * 4 + [row_spec(LANES)] * 2 + [_layer_spec(gain_rows, i_a)],
        out_specs=row_spec(A_WIDTH),
        out_shape=jax.ShapeDtypeStruct((s, A_WIDTH), BF16),
        scratch_shapes=[pltpu.VMEM((A_HEADS, A_HEAD_DIM, A_HEAD_DIM), F32)],
        compiler_params=_params("arbitrary"),
        name="gdn",
    )(q, k, v, gate, beta, gcum, gain_rows)


def _kv_kernel(x_ref, g_ref, wd_ref, lg_ref, wu_ref, cos_ref, sin_ref, kt_ref, v_ref):
    xn = _rms(x_ref[...], g_ref[...]).astype(BF16)
    ckr = _dot(xn, wd_ref[...])
    cn = _rms(ckr[:, :KV_LORA], lg_ref[...]).astype(BF16)
    kr = ckr[:, KV_LORA:KV_LORA + LANES] * cos_ref[...] + ckr[:, KV_LORA + LANES:] * sin_ref[...]
    kv = _dot(cn, wu_ref[...])
    ones_col = jnp.where(lax.broadcasted_iota(jnp.int32, (KV_ROWS, V_PAD - V_HEAD), 1) == 0, 1.0, 0.0)
    for h in range(B_HEADS):
        k_full = jnp.concatenate([kv[:, QK_NOPE * h:QK_NOPE * (h + 1)], kr], axis=1)
        kt_ref[h] = k_full.T.astype(BF16)
        v_h = kv[:, B_HEADS * QK_NOPE + V_HEAD * h:B_HEADS * QK_NOPE + V_HEAD * (h + 1)]
        v_ref[h] = jnp.concatenate([v_h, ones_col], axis=1).astype(BF16)


def _kv(x, norm_row, wd, lat_row, wu, cos_t, sin_t):
    s = x.shape[0]
    tm = KV_ROWS
    consts1 = [norm_row, wd, lat_row, wu]
    return pl.pallas_call(
        _kv_kernel,
        grid=(s // tm,),
        in_specs=[pl.BlockSpec((tm, D_MODEL), lambda i: (i, 0))]
        + [_const_spec(a.shape) for a in consts1]
        + [pl.BlockSpec((tm, LANES), lambda i: (i, 0))] * 2,
        out_specs=[pl.BlockSpec((B_HEADS, QK_PAD, tm), lambda i: (0, 0, i)),
                   pl.BlockSpec((B_HEADS, tm, V_PAD), lambda i: (0, i, 0))],
        out_shape=[jax.ShapeDtypeStruct((B_HEADS, QK_PAD, s), BF16),
                   jax.ShapeDtypeStruct((B_HEADS, s, V_PAD), BF16)],
        compiler_params=_params("parallel"),
        name="mla_kv",
    )(x, *consts1, cos_t, sin_t)


def _b_in_kernel(x_ref, g_ref, win_ref, qg_ref, wuq_ref, wrot_ref, cos_ref, sin_ref, kt_ref, vp_ref,
                 q_ref, om_ref):
    scale = (QK_NOPE + QK_ROPE) ** -0.5 * LOG2E
    xn = _rms(x_ref[...], g_ref[...]).astype(BF16)
    h_in = _dot(xn, win_ref[...])
    cqn = _rms(h_in[:, :Q_LORA], qg_ref[...]).astype(BF16)
    qa = _dot(cqn, wuq_ref[...])
    qb = _dot(cqn, wrot_ref[...])
    cos_t = cos_ref[...]
    sin_t = sin_ref[...]
    for h in range(B_HEADS):
        q_ref[h, :, :QK_NOPE] = (qa[:, QK_PAD * h:QK_PAD * h + QK_NOPE] * scale).astype(BF16)
        hi = qa[:, QK_PAD * h + QK_NOPE:QK_PAD * (h + 1)] * cos_t + qb[:, LANES * h:LANES * (h + 1)] * sin_t
        q_ref[h, :, QK_NOPE:] = (hi * scale).astype(BF16)
    qm = h_in[:, Q_LORA:].astype(BF16)
    om_ref[...] = _mem_attention(qm, kt_ref, vp_ref).astype(BF16)


def _b_in(x, l, j_b, norms, win, qg_rows, wuq, wrot, cos_t, sin_t, kt, vp):
    s = x.shape[0]
    tm = B_IN_ROWS
    consts1 = [norms, win, qg_rows, wuq, wrot]
    consts2 = [kt, vp]
    return pl.pallas_call(
        _b_in_kernel,
        grid=(s // tm,),
        in_specs=[pl.BlockSpec((tm, D_MODEL), lambda i: (i, 0))]
        + [_layer_spec(a, jj) for a, jj in zip(consts1, [l, j_b, j_b, j_b, j_b])]
        + [pl.BlockSpec((tm, LANES), lambda i: (i, 0))] * 2
        + [_layer_spec(a, l) for a in consts2],
        out_specs=[pl.BlockSpec((B_HEADS, tm, QK_PAD), lambda i: (0, i, 0)),
                   pl.BlockSpec((tm, MEM_WIDTH), lambda i: (i, 0))],
        out_shape=[jax.ShapeDtypeStruct((B_HEADS, s, QK_PAD), BF16),
                   jax.ShapeDtypeStruct((s, MEM_WIDTH), BF16)],
        compiler_params=_params("parallel"),
        name="b_in",
    )(x, *consts1, cos_t, sin_t, *consts2)


def _attn_kernel(q_ref, kt_ref, v_ref, o_ref, m_ref, acc_ref):
    i = pl.program_id(1)
    q = q_ref[0]
    m_ref[...] = jnp.full(m_ref.shape, -jnp.inf, F32)
    acc_ref[...] = jnp.zeros(acc_ref.shape, F32)
    lane_tiles = ATT_K // LANES

    def run(first_block, n_full, n_diag):
        m = m_ref[...]
        acc = acc_ref[...]
        n = n_full + n_diag
        starts = [pl.multiple_of((first_block + b) * ATT_K, ATT_K) for b in range(n)]
        offs = [0] * n_full + [j * ATT_K for j in range(n_diag)]
        ss = [_dot(q[off:], kt_ref[0, :, pl.ds(st, ATT_K)]) for st, off in zip(starts, offs)]
        ps, alphas = [], []
        for b, off in enumerate(offs):
            s = ss[b]
            if b >= n_full:
                qc = lax.broadcasted_iota(jnp.int32, s.shape, 0)
                kc = lax.broadcasted_iota(jnp.int32, s.shape, 1)
                s = jnp.where((kc // CHUNK) <= (qc // CHUNK), s, -jnp.inf)
            mx = s[:, :LANES]
            for t in range(1, lane_tiles):
                mx = jnp.maximum(mx, s[:, LANES * t:LANES * (t + 1)])
            mx = jnp.broadcast_to(jnp.max(mx, axis=-1, keepdims=True), mx.shape)
            m_old = m[off:]
            m_new = jnp.maximum(m_old, mx)
            alphas.append(jnp.exp2(m_old - m_new))
            ps.append(jnp.exp2(s - jnp.concatenate([m_new] * lane_tiles, axis=1)).astype(BF16))
            m = m_new if off == 0 else jnp.concatenate([m[:off], m_new], axis=0)
        for b, off in enumerate(offs):
            pv = _dot(ps[b], v_ref[0, pl.ds(starts[b], ATT_K), :])
            upd = acc[off:] * jnp.concatenate([alphas[b]] * (V_PAD // LANES), axis=1) + pv
            acc = upd if off == 0 else jnp.concatenate([acc[:off], upd], axis=0)
        m_ref[...] = m
        acc_ref[...] = acc

    big = ATT_UNROLL * ATT_BLOCKS
    lax.fori_loop(0, i // ATT_UNROLL, lambda t, _: run(t * big, big, 0), None)

    @pl.when(i % ATT_UNROLL == 1)
    def _():
        run((i - 1) * ATT_BLOCKS, ATT_BLOCKS, ATT_BLOCKS)

    @pl.when(i % ATT_UNROLL == 0)
    def _():
        run(i * ATT_BLOCKS, 0, ATT_BLOCKS)

    acc = acc_ref[...]
    o_ref[...] = (acc[:, :V_HEAD] / acc[:, V_HEAD:V_HEAD + 1]).astype(BF16)


def _attn(q, kt, v):
    h, s, _ = q.shape
    assert ATT_Q == ATT_BLOCKS * ATT_K and ATT_K % CHUNK == 0 and ATT_UNROLL == 2
    return pl.pallas_call(
        _attn_kernel,
        grid=(h, s // ATT_Q),
        in_specs=[pl.BlockSpec((1, ATT_Q, QK_PAD), lambda hh, i: (hh, i, 0)),
                  pl.BlockSpec((1, QK_PAD, s), lambda hh, i: (hh, 0, 0)),
                  pl.BlockSpec((1, s, V_PAD), lambda hh, i: (hh, 0, 0))],
        out_specs=pl.BlockSpec((ATT_Q, V_HEAD), lambda hh, i: (i, hh)),
        out_shape=jax.ShapeDtypeStruct((s, h * V_HEAD), BF16),
        scratch_shapes=[pltpu.VMEM((ATT_Q, LANES), F32), pltpu.VMEM((ATT_Q, V_PAD), F32)],
        compiler_params=_params("parallel", "arbitrary"),
        name="mla_attn",
    )(q, kt, v)


def _rot_cols(w):
    half = w.shape[-1] // 2
    return jnp.concatenate([-w[..., half:], w[..., :half]], axis=-1)


def _pad_cols(w, width):
    return jnp.pad(w, [(0, 0)] * (w.ndim - 1) + [(0, width - w.shape[-1])])


def _mem_layout(mem_kv_all):
    k = mem_kv_all[:, :, :MEM_WIDTH]
    v = mem_kv_all[:, :, MEM_WIDTH:]
    head_of = jnp.arange(MEM_WIDTH) // MEM_HEAD_DIM
    sel = (head_of[None, :] == jnp.arange(MEM_HEADS)[:, None]).astype(F32)
    kt = (jnp.swapaxes(k, 1, 2)[:, None, :, :] * sel[None, :, :, None]).astype(BF16)
    vp = (v[:, None, :, :] * sel[None, :, None, :]).astype(BF16)
    return kt, vp


def kernel(x, mem, positions, ffn1_norm, ffn1_w_gu, ffn1_w_down, mix_norm, ffn2_norm, ffn2_w_gu,
           ffn2_w_down, w_out, mem_norm, w_mem_kv, a_w_in, a_conv, a_A_log, a_dt_bias, a_out_norm,
           b_w_in, b_q_norm, b_w_uq, kv_in_norm, w_dkv, kv_lat_norm, w_ukv, final_norm):
    b, s, d = x.shape
    assert (b, s, d) == (1, SEQ, D_MODEL)
    xs = x.reshape(s, d)
    row = lambda v: v.reshape(1, -1).astype(F32)

    inv = ROPE_THETA ** (-jnp.arange(0, QK_ROPE, 2, dtype=F32) / QK_ROPE)
    inv_row = _pad_cols(jnp.concatenate([inv, inv])[None, :], LANES)
    cos_t, sin_t = _rope_tables(positions.reshape(s, 1), inv_row)

    mem_kv_all = _mem_kv(mem.reshape(N_MEM, d), row(mem_norm), w_mem_kv.astype(BF16))
    kt, vp = _mem_layout(mem_kv_all)

    rows = lambda v: v.astype(F32)[:, None, :]
    ffn1_n, ffn2_n, mix_n = rows(ffn1_norm), rows(ffn2_norm), rows(mix_norm)
    ffn1_gu, ffn1_d = ffn1_w_gu.astype(BF16), ffn1_w_down.astype(BF16)
    ffn2_gu, ffn2_d = ffn2_w_gu.astype(BF16), ffn2_w_down.astype(BF16)
    w_out_bf = w_out.astype(BF16)

    b_off = 4 * A_WIDTH
    w_a = jnp.concatenate(
        [a_w_in[:, :, :b_off], _pad_cols(a_w_in[:, :, b_off:b_off + A_HEADS], LANES),
         _pad_cols(a_w_in[:, :, b_off + A_HEADS:b_off + 2 * A_HEADS], LANES),
         a_w_in[:, :, b_off + 2 * A_HEADS:]], axis=2).astype(BF16)
    conv_w = a_conv.astype(F32)
    alog_rows = _pad_cols(rows(a_A_log), LANES)
    dtb_rows = _pad_cols(rows(a_dt_bias), LANES)
    gain_rows = rows(a_out_norm)

    for i in range(N_A_LAYERS):
        l = i
        xs = _ffn(xs, l, ffn1_n, ffn1_gu, ffn1_d)
        q, k, v, gate, beta, gcum, o_m = _a_in(xs, l, i, mix_n, w_a, conv_w, alog_rows, dtb_rows, kt, vp)
        o_a = _gdn(q, k, v, gate, beta, gcum, i, gain_rows)
        xs = _ffn(xs, l, ffn2_n, ffn2_gu, ffn2_d, proj=(o_a, o_m, w_out_bf))

    w_c = w_dkv[:, :KV_LORA]
    w_r = w_dkv[:, KV_LORA:]
    wd = jnp.concatenate([w_c, _pad_cols(w_r, LANES), _pad_cols(_rot_cols(w_r), LANES)], axis=1).astype(BF16)
    w_ukv3 = w_ukv.reshape(KV_LORA, B_HEADS, QK_NOPE + V_HEAD)
    wu = jnp.concatenate([w_ukv3[:, :, :QK_NOPE].reshape(KV_LORA, B_HEADS * QK_NOPE),
                          w_ukv3[:, :, QK_NOPE:].reshape(KV_LORA, B_HEADS * V_HEAD)], axis=1).astype(BF16)
    k_all, v_all = _kv(xs, row(kv_in_norm), wd, row(kv_lat_norm), wu, cos_t, sin_t)

    w_uq4 = b_w_uq.reshape(N_B_LAYERS, Q_LORA, B_HEADS, QK_NOPE + QK_ROPE)
    wuq = _pad_cols(w_uq4, QK_PAD).reshape(N_B_LAYERS, Q_LORA, B_HEADS * QK_PAD).astype(BF16)
    wrot = _pad_cols(_rot_cols(w_uq4[..., QK_NOPE:]), LANES).reshape(N_B_LAYERS, Q_LORA, B_HEADS * LANES).astype(BF16)
    b_win = b_w_in.astype(BF16)
    qg_rows = rows(b_q_norm)

    for j in range(N_B_LAYERS):
        l = N_A_LAYERS + j
        xs = _ffn(xs, l, ffn1_n, ffn1_gu, ffn1_d)
        q_all, o_m = _b_in(xs, l, j, mix_n, b_win, qg_rows, wuq, wrot, cos_t, sin_t, kt, vp)
        o_b = _attn(q_all, k_all, v_all)
        last = j == N_B_LAYERS - 1
        xs = _ffn(xs, l, ffn2_n, ffn2_gu, ffn2_d, proj=(o_b, o_m, w_out_bf),
                  final_row=row(final_norm) if last else None)

    return xs.reshape(b, s, d)
```

```python
import functools

import jax
import jax.numpy as jnp
from jax import lax
from jax.experimental import pallas as pl
from jax.experimental.pallas import tpu as pltpu

F32 = jnp.float32
BF16 = jnp.bfloat16

D_MODEL = 1024
SEQ = 16384
DEPTH = 4
CHUNK = 64
EPS = 1e-6
N_A_LAYERS = DEPTH // 2
N_B_LAYERS = DEPTH - N_A_LAYERS

A_HEADS = 6
A_HEAD_DIM = 128
A_WIDTH = A_HEADS * A_HEAD_DIM
CONV_K = 4

B_HEADS = 6
QK_NOPE = 128
QK_ROPE = 64
V_HEAD = 128
Q_LORA = 256
KV_LORA = 256
B_WIDTH = B_HEADS * V_HEAD
ROPE_THETA = 10000.0

N_MEM = 256
MEM_HEADS = 4
MEM_HEAD_DIM = 64
MEM_WIDTH = MEM_HEADS * MEM_HEAD_DIM

D_FF = 2816

LANES = 128
SUBLANES = 8
QK_PAD = 256
VMEM_LIMIT = 56 * 1024 * 1024

FFN_ROWS = 1024
FFN_COLS = 512
A_IN_ROWS = 512
GDN_CHUNKS = 4
GDN_ROWS = GDN_CHUNKS * CHUNK
B_IN_ROWS = 1024
KV_ROWS = 1024
ATT_Q = 1024
ATT_K = 512
ATT_BLOCKS = ATT_Q // ATT_K
ATT_UNROLL = 3
V_PAD = 256
LOG2E = 1.4426950408889634
ROPE_ROWS = 2048


def _params(*sem):
    return pltpu.CompilerParams(dimension_semantics=sem, vmem_limit_bytes=VMEM_LIMIT)


def _const_spec(shape):
    n = len(shape)
    return pl.BlockSpec(shape, lambda *_: (0,) * n, pipeline_mode=pl.Buffered(1))


def _layer_spec(stacked, l):
    n = stacked.ndim - 1
    return pl.BlockSpec((None,) + tuple(stacked.shape[1:]), lambda *_: (l,) + (0,) * n,
                        pipeline_mode=pl.Buffered(1))


def _rms(x, g):
    return x * lax.rsqrt(jnp.mean(x * x, axis=-1, keepdims=True) + EPS) * g


def _sigmoid(x):
    return 1.0 / (1.0 + jnp.exp(-x))


def _dot(a, b):
    return jnp.dot(a, b, preferred_element_type=F32)


def _dot_nt(a, b):
    return lax.dot_general(a, b, (((1,), (1,)), ((), ())), preferred_element_type=F32)


def _dot_tn(a, b):
    return lax.dot_general(a, b, (((0,), (0,)), ((), ())), preferred_element_type=F32)


def _softmax_rows(s):
    m = jnp.max(s, axis=-1, keepdims=True)
    p = jnp.exp(s - m)
    return p / jnp.sum(p, axis=-1, keepdims=True)


def _mem_attention(qm, kt_ref, vp_ref):
    out = None
    for h in range(MEM_HEADS):
        s = _dot(qm, kt_ref[h]) * (MEM_HEAD_DIM ** -0.5)
        p = _softmax_rows(s).astype(BF16)
        o = _dot(p, vp_ref[h])
        out = o if out is None else out + o
    return out


def _rope_kernel(pos_ref, inv_ref, cos_ref, sin_ref):
    ang = pos_ref[...].astype(F32) * inv_ref[...]
    cos_ref[...] = jnp.cos(ang)
    sin_ref[...] = jnp.sin(ang)


def _rope_tables(pos_col, inv_row):
    s = pos_col.shape[0]
    return pl.pallas_call(
        _rope_kernel,
        grid=(s // ROPE_ROWS,),
        in_specs=[pl.BlockSpec((ROPE_ROWS, 1), lambda i: (i, 0)),
                  pl.BlockSpec((1, LANES), lambda i: (0, 0))],
        out_specs=[pl.BlockSpec((ROPE_ROWS, LANES), lambda i: (i, 0))] * 2,
        out_shape=[jax.ShapeDtypeStruct((s, LANES), F32)] * 2,
        compiler_params=_params("parallel"),
        name="rope_tables",
    )(pos_col, inv_row)


def _mem_kv_kernel(mem_ref, g_ref, w_ref, o_ref):
    mn = _rms(mem_ref[...], g_ref[...]).astype(BF16)
    o_ref[0] = _dot(mn, w_ref[0])


def _mem_kv(mem2d, mem_norm_row, w_mem_kv_bf):
    return pl.pallas_call(
        _mem_kv_kernel,
        grid=(DEPTH,),
        in_specs=[pl.BlockSpec((N_MEM, D_MODEL), lambda l: (0, 0)),
                  pl.BlockSpec((1, D_MODEL), lambda l: (0, 0)),
                  pl.BlockSpec((1, D_MODEL, 2 * MEM_WIDTH), lambda l: (l, 0, 0))],
        out_specs=pl.BlockSpec((1, N_MEM, 2 * MEM_WIDTH), lambda l: (l, 0, 0)),
        out_shape=jax.ShapeDtypeStruct((DEPTH, N_MEM, 2 * MEM_WIDTH), F32),
        compiler_params=_params("parallel"),
        name="mem_kv",
    )(mem2d, mem_norm_row, w_mem_kv_bf)


def _ffn_kernel(*refs, has_proj, has_final):
    refs = list(refs)
    x_ref = refs.pop(0)
    if has_proj:
        oa_ref, om_ref, wout_ref = refs[:3]
        refs = refs[3:]
    g_ref, wgu_ref, wd_ref = refs[:3]
    refs = refs[3:]
    if has_final:
        fg_ref = refs.pop(0)
    o_ref, act_ref = refs

    x = x_ref[...]
    if has_proj:
        mix_w = oa_ref.shape[1]
        x = x + _dot(oa_ref[...], wout_ref[:mix_w]) + _dot(om_ref[...], wout_ref[mix_w:])
    xn = _rms(x, g_ref[...]).astype(BF16)
    for c in range(0, D_FF, FFN_COLS):
        w = min(FFN_COLS, D_FF - c)
        gate = _dot(xn, wgu_ref[:, c:c + w])
        up = _dot(xn, wgu_ref[:, D_FF + c:D_FF + c + w])
        act_ref[:, c:c + w] = (gate * _sigmoid(gate) * up).astype(BF16)
    y = x + 0.5 * _dot(act_ref[...], wd_ref[...])
    if has_final:
        y = _rms(y, fg_ref[...])
    o_ref[...] = y


def _ffn(x, l, norms, wgu, wd, proj=None, final_row=None):
    s = x.shape[0]
    row_spec = lambda w: pl.BlockSpec((FFN_ROWS, w), lambda i: (i, 0))
    args, specs = [x], [row_spec(D_MODEL)]
    if proj is not None:
        oa, om, wout = proj
        args += [oa, om, wout]
        specs += [row_spec(oa.shape[1]), row_spec(om.shape[1]), _layer_spec(wout, l)]
    args += [norms, wgu, wd]
    specs += [_layer_spec(norms, l), _layer_spec(wgu, l), _layer_spec(wd, l)]
    if final_row is not None:
        args.append(final_row)
        specs.append(_const_spec(final_row.shape))
    return pl.pallas_call(
        functools.partial(_ffn_kernel, has_proj=proj is not None, has_final=final_row is not None),
        grid=(s // FFN_ROWS,),
        in_specs=specs,
        out_specs=row_spec(D_MODEL),
        out_shape=jax.ShapeDtypeStruct((s, D_MODEL), F32),
        scratch_shapes=[pltpu.VMEM((FFN_ROWS, D_FF), BF16)],
        compiler_params=_params("parallel"),
        name="ffn",
    )(*args)


def _a_in_kernel(x_ref, g_ref, w_ref, conv_ref, alog_ref, dtb_ref, kt_ref, vp_ref,
                 q_ref, k_ref, v_ref, gate_ref, beta_ref, gcum_ref, om_ref, ext_ref):
    tm = A_IN_ROWS
    tail = SUBLANES
    wqkv_ref = w_ref.at[:, :3 * A_WIDTH]
    wgate_ref = w_ref.at[:, 3 * A_WIDTH:4 * A_WIDTH]
    wba_ref = w_ref.at[:, 4 * A_WIDTH:4 * A_WIDTH + 2 * LANES]
    wqm_ref = w_ref.at[:, 4 * A_WIDTH + 2 * LANES:]

    @pl.when(pl.program_id(0) == 0)
    def _():
        ext_ref[0:tail, :] = jnp.zeros((tail, 3 * A_WIDTH), F32)

    xn = _rms(x_ref[...], g_ref[...]).astype(BF16)
    ext_ref[tail:tail + tm, :] = _dot(xn, wqkv_ref[...])
    for b in range(3 * A_HEADS):
        sl = slice(LANES * b, LANES * (b + 1))
        acc = ext_ref[tail:tail + tm, sl] * conv_ref[CONV_K - 1:CONV_K, sl]
        for j in range(1, CONV_K):
            acc = acc + ext_ref[tail - j:tail - j + tm, sl] * conv_ref[CONV_K - 1 - j:CONV_K - j, sl]
        y = acc * _sigmoid(acc)
        if b < 2 * A_HEADS:
            y = y * lax.rsqrt(jnp.sum(y * y, axis=-1, keepdims=True) + EPS)
        if b < A_HEADS:
            q_ref[:, sl] = y * (A_HEAD_DIM ** -0.5)
        elif b < 2 * A_HEADS:
            k_ref[:, LANES * (b - A_HEADS):LANES * (b - A_HEADS + 1)] = y
        else:
            v_ref[:, LANES * (b - 2 * A_HEADS):LANES * (b - 2 * A_HEADS + 1)] = y
    ext_ref[0:tail, :] = ext_ref[tm:tm + tail, :]

    gate = _dot(xn, wgate_ref[...])
    gate_ref[...] = (gate * _sigmoid(gate)).astype(BF16)

    ba = _dot(xn, wba_ref[...])
    beta_ref[...] = _sigmoid(ba[:, :LANES])
    z = ba[:, LANES:] + dtb_ref[...]
    softplus = jnp.maximum(z, 0.0) + jnp.log(1.0 + jnp.exp(-jnp.abs(z)))
    g = -jnp.exp(alog_ref[...]) * softplus
    r = lax.broadcasted_iota(jnp.int32, (tm, tm), 0)
    c = lax.broadcasted_iota(jnp.int32, (tm, tm), 1)
    tri = jnp.where((c <= r) & ((c // CHUNK) == (r // CHUNK)), 1.0, 0.0).astype(BF16)
    g1 = g.astype(BF16)
    g2 = (g - g1.astype(F32)).astype(BF16)
    g3 = (g - g1.astype(F32) - g2.astype(F32)).astype(BF16)
    gcum_ref[...] = _dot(tri, g1) + _dot(tri, g2) + _dot(tri, g3)

    qm = _dot(xn, wqm_ref[...]).astype(BF16)
    om_ref[...] = _mem_attention(qm, kt_ref, vp_ref).astype(BF16)


def _a_in(x, l, i_a, norms, w_a, conv_w, alog_rows, dtb_rows, kt, vp):
    s = x.shape[0]
    tm = A_IN_ROWS
    row_spec = lambda w: pl.BlockSpec((tm, w), lambda i: (i, 0))
    consts = [norms, w_a, conv_w, alog_rows, dtb_rows, kt, vp]
    layer_of = [l, i_a, i_a, i_a, i_a, l, l]
    return pl.pallas_call(
        _a_in_kernel,
        grid=(s // tm,),
        in_specs=[row_spec(D_MODEL)] + [_layer_spec(a, j) for a, j in zip(consts, layer_of)],
        out_specs=[row_spec(A_WIDTH), row_spec(A_WIDTH), row_spec(A_WIDTH), row_spec(A_WIDTH),
                   row_spec(LANES), row_spec(LANES), row_spec(MEM_WIDTH)],
        out_shape=[jax.ShapeDtypeStruct((s, A_WIDTH), F32), jax.ShapeDtypeStruct((s, A_WIDTH), F32),
                   jax.ShapeDtypeStruct((s, A_WIDTH), F32), jax.ShapeDtypeStruct((s, A_WIDTH), BF16),
                   jax.ShapeDtypeStruct((s, LANES), F32), jax.ShapeDtypeStruct((s, LANES), F32),
                   jax.ShapeDtypeStruct((s, MEM_WIDTH), BF16)],
        scratch_shapes=[pltpu.VMEM((tm + SUBLANES, 3 * A_WIDTH), F32)],
        compiler_params=_params("arbitrary"),
        name="a_in",
    )(x, *consts)


def _gdn_kernel(q_ref, k_ref, v_ref, gate_ref, beta_ref, gcum_ref, gain_ref, o_ref, state_ref):
    @pl.when(pl.program_id(0) == 0)
    def _():
        state_ref[...] = jnp.zeros(state_ref.shape, F32)

    rows = GDN_ROWS
    gc_all = gcum_ref[...]
    gc_rows = gc_all.T
    beta_all = beta_ref[...]
    r = lax.broadcasted_iota(jnp.int32, (rows, rows), 0)
    c = lax.broadcasted_iota(jnp.int32, (rows, rows), 1)
    same = (r // CHUNK) == (c // CHUNK)
    causal = same & (c <= r)
    strict = same & (c < r)
    row_chunk = lax.broadcasted_iota(jnp.int32, (rows, A_HEAD_DIM), 0) // CHUNK

    def chunk_columns(x):
        return jnp.concatenate([jnp.where(row_chunk == ci, x, 0.0) for ci in range(GDN_CHUNKS)], axis=1)

    heads = range(A_HEADS)
    sls = [slice(A_HEAD_DIM * h, A_HEAD_DIM * (h + 1)) for h in heads]
    gcs = [gc_all[:, h:h + 1] for h in heads]
    bts = [beta_all[:, h:h + 1] for h in heads]
    g_last = [[gc[CHUNK * (ci + 1) - 1:CHUNK * (ci + 1), :] for ci in range(GDN_CHUNKS)] for gc in gcs]
    ks = [k_ref[:, sl] for sl in sls]
    kbs = [k * bt for k, bt in zip(ks, bts)]
    d1s = [_dot_nt(jnp.concatenate([kb, q_ref[:, sl]], axis=0).astype(BF16), k.astype(BF16))
           for kb, k, sl in zip(kbs, ks, sls)]
    ps, qks, xs = [], [], []
    for h in heads:
        decay = jnp.where(causal, jnp.exp(jnp.where(causal, gcs[h] - gc_rows[h:h + 1, :], 0.0)), 0.0)
        ps.append(jnp.where(strict, d1s[h][:rows] * decay, 0.0).astype(BF16))
        qks.append((d1s[h][rows:] * decay).astype(BF16))
        xs.append(jnp.concatenate([v_ref[:, sls[h]] * bts[h], kbs[h] * jnp.exp(gcs[h])], axis=1))
    sign = -1.0
    pw = 1
    while 2 * pw < CHUNK:
        ds = [_dot(ps[h], jnp.concatenate([xs[h].astype(BF16), ps[h]], axis=1)) for h in heads]
        xs = [xs[h] + sign * ds[h][:, :2 * A_HEAD_DIM] for h in heads]
        ps = [ds[h][:, 2 * A_HEAD_DIM:].astype(BF16) for h in heads]
        sign = 1.0
        pw *= 2
    x_bfs = [(xs[h] + _dot(ps[h], xs[h].astype(BF16))).astype(BF16) for h in heads]
    d2s = [_dot(qks[h], x_bfs[h]) for h in heads]
    d3s = []
    for h in heads:
        g_last_rows = jnp.concatenate([jnp.broadcast_to(g, (CHUNK, 1)) for g in g_last[h]], axis=0)
        k_dec = ks[h] * jnp.exp(g_last_rows - gcs[h])
        d3s.append(_dot_tn(chunk_columns(k_dec).astype(BF16), x_bfs[h]))
    q_effs = [(q_ref[:, sls[h]] * jnp.exp(gcs[h]) - d2s[h][:, A_HEAD_DIM:]).astype(BF16) for h in heads]
    sts = [state_ref[h] for h in heads]
    inter = [[] for _ in heads]
    for ci in range(GDN_CHUNKS):
        for h in heads:
            st_bf = sts[h].astype(BF16)
            inter[h].append(_dot(q_effs[h][CHUNK * ci:CHUNK * (ci + 1)], st_bf))
            blk = d3s[h][A_HEAD_DIM * ci:A_HEAD_DIM * (ci + 1)]
            sts[h] = (sts[h] * jnp.exp(g_last[h][ci]) + blk[:, :A_HEAD_DIM]
                      - _dot(blk[:, A_HEAD_DIM:].astype(BF16), st_bf))
    for h in heads:
        state_ref[h] = sts[h]
        out = d2s[h][:, :A_HEAD_DIM] + jnp.concatenate(inter[h], axis=0)
        o = _rms(out, gain_ref[...]) * gate_ref[:, sls[h]].astype(F32)
        o_ref[:, sls[h]] = o.astype(BF16)


def _gdn(q, k, v, gate, beta, gcum, i_a, gain_rows):
    s = q.shape[0]
    row_spec = lambda w: pl.BlockSpec((GDN_ROWS, w), lambda i: (i, 0))
    return pl.pallas_call(
        _gdn_kernel,
        grid=(s // GDN_ROWS,),
        in_specs=[row_spec(A_WIDTH)] * 4 + [row_spec(LANES)] * 2 + [_layer_spec(gain_rows, i_a)],
        out_specs=row_spec(A_WIDTH),
        out_shape=jax.ShapeDtypeStruct((s, A_WIDTH), BF16),
        scratch_shapes=[pltpu.VMEM((A_HEADS, A_HEAD_DIM, A_HEAD_DIM), F32)],
        compiler_params=_params("arbitrary"),
        name="gdn",
    )(q, k, v, gate, beta, gcum, gain_rows)


def _kv_kernel(x_ref, g_ref, wd_ref, lg_ref, wu_ref, cos_ref, sin_ref, kt_ref, v_ref):
    xn = _rms(x_ref[...], g_ref[...]).astype(BF16)
    ckr = _dot(xn, wd_ref[...])
    cn = _rms(ckr[:, :KV_LORA], lg_ref[...]).astype(BF16)
    kr = ckr[:, KV_LORA:KV_LORA + LANES] * cos_ref[...] + ckr[:, KV_LORA + LANES:] * sin_ref[...]
    kv = _dot(cn, wu_ref[...])
    ones_col = jnp.where(lax.broadcasted_iota(jnp.int32, (KV_ROWS, V_PAD - V_HEAD), 1) == 0, 1.0, 0.0)
    for h in range(B_HEADS):
        k_full = jnp.concatenate([kv[:, QK_NOPE * h:QK_NOPE * (h + 1)], kr], axis=1)
        kt_ref[h] = k_full.T.astype(BF16)
        v_h = kv[:, B_HEADS * QK_NOPE + V_HEAD * h:B_HEADS * QK_NOPE + V_HEAD * (h + 1)]
        v_ref[h] = jnp.concatenate([v_h, ones_col], axis=1).astype(BF16)


def _kv(x, norm_row, wd, lat_row, wu, cos_t, sin_t):
    s = x.shape[0]
    tm = KV_ROWS
    consts1 = [norm_row, wd, lat_row, wu]
    return pl.pallas_call(
        _kv_kernel,
        grid=(s // tm,),
        in_specs=[pl.BlockSpec((tm, D_MODEL), lambda i: (i, 0))]
        + [_const_spec(a.shape) for a in consts1]
        + [pl.BlockSpec((tm, LANES), lambda i: (i, 0))] * 2,
        out_specs=[pl.BlockSpec((B_HEADS, QK_PAD, tm), lambda i: (0, 0, i)),
                   pl.BlockSpec((B_HEADS, tm, V_PAD), lambda i: (0, i, 0))],
        out_shape=[jax.ShapeDtypeStruct((B_HEADS, QK_PAD, s), BF16),
                   jax.ShapeDtypeStruct((B_HEADS, s, V_PAD), BF16)],
        compiler_params=_params("parallel"),
        name="mla_kv",
    )(x, *consts1, cos_t, sin_t)


def _b_in_kernel(x_ref, g_ref, win_ref, qg_ref, wuq_ref, wrot_ref, cos_ref, sin_ref, kt_ref, vp_ref,
                 q_ref, om_ref):
    scale = (QK_NOPE + QK_ROPE) ** -0.5 * LOG2E
    xn = _rms(x_ref[...], g_ref[...]).astype(BF16)
    h_in = _dot(xn, win_ref[...])
    cqn = _rms(h_in[:, :Q_LORA], qg_ref[...]).astype(BF16)
    qa = _dot(cqn, wuq_ref[...])
    qb = _dot(cqn, wrot_ref[...])
    cos_t = cos_ref[...]
    sin_t = sin_ref[...]
    for h in range(B_HEADS):
        q_ref[h, :, :QK_NOPE] = (qa[:, QK_PAD * h:QK_PAD * h + QK_NOPE] * scale).astype(BF16)
        hi = qa[:, QK_PAD * h + QK_NOPE:QK_PAD * (h + 1)] * cos_t + qb[:, LANES * h:LANES * (h + 1)] * sin_t
        q_ref[h, :, QK_NOPE:] = (hi * scale).astype(BF16)
    qm = h_in[:, Q_LORA:].astype(BF16)
    om_ref[...] = _mem_attention(qm, kt_ref, vp_ref).astype(BF16)


def _b_in(x, l, j_b, norms, win, qg_rows, wuq, wrot, cos_t, sin_t, kt, vp):
    s = x.shape[0]
    tm = B_IN_ROWS
    consts1 = [norms, win, qg_rows, wuq, wrot]
    consts2 = [kt, vp]
    return pl.pallas_call(
        _b_in_kernel,
        grid=(s // tm,),
        in_specs=[pl.BlockSpec((tm, D_MODEL), lambda i: (i, 0))]
        + [_layer_spec(a, jj) for a, jj in zip(consts1, [l, j_b, j_b, j_b, j_b])]
        + [pl.BlockSpec((tm, LANES), lambda i: (i, 0))] * 2
        + [_layer_spec(a, l) for a in consts2],
        out_specs=[pl.BlockSpec((B_HEADS, tm, QK_PAD), lambda i: (0, i, 0)),
                   pl.BlockSpec((tm, MEM_WIDTH), lambda i: (i, 0))],
        out_shape=[jax.ShapeDtypeStruct((B_HEADS, s, QK_PAD), BF16),
                   jax.ShapeDtypeStruct((s, MEM_WIDTH), BF16)],
        compiler_params=_params("parallel"),
        name="b_in",
    )(x, *consts1, cos_t, sin_t, *consts2)


def _attn_kernel(q_ref, kt_ref, v_ref, o_ref, m_ref, acc_ref):
    i = pl.program_id(1)
    q = q_ref[0]
    m_ref[...] = jnp.full(m_ref.shape, -jnp.inf, F32)
    acc_ref[...] = jnp.zeros(acc_ref.shape, F32)
    lane_tiles = ATT_K // LANES

    def run(first_block, n_full, n_diag):
        m = m_ref[...]
        acc = acc_ref[...]
        n = n_full + n_diag
        starts = [pl.multiple_of((first_block + b) * ATT_K, ATT_K) for b in range(n)]
        offs = [0] * n_full + [j * ATT_K for j in range(n_diag)]
        ss = [_dot(q[off:], kt_ref[0, :, pl.ds(st, ATT_K)]) for st, off in zip(starts, offs)]
        ps, alphas = [], []
        for b, off in enumerate(offs):
            s = ss[b]
            if b >= n_full:
                qc = lax.broadcasted_iota(jnp.int32, s.shape, 0)
                kc = lax.broadcasted_iota(jnp.int32, s.shape, 1)
                s = jnp.where((kc // CHUNK) <= (qc // CHUNK), s, -jnp.inf)
            mx = s[:, :LANES]
            for t in range(1, lane_tiles):
                mx = jnp.maximum(mx, s[:, LANES * t:LANES * (t + 1)])
            mx = jnp.broadcast_to(jnp.max(mx, axis=-1, keepdims=True), mx.shape)
            m_old = m[off:]
            m_new = jnp.maximum(m_old, mx)
            alphas.append(jnp.exp2(m_old - m_new))
            ps.append(jnp.exp2(s - jnp.concatenate([m_new] * lane_tiles, axis=1)).astype(BF16))
            m = m_new if off == 0 else jnp.concatenate([m[:off], m_new], axis=0)
        for b, off in enumerate(offs):
            pv = _dot(ps[b], v_ref[0, pl.ds(starts[b], ATT_K), :])
            upd = acc[off:] * jnp.concatenate([alphas[b]] * (V_PAD // LANES), axis=1) + pv
            acc = upd if off == 0 else jnp.concatenate([acc[:off], upd], axis=0)
        m_ref[...] = m
        acc_ref[...] = acc

    big = ATT_UNROLL * ATT_BLOCKS
    lax.fori_loop(0, i // ATT_UNROLL, lambda t, _: run(t * big, big, 0), None)

    for left in range(ATT_UNROLL):
        @pl.when(i % ATT_UNROLL == left)
        def _(left=left):
            run((i - left) * ATT_BLOCKS, left * ATT_BLOCKS, ATT_BLOCKS)

    acc = acc_ref[...]
    o_ref[...] = (acc[:, :V_HEAD] / acc[:, V_HEAD:V_HEAD + 1]).astype(BF16)


def _attn(q, kt, v):
    h, s, _ = q.shape
    assert ATT_Q == ATT_BLOCKS * ATT_K and ATT_K % CHUNK == 0
    return pl.pallas_call(
        _attn_kernel,
        grid=(h, s // ATT_Q),
        in_specs=[pl.BlockSpec((1, ATT_Q, QK_PAD), lambda hh, i: (hh, i, 0)),
                  pl.BlockSpec((1, QK_PAD, s), lambda hh, i: (hh, 0, 0)),
                  pl.BlockSpec((1, s, V_PAD), lambda hh, i: (hh, 0, 0))],
        out_specs=pl.BlockSpec((ATT_Q, V_HEAD), lambda hh, i: (i, hh)),
        out_shape=jax.ShapeDtypeStruct((s, h * V_HEAD), BF16),
        scratch_shapes=[pltpu.VMEM((ATT_Q, LANES), F32), pltpu.VMEM((ATT_Q, V_PAD), F32)],
        compiler_params=_params("parallel", "arbitrary"),
        name="mla_attn",
    )(q, kt, v)


def _rot_cols(w):
    half = w.shape[-1] // 2
    return jnp.concatenate([-w[..., half:], w[..., :half]], axis=-1)


def _pad_cols(w, width):
    return jnp.pad(w, [(0, 0)] * (w.ndim - 1) + [(0, width - w.shape[-1])])


def _mem_layout(mem_kv_all):
    k = mem_kv_all[:, :, :MEM_WIDTH]
    v = mem_kv_all[:, :, MEM_WIDTH:]
    head_of = jnp.arange(MEM_WIDTH) // MEM_HEAD_DIM
    sel = (head_of[None, :] == jnp.arange(MEM_HEADS)[:, None]).astype(F32)
    kt = (jnp.swapaxes(k, 1, 2)[:, None, :, :] * sel[None, :, :, None]).astype(BF16)
    vp = (v[:, None, :, :] * sel[None, :, None, :]).astype(BF16)
    return kt, vp


def kernel(x, mem, positions, ffn1_norm, ffn1_w_gu, ffn1_w_down, mix_norm, ffn2_norm, ffn2_w_gu,
           ffn2_w_down, w_out, mem_norm, w_mem_kv, a_w_in, a_conv, a_A_log, a_dt_bias, a_out_norm,
           b_w_in, b_q_norm, b_w_uq, kv_in_norm, w_dkv, kv_lat_norm, w_ukv, final_norm):
    b, s, d = x.shape
    assert (b, s, d) == (1, SEQ, D_MODEL)
    xs = x.reshape(s, d)
    row = lambda v: v.reshape(1, -1).astype(F32)

    inv = ROPE_THETA ** (-jnp.arange(0, QK_ROPE, 2, dtype=F32) / QK_ROPE)
    inv_row = _pad_cols(jnp.concatenate([inv, inv])[None, :], LANES)
    cos_t, sin_t = _rope_tables(positions.reshape(s, 1), inv_row)

    mem_kv_all = _mem_kv(mem.reshape(N_MEM, d), row(mem_norm), w_mem_kv.astype(BF16))
    kt, vp = _mem_layout(mem_kv_all)

    rows = lambda v: v.astype(F32)[:, None, :]
    ffn1_n, ffn2_n, mix_n = rows(ffn1_norm), rows(ffn2_norm), rows(mix_norm)
    ffn1_gu, ffn1_d = ffn1_w_gu.astype(BF16), ffn1_w_down.astype(BF16)
    ffn2_gu, ffn2_d = ffn2_w_gu.astype(BF16), ffn2_w_down.astype(BF16)
    w_out_bf = w_out.astype(BF16)

    b_off = 4 * A_WIDTH
    w_a = jnp.concatenate(
        [a_w_in[:, :, :b_off], _pad_cols(a_w_in[:, :, b_off:b_off + A_HEADS], LANES),
         _pad_cols(a_w_in[:, :, b_off + A_HEADS:b_off + 2 * A_HEADS], LANES),
         a_w_in[:, :, b_off + 2 * A_HEADS:]], axis=2).astype(BF16)
    conv_w = a_conv.astype(F32)
    alog_rows = _pad_cols(rows(a_A_log), LANES)
    dtb_rows = _pad_cols(rows(a_dt_bias), LANES)
    gain_rows = rows(a_out_norm)

    for i in range(N_A_LAYERS):
        l = i
        xs = _ffn(xs, l, ffn1_n, ffn1_gu, ffn1_d)
        q, k, v, gate, beta, gcum, o_m = _a_in(xs, l, i, mix_n, w_a, conv_w, alog_rows, dtb_rows, kt, vp)
        o_a = _gdn(q, k, v, gate, beta, gcum, i, gain_rows)
        xs = _ffn(xs, l, ffn2_n, ffn2_gu, ffn2_d, proj=(o_a, o_m, w_out_bf))

    w_c = w_dkv[:, :KV_LORA]
    w_r = w_dkv[:, KV_LORA:]
    wd = jnp.concatenate([w_c, _pad_cols(w_r, LANES), _pad_cols(_rot_cols(w_r), LANES)], axis=1).astype(BF16)
    w_ukv3 = w_ukv.reshape(KV_LORA, B_HEADS, QK_NOPE + V_HEAD)
    wu = jnp.concatenate([w_ukv3[:, :, :QK_NOPE].reshape(KV_LORA, B_HEADS * QK_NOPE),
                          w_ukv3[:, :, QK_NOPE:].reshape(KV_LORA, B_HEADS * V_HEAD)], axis=1).astype(BF16)
    k_all, v_all = _kv(xs, row(kv_in_norm), wd, row(kv_lat_norm), wu, cos_t, sin_t)

    w_uq4 = b_w_uq.reshape(N_B_LAYERS, Q_LORA, B_HEADS, QK_NOPE + QK_ROPE)
    wuq = _pad_cols(w_uq4, QK_PAD).reshape(N_B_LAYERS, Q_LORA, B_HEADS * QK_PAD).astype(BF16)
    wrot = _pad_cols(_rot_cols(w_uq4[..., QK_NOPE:]), LANES).reshape(N_B_LAYERS, Q_LORA, B_HEADS * LANES).astype(BF16)
    b_win = b_w_in.astype(BF16)
    qg_rows = rows(b_q_norm)

    for j in range(N_B_LAYERS):
        l = N_A_LAYERS + j
        xs = _ffn(xs, l, ffn1_n, ffn1_gu, ffn1_d)
        q_all, o_m = _b_in(xs, l, j, mix_n, b_win, qg_rows, wuq, wrot, cos_t, sin_t, kt, vp)
        o_b = _attn(q_all, k_all, v_all)
        last = j == N_B_LAYERS - 1
        xs = _ffn(xs, l, ffn2_n, ffn2_gu, ffn2_d, proj=(o_b, o_m, w_out_bf),
                  final_row=row(final_norm) if last else None)

    return xs.reshape(b, s, d)
```

```python
import functools

import jax
import jax.numpy as jnp
from jax import lax
from jax.experimental import pallas as pl
from jax.experimental.pallas import tpu as pltpu

F32 = jnp.float32
BF16 = jnp.bfloat16

D_MODEL = 1024
SEQ = 16384
DEPTH = 4
CHUNK = 64
EPS = 1e-6
N_A_LAYERS = DEPTH // 2
N_B_LAYERS = DEPTH - N_A_LAYERS

A_HEADS = 6
A_HEAD_DIM = 128
A_WIDTH = A_HEADS * A_HEAD_DIM
CONV_K = 4

B_HEADS = 6
QK_NOPE = 128
QK_ROPE = 64
V_HEAD = 128
Q_LORA = 256
KV_LORA = 256
B_WIDTH = B_HEADS * V_HEAD
ROPE_THETA = 10000.0

N_MEM = 256
MEM_HEADS = 4
MEM_HEAD_DIM = 64
MEM_WIDTH = MEM_HEADS * MEM_HEAD_DIM

D_FF = 2816

LANES = 128
SUBLANES = 8
QK_PAD = 256
VMEM_LIMIT = 56 * 1024 * 1024

FFN_ROWS = 1024
FFN_COLS = 512
A_IN_ROWS = 512
GDN_CHUNKS = 4
GDN_ROWS = GDN_CHUNKS * CHUNK
B_IN_ROWS = 1024
KV_ROWS = 1024
ATT_Q = 1024
ATT_K = 512
ATT_BLOCKS = ATT_Q // ATT_K
ATT_UNROLL = 4
V_PAD = 256
LOG2E = 1.4426950408889634
ROPE_ROWS = 2048


def _params(*sem):
    return pltpu.CompilerParams(dimension_semantics=sem, vmem_limit_bytes=VMEM_LIMIT)


def _const_spec(shape):
    n = len(shape)
    return pl.BlockSpec(shape, lambda *_: (0,) * n, pipeline_mode=pl.Buffered(1))


def _layer_spec(stacked, l):
    n = stacked.ndim - 1
    return pl.BlockSpec((None,) + tuple(stacked.shape[1:]), lambda *_: (l,) + (0,) * n,
                        pipeline_mode=pl.Buffered(1))


def _rms(x, g):
    return x * lax.rsqrt(jnp.mean(x * x, axis=-1, keepdims=True) + EPS) * g


def _sigmoid(x):
    return 1.0 / (1.0 + jnp.exp(-x))


def _dot(a, b):
    return jnp.dot(a, b, preferred_element_type=F32)


def _dot_nt(a, b):
    return lax.dot_general(a, b, (((1,), (1,)), ((), ())), preferred_element_type=F32)


def _dot_tn(a, b):
    return lax.dot_general(a, b, (((0,), (0,)), ((), ())), preferred_element_type=F32)


def _softmax_rows(s):
    m = jnp.max(s, axis=-1, keepdims=True)
    p = jnp.exp(s - m)
    return p / jnp.sum(p, axis=-1, keepdims=True)


def _mem_attention(qm, kt_ref, vp_ref):
    out = None
    for h in range(MEM_HEADS):
        s = _dot(qm, kt_ref[h]) * (MEM_HEAD_DIM ** -0.5)
        p = _softmax_rows(s).astype(BF16)
        o = _dot(p, vp_ref[h])
        out = o if out is None else out + o
    return out


def _rope_kernel(pos_ref, inv_ref, cos_ref, sin_ref):
    ang = pos_ref[...].astype(F32) * inv_ref[...]
    cos_ref[...] = jnp.cos(ang)
    sin_ref[...] = jnp.sin(ang)


def _rope_tables(pos_col, inv_row):
    s = pos_col.shape[0]
    return pl.pallas_call(
        _rope_kernel,
        grid=(s // ROPE_ROWS,),
        in_specs=[pl.BlockSpec((ROPE_ROWS, 1), lambda i: (i, 0)),
                  pl.BlockSpec((1, LANES), lambda i: (0, 0))],
        out_specs=[pl.BlockSpec((ROPE_ROWS, LANES), lambda i: (i, 0))] * 2,
        out_shape=[jax.ShapeDtypeStruct((s, LANES), F32)] * 2,
        compiler_params=_params("parallel"),
        name="rope_tables",
    )(pos_col, inv_row)


def _mem_kv_kernel(mem_ref, g_ref, w_ref, o_ref):
    mn = _rms(mem_ref[...], g_ref[...]).astype(BF16)
    o_ref[0] = _dot(mn, w_ref[0])


def _mem_kv(mem2d, mem_norm_row, w_mem_kv_bf):
    return pl.pallas_call(
        _mem_kv_kernel,
        grid=(DEPTH,),
        in_specs=[pl.BlockSpec((N_MEM, D_MODEL), lambda l: (0, 0)),
                  pl.BlockSpec((1, D_MODEL), lambda l: (0, 0)),
                  pl.BlockSpec((1, D_MODEL, 2 * MEM_WIDTH), lambda l: (l, 0, 0))],
        out_specs=pl.BlockSpec((1, N_MEM, 2 * MEM_WIDTH), lambda l: (l, 0, 0)),
        out_shape=jax.ShapeDtypeStruct((DEPTH, N_MEM, 2 * MEM_WIDTH), F32),
        compiler_params=_params("parallel"),
        name="mem_kv",
    )(mem2d, mem_norm_row, w_mem_kv_bf)


def _ffn_kernel(*refs, has_proj, has_final):
    refs = list(refs)
    x_ref = refs.pop(0)
    if has_proj:
        oa_ref, om_ref, wout_ref = refs[:3]
        refs = refs[3:]
    g_ref, wgu_ref, wd_ref = refs[:3]
    refs = refs[3:]
    if has_final:
        fg_ref = refs.pop(0)
    o_ref, act_ref = refs

    x = x_ref[...]
    if has_proj:
        mix_w = oa_ref.shape[1]
        x = x + _dot(oa_ref[...], wout_ref[:mix_w]) + _dot(om_ref[...], wout_ref[mix_w:])
    xn = _rms(x, g_ref[...]).astype(BF16)
    for c in range(0, D_FF, FFN_COLS):
        w = min(FFN_COLS, D_FF - c)
        gate = _dot(xn, wgu_ref[:, c:c + w])
        up = _dot(xn, wgu_ref[:, D_FF + c:D_FF + c + w])
        act_ref[:, c:c + w] = (gate * _sigmoid(gate) * up).astype(BF16)
    y = x + 0.5 * _dot(act_ref[...], wd_ref[...])
    if has_final:
        y = _rms(y, fg_ref[...])
    o_ref[...] = y


def _ffn(x, l, norms, wgu, wd, proj=None, final_row=None):
    s = x.shape[0]
    row_spec = lambda w: pl.BlockSpec((FFN_ROWS, w), lambda i: (i, 0))
    args, specs = [x], [row_spec(D_MODEL)]
    if proj is not None:
        oa, om, wout = proj
        args += [oa, om, wout]
        specs += [row_spec(oa.shape[1]), row_spec(om.shape[1]), _layer_spec(wout, l)]
    args += [norms, wgu, wd]
    specs += [_layer_spec(norms, l), _layer_spec(wgu, l), _layer_spec(wd, l)]
    if final_row is not None:
        args.append(final_row)
        specs.append(_const_spec(final_row.shape))
    return pl.pallas_call(
        functools.partial(_ffn_kernel, has_proj=proj is not None, has_final=final_row is not None),
        grid=(s // FFN_ROWS,),
        in_specs=specs,
        out_specs=row_spec(D_MODEL),
        out_shape=jax.ShapeDtypeStruct((s, D_MODEL), F32),
        scratch_shapes=[pltpu.VMEM((FFN_ROWS, D_FF), BF16)],
        compiler_params=_params("parallel"),
        name="ffn",
    )(*args)


def _a_in_kernel(x_ref, g_ref, w_ref, conv_ref, alog_ref, dtb_ref, kt_ref, vp_ref,
                 q_ref, k_ref, v_ref, gate_ref, beta_ref, gcum_ref, om_ref, ext_ref):
    tm = A_IN_ROWS
    tail = SUBLANES
    wqkv_ref = w_ref.at[:, :3 * A_WIDTH]
    wgate_ref = w_ref.at[:, 3 * A_WIDTH:4 * A_WIDTH]
    wba_ref = w_ref.at[:, 4 * A_WIDTH:4 * A_WIDTH + 2 * LANES]
    wqm_ref = w_ref.at[:, 4 * A_WIDTH + 2 * LANES:]

    @pl.when(pl.program_id(0) == 0)
    def _():
        ext_ref[0:tail, :] = jnp.zeros((tail, 3 * A_WIDTH), F32)

    xn = _rms(x_ref[...], g_ref[...]).astype(BF16)
    ext_ref[tail:tail + tm, :] = _dot(xn, wqkv_ref[...])
    for b in range(3 * A_HEADS):
        sl = slice(LANES * b, LANES * (b + 1))
        acc = ext_ref[tail:tail + tm, sl] * conv_ref[CONV_K - 1:CONV_K, sl]
        for j in range(1, CONV_K):
            acc = acc + ext_ref[tail - j:tail - j + tm, sl] * conv_ref[CONV_K - 1 - j:CONV_K - j, sl]
        y = acc * _sigmoid(acc)
        if b < 2 * A_HEADS:
            y = y * lax.rsqrt(jnp.sum(y * y, axis=-1, keepdims=True) + EPS)
        if b < A_HEADS:
            q_ref[:, sl] = y * (A_HEAD_DIM ** -0.5)
        elif b < 2 * A_HEADS:
            k_ref[:, LANES * (b - A_HEADS):LANES * (b - A_HEADS + 1)] = y
        else:
            v_ref[:, LANES * (b - 2 * A_HEADS):LANES * (b - 2 * A_HEADS + 1)] = y
    ext_ref[0:tail, :] = ext_ref[tm:tm + tail, :]

    gate = _dot(xn, wgate_ref[...])
    gate_ref[...] = (gate * _sigmoid(gate)).astype(BF16)

    ba = _dot(xn, wba_ref[...])
    beta_ref[...] = _sigmoid(ba[:, :LANES])
    z = ba[:, LANES:] + dtb_ref[...]
    softplus = jnp.maximum(z, 0.0) + jnp.log(1.0 + jnp.exp(-jnp.abs(z)))
    g = -jnp.exp(alog_ref[...]) * softplus
    r = lax.broadcasted_iota(jnp.int32, (tm, tm), 0)
    c = lax.broadcasted_iota(jnp.int32, (tm, tm), 1)
    tri = jnp.where((c <= r) & ((c // CHUNK) == (r // CHUNK)), 1.0, 0.0).astype(BF16)
    g1 = g.astype(BF16)
    g2 = (g - g1.astype(F32)).astype(BF16)
    g3 = (g - g1.astype(F32) - g2.astype(F32)).astype(BF16)
    gcum_ref[...] = _dot(tri, g1) + _dot(tri, g2) + _dot(tri, g3)

    qm = _dot(xn, wqm_ref[...]).astype(BF16)
    om_ref[...] = _mem_attention(qm, kt_ref, vp_ref).astype(BF16)


def _a_in(x, l, i_a, norms, w_a, conv_w, alog_rows, dtb_rows, kt, vp):
    s = x.shape[0]
    tm = A_IN_ROWS
    row_spec = lambda w: pl.BlockSpec((tm, w), lambda i: (i, 0))
    consts = [norms, w_a, conv_w, alog_rows, dtb_rows, kt, vp]
    layer_of = [l, i_a, i_a, i_a, i_a, l, l]
    return pl.pallas_call(
        _a_in_kernel,
        grid=(s // tm,),
        in_specs=[row_spec(D_MODEL)] + [_layer_spec(a, j) for a, j in zip(consts, layer_of)],
        out_specs=[row_spec(A_WIDTH), row_spec(A_WIDTH), row_spec(A_WIDTH), row_spec(A_WIDTH),
                   row_spec(LANES), row_spec(LANES), row_spec(MEM_WIDTH)],
        out_shape=[jax.ShapeDtypeStruct((s, A_WIDTH), F32), jax.ShapeDtypeStruct((s, A_WIDTH), F32),
                   jax.ShapeDtypeStruct((s, A_WIDTH), F32), jax.ShapeDtypeStruct((s, A_WIDTH), BF16),
                   jax.ShapeDtypeStruct((s, LANES), F32), jax.ShapeDtypeStruct((s, LANES), F32),
                   jax.ShapeDtypeStruct((s, MEM_WIDTH), BF16)],
        scratch_shapes=[pltpu.VMEM((tm + SUBLANES, 3 * A_WIDTH), F32)],
        compiler_params=_params("arbitrary"),
        name="a_in",
    )(x, *consts)


def _gdn_kernel(q_ref, k_ref, v_ref, gate_ref, beta_ref, gcum_ref, gain_ref, o_ref, state_ref):
    @pl.when(pl.program_id(0) == 0)
    def _():
        state_ref[...] = jnp.zeros(state_ref.shape, F32)

    rows = GDN_ROWS
    gc_all = gcum_ref[...]
    gc_rows = gc_all.T
    beta_all = beta_ref[...]
    r = lax.broadcasted_iota(jnp.int32, (rows, rows), 0)
    c = lax.broadcasted_iota(jnp.int32, (rows, rows), 1)
    same = (r // CHUNK) == (c // CHUNK)
    causal = same & (c <= r)
    strict = same & (c < r)
    row_chunk = lax.broadcasted_iota(jnp.int32, (rows, A_HEAD_DIM), 0) // CHUNK

    def chunk_columns(x):
        return jnp.concatenate([jnp.where(row_chunk == ci, x, 0.0) for ci in range(GDN_CHUNKS)], axis=1)

    heads = range(A_HEADS)
    sls = [slice(A_HEAD_DIM * h, A_HEAD_DIM * (h + 1)) for h in heads]
    gcs = [gc_all[:, h:h + 1] for h in heads]
    bts = [beta_all[:, h:h + 1] for h in heads]
    g_last = [[gc[CHUNK * (ci + 1) - 1:CHUNK * (ci + 1), :] for ci in range(GDN_CHUNKS)] for gc in gcs]
    ks = [k_ref[:, sl] for sl in sls]
    kbs = [k * bt for k, bt in zip(ks, bts)]
    d1s = [_dot_nt(jnp.concatenate([kb, q_ref[:, sl]], axis=0).astype(BF16), k.astype(BF16))
           for kb, k, sl in zip(kbs, ks, sls)]
    ps, qks, xs = [], [], []
    for h in heads:
        decay = jnp.where(causal, jnp.exp(jnp.where(causal, gcs[h] - gc_rows[h:h + 1, :], 0.0)), 0.0)
        ps.append(jnp.where(strict, d1s[h][:rows] * decay, 0.0).astype(BF16))
        qks.append((d1s[h][rows:] * decay).astype(BF16))
        xs.append(jnp.concatenate([v_ref[:, sls[h]] * bts[h], kbs[h] * jnp.exp(gcs[h])], axis=1))
    sign = -1.0
    pw = 1
    while 2 * pw < CHUNK:
        ds = [_dot(ps[h], jnp.concatenate([xs[h].astype(BF16), ps[h]], axis=1)) for h in heads]
        xs = [xs[h] + sign * ds[h][:, :2 * A_HEAD_DIM] for h in heads]
        ps = [ds[h][:, 2 * A_HEAD_DIM:].astype(BF16) for h in heads]
        sign = 1.0
        pw *= 2
    x_bfs = [(xs[h] + _dot(ps[h], xs[h].astype(BF16))).astype(BF16) for h in heads]
    d2s = [_dot(qks[h], x_bfs[h]) for h in heads]
    d3s = []
    for h in heads:
        g_last_rows = jnp.concatenate([jnp.broadcast_to(g, (CHUNK, 1)) for g in g_last[h]], axis=0)
        k_dec = ks[h] * jnp.exp(g_last_rows - gcs[h])
        d3s.append(_dot_tn(chunk_columns(k_dec).astype(BF16), x_bfs[h]))
    q_effs = [(q_ref[:, sls[h]] * jnp.exp(gcs[h]) - d2s[h][:, A_HEAD_DIM:]).astype(BF16) for h in heads]
    sts = [state_ref[h] for h in heads]
    inter = [[] for _ in heads]
    for ci in range(GDN_CHUNKS):
        for h in heads:
            st_bf = sts[h].astype(BF16)
            inter[h].append(_dot(q_effs[h][CHUNK * ci:CHUNK * (ci + 1)], st_bf))
            blk = d3s[h][A_HEAD_DIM * ci:A_HEAD_DIM * (ci + 1)]
            sts[h] = (sts[h] * jnp.exp(g_last[h][ci]) + blk[:, :A_HEAD_DIM]
                      - _dot(blk[:, A_HEAD_DIM:].astype(BF16), st_bf))
    for h in heads:
        state_ref[h] = sts[h]
        out = d2s[h][:, :A_HEAD_DIM] + jnp.concatenate(inter[h], axis=0)
        o = _rms(out, gain_ref[...]) * gate_ref[:, sls[h]].astype(F32)
        o_ref[:, sls[h]] = o.astype(BF16)


def _gdn(q, k, v, gate, beta, gcum, i_a, gain_rows):
    s = q.shape[0]
    row_spec = lambda w: pl.BlockSpec((GDN_ROWS, w), lambda i: (i, 0))
    return pl.pallas_call(
        _gdn_kernel,
        grid=(s // GDN_ROWS,),
        in_specs=[row_spec(A_WIDTH)] * 4 + [row_spec(LANES)] * 2 + [_layer_spec(gain_rows, i_a)],
        out_specs=row_spec(A_WIDTH),
        out_shape=jax.ShapeDtypeStruct((s, A_WIDTH), BF16),
        scratch_shapes=[pltpu.VMEM((A_HEADS, A_HEAD_DIM, A_HEAD_DIM), F32)],
        compiler_params=_params("arbitrary"),
        name="gdn",
    )(q, k, v, gate, beta, gcum, gain_rows)


def _kv_kernel(x_ref, g_ref, wd_ref, lg_ref, wu_ref, cos_ref, sin_ref, kt_ref, v_ref):
    xn = _rms(x_ref[...], g_ref[...]).astype(BF16)
    ckr = _dot(xn, wd_ref[...])
    cn = _rms(ckr[:, :KV_LORA], lg_ref[...]).astype(BF16)
    kr = ckr[:, KV_LORA:KV_LORA + LANES] * cos_ref[...] + ckr[:, KV_LORA + LANES:] * sin_ref[...]
    kv = _dot(cn, wu_ref[...])
    ones_col = jnp.where(lax.broadcasted_iota(jnp.int32, (KV_ROWS, V_PAD - V_HEAD), 1) == 0, 1.0, 0.0)
    for h in range(B_HEADS):
        k_full = jnp.concatenate([kv[:, QK_NOPE * h:QK_NOPE * (h + 1)], kr], axis=1)
        kt_ref[h] = k_full.T.astype(BF16)
        v_h = kv[:, B_HEADS * QK_NOPE + V_HEAD * h:B_HEADS * QK_NOPE + V_HEAD * (h + 1)]
        v_ref[h] = jnp.concatenate([v_h, ones_col], axis=1).astype(BF16)


def _kv(x, norm_row, wd, lat_row, wu, cos_t, sin_t):
    s = x.shape[0]
    tm = KV_ROWS
    consts1 = [norm_row, wd, lat_row, wu]
    return pl.pallas_call(
        _kv_kernel,
        grid=(s // tm,),
        in_specs=[pl.BlockSpec((tm, D_MODEL), lambda i: (i, 0))]
        + [_const_spec(a.shape) for a in consts1]
        + [pl.BlockSpec((tm, LANES), lambda i: (i, 0))] * 2,
        out_specs=[pl.BlockSpec((B_HEADS, QK_PAD, tm), lambda i: (0, 0, i)),
                   pl.BlockSpec((B_HEADS, tm, V_PAD), lambda i: (0, i, 0))],
        out_shape=[jax.ShapeDtypeStruct((B_HEADS, QK_PAD, s), BF16),
                   jax.ShapeDtypeStruct((B_HEADS, s, V_PAD), BF16)],
        compiler_params=_params("parallel"),
        name="mla_kv",
    )(x, *consts1, cos_t, sin_t)


def _b_in_kernel(x_ref, g_ref, win_ref, qg_ref, wuq_ref, wrot_ref, cos_ref, sin_ref, kt_ref, vp_ref,
                 q_ref, om_ref):
    scale = (QK_NOPE + QK_ROPE) ** -0.5 * LOG2E
    xn = _rms(x_ref[...], g_ref[...]).astype(BF16)
    h_in = _dot(xn, win_ref[...])
    cqn = _rms(h_in[:, :Q_LORA], qg_ref[...]).astype(BF16)
    qa = _dot(cqn, wuq_ref[...])
    qb = _dot(cqn, wrot_ref[...])
    cos_t = cos_ref[...]
    sin_t = sin_ref[...]
    for h in range(B_HEADS):
        q_ref[h, :, :QK_NOPE] = (qa[:, QK_PAD * h:QK_PAD * h + QK_NOPE] * scale).astype(BF16)
        hi = qa[:, QK_PAD * h + QK_NOPE:QK_PAD * (h + 1)] * cos_t + qb[:, LANES * h:LANES * (h + 1)] * sin_t
        q_ref[h, :, QK_NOPE:] = (hi * scale).astype(BF16)
    qm = h_in[:, Q_LORA:].astype(BF16)
    om_ref[...] = _mem_attention(qm, kt_ref, vp_ref).astype(BF16)


def _b_in(x, l, j_b, norms, win, qg_rows, wuq, wrot, cos_t, sin_t, kt, vp):
    s = x.shape[0]
    tm = B_IN_ROWS
    consts1 = [norms, win, qg_rows, wuq, wrot]
    consts2 = [kt, vp]
    return pl.pallas_call(
        _b_in_kernel,
        grid=(s // tm,),
        in_specs=[pl.BlockSpec((tm, D_MODEL), lambda i: (i, 0))]
        + [_layer_spec(a, jj) for a, jj in zip(consts1, [l, j_b, j_b, j_b, j_b])]
        + [pl.BlockSpec((tm, LANES), lambda i: (i, 0))] * 2
        + [_layer_spec(a, l) for a in consts2],
        out_specs=[pl.BlockSpec((B_HEADS, tm, QK_PAD), lambda i: (0, i, 0)),
                   pl.BlockSpec((tm, MEM_WIDTH), lambda i: (i, 0))],
        out_shape=[jax.ShapeDtypeStruct((B_HEADS, s, QK_PAD), BF16),
                   jax.ShapeDtypeStruct((s, MEM_WIDTH), BF16)],
        compiler_params=_params("parallel"),
        name="b_in",
    )(x, *consts1, cos_t, sin_t, *consts2)


def _attn_kernel(q_ref, kt_ref, v_ref, o_ref, m_ref, acc_ref):
    i = pl.program_id(1)
    q = q_ref[0]
    m_ref[...] = jnp.full(m_ref.shape, -jnp.inf, F32)
    acc_ref[...] = jnp.zeros(acc_ref.shape, F32)
    lane_tiles = ATT_K // LANES

    def run(first_block, n_full, n_diag):
        m = m_ref[...]
        acc = acc_ref[...]
        n = n_full + n_diag
        starts = [pl.multiple_of((first_block + b) * ATT_K, ATT_K) for b in range(n)]
        offs = [0] * n_full + [j * ATT_K for j in range(n_diag)]
        ss = [_dot(q[off:], kt_ref[0, :, pl.ds(st, ATT_K)]) for st, off in zip(starts, offs)]
        ps, alphas = [], []
        for b, off in enumerate(offs):
            s = ss[b]
            if b >= n_full:
                qc = lax.broadcasted_iota(jnp.int32, s.shape, 0)
                kc = lax.broadcasted_iota(jnp.int32, s.shape, 1)
                s = jnp.where((kc // CHUNK) <= (qc // CHUNK), s, -jnp.inf)
            mx = s[:, :LANES]
            for t in range(1, lane_tiles):
                mx = jnp.maximum(mx, s[:, LANES * t:LANES * (t + 1)])
            mx = jnp.broadcast_to(jnp.max(mx, axis=-1, keepdims=True), mx.shape)
            m_old = m[off:]
            m_new = jnp.maximum(m_old, mx)
            alphas.append(jnp.exp2(m_old - m_new))
            ps.append(jnp.exp2(s - jnp.concatenate([m_new] * lane_tiles, axis=1)).astype(BF16))
            m = m_new if off == 0 else jnp.concatenate([m[:off], m_new], axis=0)
        for b, off in enumerate(offs):
            pv = _dot(ps[b], v_ref[0, pl.ds(starts[b], ATT_K), :])
            upd = acc[off:] * jnp.concatenate([alphas[b]] * (V_PAD // LANES), axis=1) + pv
            acc = upd if off == 0 else jnp.concatenate([acc[:off], upd], axis=0)
        m_ref[...] = m
        acc_ref[...] = acc

    big = ATT_UNROLL * ATT_BLOCKS
    lax.fori_loop(0, i // ATT_UNROLL, lambda t, _: run(t * big, big, 0), None)

    for left in range(ATT_UNROLL):
        @pl.when(i % ATT_UNROLL == left)
        def _(left=left):
            run((i - left) * ATT_BLOCKS, left * ATT_BLOCKS, ATT_BLOCKS)

    acc = acc_ref[...]
    o_ref[...] = (acc[:, :V_HEAD] / acc[:, V_HEAD:V_HEAD + 1]).astype(BF16)


def _attn(q, kt, v):
    h, s, _ = q.shape
    assert ATT_Q == ATT_BLOCKS * ATT_K and ATT_K % CHUNK == 0
    return pl.pallas_call(
        _attn_kernel,
        grid=(h, s // ATT_Q),
        in_specs=[pl.BlockSpec((1, ATT_Q, QK_PAD), lambda hh, i: (hh, i, 0)),
                  pl.BlockSpec((1, QK_PAD, s), lambda hh, i: (hh, 0, 0)),
                  pl.BlockSpec((1, s, V_PAD), lambda hh, i: (hh, 0, 0))],
        out_specs=pl.BlockSpec((ATT_Q, V_HEAD), lambda hh, i: (i, hh)),
        out_shape=jax.ShapeDtypeStruct((s, h * V_HEAD), BF16),
        scratch_shapes=[pltpu.VMEM((ATT_Q, LANES), F32), pltpu.VMEM((ATT_Q, V_PAD), F32)],
        compiler_params=_params("parallel", "arbitrary"),
        name="mla_attn",
    )(q, kt, v)


def _rot_cols(w):
    half = w.shape[-1] // 2
    return jnp.concatenate([-w[..., half:], w[..., :half]], axis=-1)


def _pad_cols(w, width):
    return jnp.pad(w, [(0, 0)] * (w.ndim - 1) + [(0, width - w.shape[-1])])


def _mem_layout(mem_kv_all):
    k = mem_kv_all[:, :, :MEM_WIDTH]
    v = mem_kv_all[:, :, MEM_WIDTH:]
    head_of = jnp.arange(MEM_WIDTH) // MEM_HEAD_DIM
    sel = (head_of[None, :] == jnp.arange(MEM_HEADS)[:, None]).astype(F32)
    kt = (jnp.swapaxes(k, 1, 2)[:, None, :, :] * sel[None, :, :, None]).astype(BF16)
    vp = (v[:, None, :, :] * sel[None, :, None, :]).astype(BF16)
    return kt, vp


def kernel(x, mem, positions, ffn1_norm, ffn1_w_gu, ffn1_w_down, mix_norm, ffn2_norm, ffn2_w_gu,
           ffn2_w_down, w_out, mem_norm, w_mem_kv, a_w_in, a_conv, a_A_log, a_dt_bias, a_out_norm,
           b_w_in, b_q_norm, b_w_uq, kv_in_norm, w_dkv, kv_lat_norm, w_ukv, final_norm):
    b, s, d = x.shape
    assert (b, s, d) == (1, SEQ, D_MODEL)
    xs = x.reshape(s, d)
    row = lambda v: v.reshape(1, -1).astype(F32)

    inv = ROPE_THETA ** (-jnp.arange(0, QK_ROPE, 2, dtype=F32) / QK_ROPE)
    inv_row = _pad_cols(jnp.concatenate([inv, inv])[None, :], LANES)
    cos_t, sin_t = _rope_tables(positions.reshape(s, 1), inv_row)

    mem_kv_all = _mem_kv(mem.reshape(N_MEM, d), row(mem_norm), w_mem_kv.astype(BF16))
    kt, vp = _mem_layout(mem_kv_all)

    rows = lambda v: v.astype(F32)[:, None, :]
    ffn1_n, ffn2_n, mix_n = rows(ffn1_norm), rows(ffn2_norm), rows(mix_norm)
    ffn1_gu, ffn1_d = ffn1_w_gu.astype(BF16), ffn1_w_down.astype(BF16)
    ffn2_gu, ffn2_d = ffn2_w_gu.astype(BF16), ffn2_w_down.astype(BF16)
    w_out_bf = w_out.astype(BF16)

    b_off = 4 * A_WIDTH
    w_a = jnp.concatenate(
        [a_w_in[:, :, :b_off], _pad_cols(a_w_in[:, :, b_off:b_off + A_HEADS], LANES),
         _pad_cols(a_w_in[:, :, b_off + A_HEADS:b_off + 2 * A_HEADS], LANES),
         a_w_in[:, :, b_off + 2 * A_HEADS:]], axis=2).astype(BF16)
    conv_w = a_conv.astype(F32)
    alog_rows = _pad_cols(rows(a_A_log), LANES)
    dtb_rows = _pad_cols(rows(a_dt_bias), LANES)
    gain_rows = rows(a_out_norm)

    for i in range(N_A_LAYERS):
        l = i
        xs = _ffn(xs, l, ffn1_n, ffn1_gu, ffn1_d)
        q, k, v, gate, beta, gcum, o_m = _a_in(xs, l, i, mix_n, w_a, conv_w, alog_rows, dtb_rows, kt, vp)
        o_a = _gdn(q, k, v, gate, beta, gcum, i, gain_rows)
        xs = _ffn(xs, l, ffn2_n, ffn2_gu, ffn2_d, proj=(o_a, o_m, w_out_bf))

    w_c = w_dkv[:, :KV_LORA]
    w_r = w_dkv[:, KV_LORA:]
    wd = jnp.concatenate([w_c, _pad_cols(w_r, LANES), _pad_cols(_rot_cols(w_r), LANES)], axis=1).astype(BF16)
    w_ukv3 = w_ukv.reshape(KV_LORA, B_HEADS, QK_NOPE + V_HEAD)
    wu = jnp.concatenate([w_ukv3[:, :, :QK_NOPE].reshape(KV_LORA, B_HEADS * QK_NOPE),
                          w_ukv3[:, :, QK_NOPE:].reshape(KV_LORA, B_HEADS * V_HEAD)], axis=1).astype(BF16)
    k_all, v_all = _kv(xs, row(kv_in_norm), wd, row(kv_lat_norm), wu, cos_t, sin_t)

    w_uq4 = b_w_uq.reshape(N_B_LAYERS, Q_LORA, B_HEADS, QK_NOPE + QK_ROPE)
    wuq = _pad_cols(w_uq4, QK_PAD).reshape(N_B_LAYERS, Q_LORA, B_HEADS * QK_PAD).astype(BF16)
    wrot = _pad_cols(_rot_cols(w_uq4[..., QK_NOPE:]), LANES).reshape(N_B_LAYERS, Q_LORA, B_HEADS * LANES).astype(BF16)
    b_win = b_w_in.astype(BF16)
    qg_rows = rows(b_q_norm)

    for j in range(N_B_LAYERS):
        l = N_A_LAYERS + j
        xs = _ffn(xs, l, ffn1_n, ffn1_gu, ffn1_d)
        q_all, o_m = _b_in(xs, l, j, mix_n, b_win, qg_rows, wuq, wrot, cos_t, sin_t, kt, vp)
        o_b = _attn(q_all, k_all, v_all)
        last = j == N_B_LAYERS - 1
        xs = _ffn(xs, l, ffn2_n, ffn2_gu, ffn2_d, proj=(o_b, o_m, w_out_bf),
                  final_row=row(final_norm) if last else None)

    return xs.reshape(b, s, d)
```
